```python
import jax, jax.numpy as jnp
from jax import lax
import numpy as np

D_MODEL = 1024
BATCH = 8
SEQ = 2048
DEPTH = 1

DILATED_GROUPS = ((128, 1), (512, 4), (2048, 16))
N_ATT_GROUPS = 3
ATT_HEADS_PER_GROUP = 4
ATT_HEAD_DIM = 128
ATT_WIDTH = N_ATT_GROUPS * ATT_HEADS_PER_GROUP * ATT_HEAD_DIM
ATT_OUT_WIDTH = ATT_HEADS_PER_GROUP * ATT_HEAD_DIM
ATT_Q_BLOCK = 128
ROPE_THETA = 500000.0
ROPE_DIMS = ATT_HEAD_DIM // 4
MLSTM_HEADS = 4
MLSTM_WIDTH = D_MODEL
MLSTM_HEAD_DIM = MLSTM_WIDTH // MLSTM_HEADS
MLSTM_CHUNK = 64
CONV_WIDTH = 4
D_FF = ((8 * D_MODEL // 3 + 127) // 128) * 128
RMS_EPS = 1e-6
COL_SIZES = (ATT_WIDTH, ATT_WIDTH, ATT_WIDTH, 2 * MLSTM_WIDTH, MLSTM_WIDTH, MLSTM_WIDTH,
             MLSTM_HEADS, MLSTM_HEADS, D_MODEL, D_MODEL)
IN_WIDTH = 3 * ATT_WIDTH + 4 * MLSTM_WIDTH + 2 * MLSTM_HEADS + 2 * D_MODEL

kernel_name = 'hybrid_dilated_attn_mlstm_macaron_block'


def rmsnorm(x, g):
    xf = x.astype(jnp.float32)
    y = xf * lax.rsqrt(jnp.mean(xf * xf, axis=-1, keepdims=True) + RMS_EPS)
    return (y * g.astype(jnp.float32)).astype(x.dtype)


def swiglu(x, w_gate, w_up, w_down):
    return (jax.nn.silu(x @ w_gate) * (x @ w_up)) @ w_down


def partial_rope(x, positions):
    half = ROPE_DIMS // 2
    inv_freq = jnp.power(ROPE_THETA, -(jnp.arange(half, dtype=jnp.float32) * 2.0 / ROPE_DIMS))
    ang = positions.astype(jnp.float32)[:, None] * inv_freq[None, :]
    cos, sin = jnp.cos(ang)[:, None, :], jnp.sin(ang)[:, None, :]
    xf = x.astype(jnp.float32)
    x1, x2, xp = xf[..., :half], xf[..., half:ROPE_DIMS], xf[..., ROPE_DIMS:]
    out = jnp.concatenate([x1 * cos - x2 * sin, x2 * cos + x1 * sin, xp], axis=-1)
    return out.astype(x.dtype)


def dilated_window_attention(q, k, v, window, dilation):
    B, S, H, dh = q.shape
    L = S // dilation
    span = window // dilation
    bq = ATT_Q_BLOCK
    nblk = -(-L // bq)
    lp = nblk * bq

    def classes(t, front):
        t = t.reshape(B, L, dilation, H, dh)
        return jnp.pad(t, ((0, 0), (front, lp - L), (0, 0), (0, 0), (0, 0)))

    qb = classes(q, 0).reshape(B, nblk, bq, dilation, H, dh)
    idx = np.arange(nblk)[:, None] * bq + np.arange(bq + span)[None, :]
    kw = classes(k, span)[:, idx]
    vw = classes(v, span)[:, idx]
    s = jnp.einsum('bnirhc,bnjrhc->bnrhij', qb, kw).astype(jnp.float32) * (dh ** -0.5)
    qpos = np.arange(nblk)[:, None, None] * bq + np.arange(bq)[None, :, None]
    kpos = np.arange(nblk)[:, None, None] * bq - span + np.arange(bq + span)[None, None, :]
    dist = qpos - kpos
    valid = (dist >= 0) & (dist <= span) & (kpos >= 0)
    s = jnp.where(valid[None, :, None, None], s, -jnp.inf)
    m = jnp.max(s, axis=-1)
    p = jnp.exp(s - m[..., None])
    l = jnp.sum(p, axis=-1)
    o = jnp.einsum('bnrhij,bnjrhc->bnirhc', p, vw.astype(jnp.float32))
    o = o / jnp.moveaxis(l, -1, 2)[..., None]
    lse = jnp.moveaxis(m + jnp.log(l), -1, 2)
    o = o.reshape(B, lp, dilation, H, dh)[:, :L].reshape(B, S, H, dh)
    lse = lse.reshape(B, lp, dilation, H)[:, :L].reshape(B, S, H)
    return o, lse


def causal_depthwise_conv(x, w, b):
    S, K = x.shape[1], w.shape[0]
    xp = jnp.pad(x, ((0, 0), (K - 1, 0), (0, 0)))
    y = b + w[0] * xp[:, 0:S]
    for j in range(1, K):
        y = y + w[j] * xp[:, j:j + S]
    return y


def mlstm_chunkwise(q, k, v, i_pre, log_f):
    B, S, H, dh = q.shape
    lc = MLSTM_CHUNK
    nc = S // lc

    def chunks(t):
        return t.astype(jnp.float32).reshape(B, nc, lc, H, dh).transpose(1, 0, 3, 2, 4)

    def gchunks(t):
        return t.reshape(B, nc, lc, H).transpose(1, 0, 3, 2)

    qc, kc, vc = chunks(q), chunks(k) * (dh ** -0.5), chunks(v)
    ic, fc = gchunks(i_pre), gchunks(log_f)
    causal = np.tril(np.ones((lc, lc), dtype=bool))

    def step(carry, inp):
        C, n, m = carry
        qt, kt, vt, it, ft = inp
        b = jnp.cumsum(ft, axis=-1)
        D = jnp.where(causal, b[..., :, None] - b[..., None, :] + it[..., None, :], -jnp.inf)
        m_inter = b + m[..., None]
        m_t = jnp.maximum(m_inter, jnp.max(D, axis=-1))
        w = jnp.exp(D - m_t[..., None]) * jnp.einsum('bhtk,bhsk->bhts', qt, kt)
        inter = jnp.exp(m_inter - m_t)
        num = jnp.einsum('bhts,bhsv->bhtv', w, vt) + inter[..., None] * jnp.einsum('bhvk,bhtk->bhtv', C, qt)
        den = jnp.sum(w, axis=-1) + inter * jnp.einsum('bhk,bhtk->bht', n, qt)
        h = num / jnp.maximum(jnp.abs(den), jnp.exp(-m_t))[..., None]
        bl = b[..., -1]
        g = bl[..., None] - b + it
        m_new = jnp.maximum(bl + m, jnp.max(g, axis=-1))
        a = jnp.exp(g - m_new[..., None])
        decay = jnp.exp(bl + m - m_new)
        C_new = decay[..., None, None] * C + jnp.einsum('bhsv,bhsk->bhvk', a[..., None] * vt, kt)
        n_new = decay[..., None] * n + jnp.einsum('bhs,bhsk->bhk', a, kt)
        return (C_new, n_new, m_new), h

    init = (jnp.zeros((B, H, dh, dh), jnp.float32), jnp.zeros((B, H, dh), jnp.float32),
            jnp.zeros((B, H), jnp.float32))
    _, hs = lax.scan(step, init, (qc, kc, vc, ic, fc))
    return hs.transpose(1, 0, 3, 2, 4).reshape(B, S, H, dh)


def hybrid_mixer(h, positions, w_in, conv_w, conv_b, i_bias, f_bias, head_g,
                 w_att_branch, w_mlstm_branch, w_out):
    B, S, _ = h.shape
    proj = h @ w_in
    points = np.cumsum(np.array(COL_SIZES))[:-1].tolist()
    q_a, k_a, v_a, qk_m, v_m, o_m, i_pre, f_pre, g_a, g_m = jnp.split(proj, points, axis=-1)

    nh = N_ATT_GROUPS * ATT_HEADS_PER_GROUP
    q_a = partial_rope(q_a.reshape(B, S, nh, ATT_HEAD_DIM), positions)
    k_a = partial_rope(k_a.reshape(B, S, nh, ATT_HEAD_DIM), positions)
    q_a = q_a.reshape(B, S, N_ATT_GROUPS, ATT_HEADS_PER_GROUP, ATT_HEAD_DIM)
    k_a = k_a.reshape(B, S, N_ATT_GROUPS, ATT_HEADS_PER_GROUP, ATT_HEAD_DIM)
    v_a = v_a.reshape(B, S, N_ATT_GROUPS, ATT_HEADS_PER_GROUP, ATT_HEAD_DIM)
    outs, lses = [], []
    for g, (window, dilation) in enumerate(DILATED_GROUPS):
        o, lse = dilated_window_attention(q_a[:, :, g], k_a[:, :, g], v_a[:, :, g], window, dilation)
        outs.append(o)
        lses.append(lse)
    alpha = jax.nn.softmax(jnp.stack(lses, axis=0), axis=0)
    att = jnp.sum(alpha[..., None] * jnp.stack(outs, axis=0), axis=0)
    att = att.reshape(B, S, ATT_OUT_WIDTH).astype(h.dtype)

    qk = jax.nn.silu(causal_depthwise_conv(qk_m, conv_w, conv_b))
    q_m, k_m = jnp.split(qk, 2, axis=-1)
    shp = (B, S, MLSTM_HEADS, MLSTM_HEAD_DIM)
    ig = i_pre.astype(jnp.float32) + i_bias.astype(jnp.float32)
    lf = jax.nn.log_sigmoid(f_pre.astype(jnp.float32) + f_bias.astype(jnp.float32))
    hm = mlstm_chunkwise(q_m.reshape(shp), k_m.reshape(shp), v_m.reshape(shp), ig, lf)
    hm = hm * lax.rsqrt(jnp.mean(hm * hm, axis=-1, keepdims=True) + RMS_EPS)
    hm = hm * head_g.astype(jnp.float32).reshape(MLSTM_HEADS, MLSTM_HEAD_DIM)
    ml = (jax.nn.sigmoid(o_m.astype(jnp.float32)) * hm.reshape(B, S, MLSTM_WIDTH)).astype(h.dtype)

    merged = jax.nn.sigmoid(g_a) * (att @ w_att_branch) + jax.nn.sigmoid(g_m) * (ml @ w_mlstm_branch)
    return merged @ w_out


def setup_inputs(seed: int = 0) -> dict:
    key = jax.random.key(seed)
    ks = jax.random.split(key, 24)
    f32 = jnp.float32

    def dense(k, fan_in, shape):
        return jax.random.normal(k, (DEPTH,) + shape, f32) * (fan_in ** -0.5)

    def gain(k, n):
        return 1.0 + 0.05 * jax.random.normal(k, (DEPTH, n), f32)

    return {
        'x': jax.random.normal(ks[0], (BATCH, SEQ, D_MODEL), f32),
        'ffn1_pre_g': gain(ks[1], D_MODEL),
        'ffn1_w_gate': dense(ks[2], D_MODEL, (D_MODEL, D_FF)),
        'ffn1_w_up': dense(ks[3], D_MODEL, (D_MODEL, D_FF)),
        'ffn1_w_down': dense(ks[4], D_FF, (D_FF, D_MODEL)),
        'ffn1_post_g': gain(ks[5], D_MODEL),
        'mix_pre_g': gain(ks[6], D_MODEL),
        'w_in': dense(ks[7], D_MODEL, (D_MODEL, IN_WIDTH)),
        'conv_w': dense(ks[8], CONV_WIDTH, (CONV_WIDTH, 2 * MLSTM_WIDTH)),
        'conv_b': 0.01 * jax.random.normal(ks[9], (DEPTH, 2 * MLSTM_WIDTH), f32),
        'mlstm_i_bias': 0.1 * jax.random.normal(ks[10], (DEPTH, MLSTM_HEADS), f32),
        'mlstm_f_bias': jnp.linspace(3.0, 6.0, MLSTM_HEADS, dtype=f32)[None, :]
                        + 0.1 * jax.random.normal(ks[11], (DEPTH, MLSTM_HEADS), f32),
        'mlstm_head_g': gain(ks[12], MLSTM_WIDTH),
        'w_att_branch': dense(ks[13], ATT_OUT_WIDTH, (ATT_OUT_WIDTH, D_MODEL)),
        'w_mlstm_branch': dense(ks[14], MLSTM_WIDTH, (MLSTM_WIDTH, D_MODEL)),
        'w_out': dense(ks[15], D_MODEL, (D_MODEL, D_MODEL)),
        'mix_post_g': gain(ks[16], D_MODEL),
        'ffn2_pre_g': gain(ks[17], D_MODEL),
        'ffn2_w_gate': dense(ks[18], D_MODEL, (D_MODEL, D_FF)),
        'ffn2_w_up': dense(ks[19], D_MODEL, (D_MODEL, D_FF)),
        'ffn2_w_down': dense(ks[20], D_FF, (D_FF, D_MODEL)),
        'ffn2_post_g': gain(ks[21], D_MODEL),
    }


def reference(x, ffn1_pre_g, ffn1_w_gate, ffn1_w_up, ffn1_w_down, ffn1_post_g,
              mix_pre_g, w_in, conv_w, conv_b, mlstm_i_bias, mlstm_f_bias, mlstm_head_g,
              w_att_branch, w_mlstm_branch, w_out, mix_post_g,
              ffn2_pre_g, ffn2_w_gate, ffn2_w_up, ffn2_w_down, ffn2_post_g):
    positions = jnp.arange(x.shape[1], dtype=jnp.int32)
    for l in range(DEPTH):
        f = swiglu(rmsnorm(x, ffn1_pre_g[l]), ffn1_w_gate[l], ffn1_w_up[l], ffn1_w_down[l])
        x = x + 0.5 * rmsnorm(f, ffn1_post_g[l])
        y = hybrid_mixer(rmsnorm(x, mix_pre_g[l]), positions, w_in[l], conv_w[l], conv_b[l],
                         mlstm_i_bias[l], mlstm_f_bias[l], mlstm_head_g[l],
                         w_att_branch[l], w_mlstm_branch[l], w_out[l])
        x = x + rmsnorm(y, mix_post_g[l])
        f = swiglu(rmsnorm(x, ffn2_pre_g[l]), ffn2_w_gate[l], ffn2_w_up[l], ffn2_w_down[l])
        x = x + 0.5 * rmsnorm(f, ffn2_post_g[l])
    return x
```

```python
import functools

import numpy as np
import jax
import jax.numpy as jnp
from jax import lax
from jax.experimental import pallas as pl
from jax.experimental.pallas import tpu as pltpu

D_MODEL = 1024
N_GROUPS = 3
GROUP_DILATIONS = (1, 4, 16)
ATT_SPAN = 128
GROUP_HEADS = 4
HEAD_DIM = 128
GROUP_WIDTH = GROUP_HEADS * HEAD_DIM
ATT_WIDTH = N_GROUPS * GROUP_WIDTH
ATT_BLOCK = 128
ROPE_THETA = 500000.0
ROPE_DIMS = HEAD_DIM // 4
ROPE_HALF = ROPE_DIMS // 2
M_HEADS = 4
M_WIDTH = D_MODEL
M_HEAD_DIM = M_WIDTH // M_HEADS
M_CHUNK = 256
CONV_WIDTH = 4
CONV_HALO = 8
RMS_EPS = 1e-6
LANES = 128
VMEM_LIMIT_BYTES = 56 * 1024 * 1024
TOKEN_TILE = 512

_BF16 = jnp.bfloat16
_F32 = jnp.float32


def _rms(x):
    return x * lax.rsqrt(jnp.mean(x * x, axis=-1, keepdims=True) + RMS_EPS)


def _sigmoid(x):
    return 1.0 / (1.0 + jnp.exp(-x))


def _dot(a, b):
    return jnp.dot(a, b, preferred_element_type=_F32)


def _resident(shape):
    return pl.BlockSpec(shape, lambda *_: (0,) * len(shape), pipeline_mode=pl.Buffered(1))


def _params(*semantics):
    return pltpu.CompilerParams(dimension_semantics=semantics, vmem_limit_bytes=VMEM_LIMIT_BYTES)


def _ffn_kernel(x_ref, pre_g_ref, wg_ref, wu_ref, wd_ref, post_g_ref, o_ref):
    x = x_ref[...]
    h = (_rms(x) * pre_g_ref[...]).astype(_BF16)
    g = _dot(h, wg_ref[...])
    u = _dot(h, wu_ref[...])
    a = (g * _sigmoid(g) * u).astype(_BF16)
    f = _dot(a, wd_ref[...])
    o_ref[...] = x + 0.5 * (_rms(f) * post_g_ref[...])


def _ffn(x2d, pre_g, w_gate, w_up, w_down, post_g):
    t, d = x2d.shape
    f = w_gate.shape[1]
    row = pl.BlockSpec((TOKEN_TILE, d), lambda i: (i, 0))
    return pl.pallas_call(
        _ffn_kernel,
        grid=(t // TOKEN_TILE,),
        in_specs=[row, _resident((1, d)), _resident((d, f)), _resident((d, f)), _resident((f, d)),
                  _resident((1, d))],
        out_specs=row,
        out_shape=jax.ShapeDtypeStruct((t, d), _F32),
        compiler_params=_params("parallel"),
        name="ffn",
    )(x2d, pre_g.reshape(1, d), w_gate.astype(_BF16), w_up.astype(_BF16), w_down.astype(_BF16),
      post_g.reshape(1, d))


def _rope(x, c, s_lo, s_hi):
    from_lo = pltpu.roll(x, ROPE_HALF, 1)
    from_hi = pltpu.roll(x, HEAD_DIM - ROPE_HALF, 1)
    return x * c + from_lo * s_lo + from_hi * s_hi


def _inproj_kernel(x_ref, g_ref, w_ref, cq_ref, sq_lo_ref, sq_hi_ref, ck_ref, sk_lo_ref, sk_hi_ref,
                   qkv0_ref, qkv1_ref, qkv2_ref, qkm_ref, vm_ref, som_ref, sga_ref, sgm_ref, gates_ref):
    h = (_rms(x_ref[...]) * g_ref[...]).astype(_BF16)
    col = 0
    for qkv_ref in (qkv0_ref, qkv1_ref, qkv2_ref):
        for part, tabs in enumerate(((cq_ref, sq_lo_ref, sq_hi_ref), (ck_ref, sk_lo_ref, sk_hi_ref), None)):
            p = _dot(h, w_ref[:, col:col + GROUP_WIDTH])
            col += GROUP_WIDTH
            if tabs is None:
                qkv_ref[:, part * GROUP_WIDTH:(part + 1) * GROUP_WIDTH] = p.astype(_BF16)
                continue
            c, s_lo, s_hi = (r[...] for r in tabs)
            for hh in range(GROUP_HEADS):
                lo = hh * HEAD_DIM
                roped = _rope(p[:, lo:lo + HEAD_DIM], c, s_lo, s_hi)
                qkv_ref[:, part * GROUP_WIDTH + lo:part * GROUP_WIDTH + lo + HEAD_DIM] = roped.astype(_BF16)
    qkm_ref[...] = _dot(h, w_ref[:, col:col + 2 * M_WIDTH]).astype(_BF16)
    col += 2 * M_WIDTH
    vm_ref[...] = _dot(h, w_ref[:, col:col + M_WIDTH]).astype(_BF16)
    col += M_WIDTH
    for out_ref in (som_ref, sga_ref, sgm_ref):
        out_ref[...] = _sigmoid(_dot(h, w_ref[:, col:col + D_MODEL])).astype(_BF16)
        col += D_MODEL
    gates_ref[...] = _dot(h, w_ref[:, col:col + LANES])


def _rope_tables(seq, scale):
    inv_freq = jnp.power(ROPE_THETA, -(jnp.arange(ROPE_HALF, dtype=_F32) * 2.0 / ROPE_DIMS))
    ang = jnp.arange(seq, dtype=jnp.int32).astype(_F32)[:, None] * inv_freq[None, :]
    cos, sin = jnp.cos(ang), jnp.sin(ang)
    zeros = jnp.zeros((seq, HEAD_DIM - ROPE_DIMS), _F32)
    zero_half = jnp.zeros((seq, ROPE_HALF), _F32)
    c = jnp.concatenate([cos, cos, jnp.ones((seq, HEAD_DIM - ROPE_DIMS), _F32)], axis=-1)
    s_lo = jnp.concatenate([zero_half, sin, zeros], axis=-1)
    s_hi = jnp.concatenate([-sin, zero_half, zeros], axis=-1)
    return c * scale, s_lo * scale, s_hi * scale


def _inproj_weights(w_in):
    a, m = ATT_WIDTH, M_WIDTH
    cols = []
    for g in range(N_GROUPS):
        for part in range(3):
            lo = part * a + g * GROUP_WIDTH
            cols.append(w_in[:, lo:lo + GROUP_WIDTH])
    base = 3 * a
    cols.append(w_in[:, base:base + 4 * m])
    gate_lo = base + 4 * m
    branch_lo = gate_lo + 2 * M_HEADS
    cols.append(w_in[:, branch_lo:branch_lo + 2 * D_MODEL])
    cols.append(jnp.pad(w_in[:, gate_lo:branch_lo], ((0, 0), (0, LANES - 2 * M_HEADS))))
    return jnp.concatenate(cols, axis=1).astype(_BF16)


def _inproj(x2d, seq, pre_g, w_in):
    t, d = x2d.shape
    w = _inproj_weights(w_in)
    tiles_per_seq = seq // TOKEN_TILE
    tables = _rope_tables(seq, HEAD_DIM ** -0.5) + _rope_tables(seq, 1.0)
    row = lambda width: pl.BlockSpec((TOKEN_TILE, width), lambda i: (i, 0))
    tab = pl.BlockSpec((TOKEN_TILE, HEAD_DIM), lambda i: (i % tiles_per_seq, 0))
    widths = (ATT_WIDTH, ATT_WIDTH, ATT_WIDTH, 2 * M_WIDTH, M_WIDTH, M_WIDTH, D_MODEL, D_MODEL)
    out_shape = [jax.ShapeDtypeStruct((t, wd), _BF16) for wd in widths]
    out_shape.append(jax.ShapeDtypeStruct((t, LANES), _F32))
    return pl.pallas_call(
        _inproj_kernel,
        grid=(t // TOKEN_TILE,),
        in_specs=[row(d), _resident((1, d)), _resident(w.shape)] + [tab] * 6,
        out_specs=[row(wd) for wd in widths] + [row(LANES)],
        out_shape=out_shape,
        compiler_params=_params("parallel"),
        name="inproj",
    )(x2d, pre_g.reshape(1, d), w, *tables)


def _attn_block(q, k, v, mask):
    s = lax.dot_general(q, k, (((1,), (1,)), ((), ())), preferred_element_type=_F32)
    s = jnp.where(mask, s, -jnp.inf)
    m = jnp.max(s, axis=1, keepdims=True)
    p = jnp.exp(s - m)
    l = jnp.sum(p, axis=1, keepdims=True)
    o = _dot(p.astype(_BF16), v) / l
    return o, m + jnp.log(l)


def _attn_kernel(qkv_ref, o_ref, lse_ref, *, dilation, n_blocks):
    qi = lax.broadcasted_iota(jnp.int32, (ATT_BLOCK, 2 * ATT_BLOCK), 0)
    kj = lax.broadcasted_iota(jnp.int32, (ATT_BLOCK, 2 * ATT_BLOCK), 1)
    mask_two = (kj >= qi) & (kj - ATT_BLOCK <= qi)
    mask_one = (lax.broadcasted_iota(jnp.int32, (ATT_BLOCK, ATT_BLOCK), 1)
                <= lax.broadcasted_iota(jnp.int32, (ATT_BLOCK, ATT_BLOCK), 0))
    lane = lax.broadcasted_iota(jnp.int32, (ATT_BLOCK, LANES), 1)

    def block(row0, key0, n_keys, mask):
        for r in range(dilation):
            lse_tile = jnp.zeros((ATT_BLOCK, LANES), _F32)
            for hh in range(GROUP_HEADS):
                qc = r * ATT_WIDTH + hh * HEAD_DIM
                q = qkv_ref[pl.ds(row0, ATT_BLOCK), qc:qc + HEAD_DIM]
                k = qkv_ref[pl.ds(key0, n_keys), qc + GROUP_WIDTH:qc + GROUP_WIDTH + HEAD_DIM]
                v = qkv_ref[pl.ds(key0, n_keys), qc + 2 * GROUP_WIDTH:qc + 2 * GROUP_WIDTH + HEAD_DIM]
                o, lse = _attn_block(q, k, v, mask)
                oc = r * GROUP_WIDTH + hh * HEAD_DIM
                o_ref[pl.ds(row0, ATT_BLOCK), oc:oc + HEAD_DIM] = o.astype(_BF16)
                lse_tile = jnp.where(lane == hh, lse, lse_tile)
            lse_ref[pl.ds(row0, ATT_BLOCK), r * LANES:(r + 1) * LANES] = lse_tile

    block(0, 0, ATT_BLOCK, mask_one)

    def body(i, carry):
        row0 = pl.multiple_of(i * ATT_BLOCK, ATT_BLOCK)
        key0 = pl.multiple_of((i - 1) * ATT_BLOCK, ATT_BLOCK)
        block(row0, key0, 2 * ATT_BLOCK, mask_two)
        return carry

    if n_blocks > 1:
        lax.fori_loop(1, n_blocks, body, 0)


def _attention_group(qkv, batch, seq, dilation):
    length = seq // dilation
    view = qkv.reshape(batch, length, dilation * ATT_WIDTH)
    spec = lambda width: pl.BlockSpec((None, length, dilation * width), lambda b: (b, 0, 0))
    o, lse = pl.pallas_call(
        functools.partial(_attn_kernel, dilation=dilation, n_blocks=length // ATT_BLOCK),
        grid=(batch,),
        in_specs=[spec(ATT_WIDTH)],
        out_specs=[spec(GROUP_WIDTH), spec(LANES)],
        out_shape=[jax.ShapeDtypeStruct((batch, length, dilation * GROUP_WIDTH), _BF16),
                   jax.ShapeDtypeStruct((batch, length, dilation * LANES), _F32)],
        compiler_params=_params("parallel"),
        name=f"attn_d{dilation}",
    )(view)
    return o.reshape(batch * seq, GROUP_WIDTH), lse.reshape(batch * seq, LANES)


def _mlstm_kernel(qkm_ref, vm_ref, som_ref, gates_ref, conv_w_ref, conv_b_ref, gate_b_ref, head_g_ref,
                  o_ref, xs_ref, ct_ref, n_ref, m_ref):
    lc = M_CHUNK
    dh = M_HEAD_DIM

    @pl.when(pl.program_id(1) == 0)
    def _():
        xs_ref[0:CONV_HALO, :] = jnp.zeros((CONV_HALO, 2 * M_WIDTH), _F32)
        ct_ref[...] = jnp.zeros(ct_ref.shape, _F32)
        n_ref[...] = jnp.zeros(n_ref.shape, _F32)
        m_ref[...] = jnp.zeros(m_ref.shape, _F32)

    xs_ref[CONV_HALO:CONV_HALO + lc, :] = qkm_ref[...].astype(_F32)
    y = conv_b_ref[...] + conv_w_ref[CONV_WIDTH - 1:CONV_WIDTH, :] * xs_ref[CONV_HALO:CONV_HALO + lc, :]
    for j in range(CONV_WIDTH - 1):
        shift = CONV_WIDTH - 1 - j
        y = y + conv_w_ref[j:j + 1, :] * xs_ref[CONV_HALO - shift:CONV_HALO - shift + lc, :]
    xs_ref[0:CONV_HALO, :] = xs_ref[lc:lc + CONV_HALO, :]
    qk = y * _sigmoid(y)

    gt = gates_ref[...] + gate_b_ref[...]
    lf = jnp.minimum(gt, 0.0) - jnp.log(1.0 + jnp.exp(-jnp.abs(gt)))
    row_t = lax.broadcasted_iota(jnp.int32, (lc, lc), 0)
    col_s = lax.broadcasted_iota(jnp.int32, (lc, lc), 1)
    causal = col_s <= row_t
    b_all = jnp.dot(causal.astype(_F32), lf, preferred_element_type=_F32, precision=lax.Precision.HIGHEST)
    lane = lax.broadcasted_iota(jnp.int32, (lc, LANES), 1)
    ib = jnp.where(lane < M_HEADS, gt, b_all)
    ib_t = ib.T

    for hh in range(M_HEADS):
        q = qk[:, hh * dh:(hh + 1) * dh].astype(_BF16)
        k_f32 = qk[:, M_WIDTH + hh * dh:M_WIDTH + (hh + 1) * dh] * (dh ** -0.5)
        k = k_f32.astype(_BF16)
        v_bf = vm_ref[:, hh * dh:(hh + 1) * dh]
        v_f32 = v_bf.astype(_F32)
        i_col = ib[:, hh:hh + 1]
        b_col = ib[:, M_HEADS + hh:M_HEADS + hh + 1]
        i_minus_b_row = ib_t[hh:hh + 1, :] - ib_t[M_HEADS + hh:M_HEADS + hh + 1, :]
        m_prev = m_ref[hh][0:1, 0:1]
        ct = ct_ref[hh]
        n_row = n_ref[hh]

        d_mat = jnp.where(causal, b_col + i_minus_b_row, -jnp.inf)
        m_inter = b_col + m_prev
        m_t = jnp.maximum(m_inter, jnp.max(d_mat, axis=1, keepdims=True))
        qk_t = lax.dot_general(q, k, (((1,), (1,)), ((), ())), preferred_element_type=_F32)
        w = jnp.exp(d_mat - m_t) * qk_t
        inter = jnp.exp(m_inter - m_t)
        num = _dot(w.astype(_BF16), v_bf) + inter * _dot(q, ct.astype(_BF16))
        q_n = jnp.sum(q.astype(_F32) * n_row, axis=1, keepdims=True)
        den = jnp.sum(w, axis=1, keepdims=True) + inter * q_n
        h_out = num / jnp.maximum(jnp.abs(den), jnp.exp(-m_t))

        b_last = b_col[lc - 1:lc, :]
        g_col = b_last - b_col + i_col
        m_new = jnp.maximum(b_last + m_prev, jnp.max(g_col, axis=0, keepdims=True))
        a_col = jnp.exp(g_col - m_new)
        decay = jnp.exp(b_last + m_prev - m_new)
        av = (a_col * v_f32).astype(_BF16)
        ct_ref[hh] = decay * ct + lax.dot_general(k, av, (((0,), (0,)), ((), ())), preferred_element_type=_F32)
        n_ref[hh] = decay * n_row + jnp.sum(a_col * k_f32, axis=0, keepdims=True)
        m_ref[hh] = jnp.broadcast_to(m_new, m_ref.shape[1:])

        hn = _rms(h_out) * head_g_ref[:, hh * dh:(hh + 1) * dh]
        gated = som_ref[:, hh * dh:(hh + 1) * dh].astype(_F32) * hn
        o_ref[:, hh * dh:(hh + 1) * dh] = gated.astype(_BF16)


def _mlstm(qkm, vm, som, gates, batch, seq, conv_w, conv_b, i_bias, f_bias, head_g):
    n_chunks = seq // M_CHUNK
    gate_b = jnp.pad(jnp.concatenate([i_bias, f_bias]), (0, LANES - 2 * M_HEADS)).reshape(1, LANES)
    chunk = lambda width: pl.BlockSpec((None, M_CHUNK, width), lambda b, c: (b, c, 0))
    out = pl.pallas_call(
        _mlstm_kernel,
        grid=(batch, n_chunks),
        in_specs=[chunk(2 * M_WIDTH), chunk(M_WIDTH), chunk(M_WIDTH), chunk(LANES),
                  _resident((CONV_WIDTH, 2 * M_WIDTH)), _resident((1, 2 * M_WIDTH)), _resident((1, LANES)),
                  _resident((1, M_WIDTH))],
        out_specs=chunk(M_WIDTH),
        out_shape=jax.ShapeDtypeStruct((batch, seq, M_WIDTH), _BF16),
        scratch_shapes=[pltpu.VMEM((M_CHUNK + CONV_HALO, 2 * M_WIDTH), _F32),
                        pltpu.VMEM((M_HEADS, M_HEAD_DIM, M_HEAD_DIM), _F32),
                        pltpu.VMEM((M_HEADS, 1, M_HEAD_DIM), _F32),
                        pltpu.VMEM((M_HEADS, 8, LANES), _F32)],
        compiler_params=_params("parallel", "arbitrary"),
        name="mlstm",
    )(qkm.reshape(batch, seq, 2 * M_WIDTH), vm.reshape(batch, seq, M_WIDTH), som.reshape(batch, seq, M_WIDTH),
      gates.reshape(batch, seq, LANES), conv_w, conv_b.reshape(1, -1), gate_b, head_g.reshape(1, -1))
    return out.reshape(batch * seq, M_WIDTH)


def _merge_kernel(x_ref, o0_ref, o1_ref, o2_ref, l0_ref, l1_ref, l2_ref, ml_ref, sga_ref, sgm_ref,
                  wa_ref, wm_ref, wo_ref, post_g_ref, out_ref):
    lses = [r[...] for r in (l0_ref, l1_ref, l2_ref)]
    top = jnp.maximum(jnp.maximum(lses[0], lses[1]), lses[2])
    es = [jnp.exp(l - top) for l in lses]
    inv = 1.0 / (es[0] + es[1] + es[2])
    heads = []
    for hh in range(GROUP_HEADS):
        acc = None
        for e, o_ref in zip(es, (o0_ref, o1_ref, o2_ref)):
            alpha = (e * inv)[:, hh:hh + 1]
            term = alpha * o_ref[:, hh * HEAD_DIM:(hh + 1) * HEAD_DIM].astype(_F32)
            acc = term if acc is None else acc + term
        heads.append(acc.astype(_BF16))
    att = jnp.concatenate(heads, axis=1)
    a = _dot(att, wa_ref[...])
    m = _dot(ml_ref[...], wm_ref[...])
    merged = sga_ref[...].astype(_F32) * a + sgm_ref[...].astype(_F32) * m
    y = _dot(merged.astype(_BF16), wo_ref[...])
    out_ref[...] = x_ref[...] + _rms(y) * post_g_ref[...]


def _merge(x2d, outs, lses, ml, sga, sgm, w_att, w_ml, w_out, post_g):
    t, d = x2d.shape
    row = lambda width: pl.BlockSpec((TOKEN_TILE, width), lambda i: (i, 0))
    return pl.pallas_call(
        _merge_kernel,
        grid=(t // TOKEN_TILE,),
        in_specs=[row(d)] + [row(GROUP_WIDTH)] * 3 + [row(LANES)] * 3 + [row(M_WIDTH), row(d), row(d),
                  _resident(w_att.shape), _resident(w_ml.shape), _resident(w_out.shape), _resident((1, d))],
        out_specs=row(d),
        out_shape=jax.ShapeDtypeStruct((t, d), _F32),
        compiler_params=_params("parallel"),
        name="merge",
    )(x2d, *outs, *lses, ml, sga, sgm, w_att.astype(_BF16), w_ml.astype(_BF16), w_out.astype(_BF16),
      post_g.reshape(1, d))


def kernel(x, ffn1_pre_g, ffn1_w_gate, ffn1_w_up, ffn1_w_down, ffn1_post_g, mix_pre_g, w_in, conv_w, conv_b, mlstm_i_bias, mlstm_f_bias, mlstm_head_g, w_att_branch, w_mlstm_branch, w_out, mix_post_g, ffn2_pre_g, ffn2_w_gate, ffn2_w_up, ffn2_w_down, ffn2_post_g):
    batch, seq, d = x.shape
    xt = x.reshape(batch * seq, d)
    for l in range(ffn1_pre_g.shape[0]):
        xt = _ffn(xt, ffn1_pre_g[l], ffn1_w_gate[l], ffn1_w_up[l], ffn1_w_down[l], ffn1_post_g[l])
        qkv0, qkv1, qkv2, qkm, vm, som, sga, sgm, gates = _inproj(xt, seq, mix_pre_g[l], w_in[l])
        outs, lses = zip(*(_attention_group(qkv, batch, seq, dil)
                           for qkv, dil in zip((qkv0, qkv1, qkv2), GROUP_DILATIONS)))
        ml = _mlstm(qkm, vm, som, gates, batch, seq, conv_w[l], conv_b[l], mlstm_i_bias[l], mlstm_f_bias[l],
                    mlstm_head_g[l])
        xt = _merge(xt, outs, lses, ml, sga, sgm, w_att_branch[l], w_mlstm_branch[l], w_out[l], mix_post_g[l])
        xt = _ffn(xt, ffn2_pre_g[l], ffn2_w_gate[l], ffn2_w_up[l], ffn2_w_down[l], ffn2_post_g[l])
    return xt.reshape(batch, seq, d)
```

```python
import functools
import math

import numpy as np
import jax
import jax.numpy as jnp
from jax import lax
from jax.experimental import pallas as pl
from jax.experimental.pallas import tpu as pltpu

D_MODEL = 1024
N_GROUPS = 3
GROUP_DILATIONS = (1, 4, 16)
ATT_SPAN = 128
GROUP_HEADS = 4
HEAD_DIM = 128
GROUP_WIDTH = GROUP_HEADS * HEAD_DIM
ATT_WIDTH = N_GROUPS * GROUP_WIDTH
ATT_BLOCK = 128
ROPE_THETA = 500000.0
ROPE_DIMS = HEAD_DIM // 4
ROPE_HALF = ROPE_DIMS // 2
M_HEADS = 4
M_WIDTH = D_MODEL
M_HEAD_DIM = M_WIDTH // M_HEADS
M_CHUNK = 256
CONV_WIDTH = 4
CONV_HALO = 8
CONV_COLS = 512
RMS_EPS = 1e-6
LANES = 128
MASK_BIAS = -1e30
VMEM_LIMIT_BYTES = 58 * 1024 * 1024
TOKEN_TILE = 512

_BF16 = jnp.bfloat16
_F32 = jnp.float32
_NT = (((1,), (1,)), ((), ()))
_TN = (((0,), (0,)), ((), ()))


def _rms(x):
    return x * lax.rsqrt(jnp.mean(x * x, axis=-1, keepdims=True) + RMS_EPS)


def _sigmoid(x):
    return 1.0 / (1.0 + jnp.exp(-x))


def _dot(a, b):
    return jnp.dot(a, b, preferred_element_type=_F32)


def _resident(shape):
    return pl.BlockSpec(shape, lambda *_: (0,) * len(shape), pipeline_mode=pl.Buffered(1))


def _params(*semantics):
    return pltpu.CompilerParams(dimension_semantics=semantics, vmem_limit_bytes=VMEM_LIMIT_BYTES)


def _ffn_kernel(x_ref, pre_g_ref, wg_ref, wu_ref, wd_ref, post_g_ref, o_ref):
    x = x_ref[...]
    h = (_rms(x) * pre_g_ref[...]).astype(_BF16)
    g = _dot(h, wg_ref[...])
    u = _dot(h, wu_ref[...])
    a = (g * _sigmoid(g) * u).astype(_BF16)
    f = _dot(a, wd_ref[...])
    o_ref[...] = x + 0.5 * (_rms(f) * post_g_ref[...])


def _ffn(x2d, pre_g, w_gate, w_up, w_down, post_g):
    t, d = x2d.shape
    f = w_gate.shape[1]
    row = pl.BlockSpec((TOKEN_TILE, d), lambda i: (i, 0))
    return pl.pallas_call(
        _ffn_kernel,
        grid=(t // TOKEN_TILE,),
        in_specs=[row, _resident((1, d)), _resident((d, f)), _resident((d, f)), _resident((f, d)),
                  _resident((1, d))],
        out_specs=row,
        out_shape=jax.ShapeDtypeStruct((t, d), _F32),
        compiler_params=_params("parallel"),
        name="ffn",
    )(x2d, pre_g.reshape(1, d), w_gate.astype(_BF16), w_up.astype(_BF16), w_down.astype(_BF16),
      post_g.reshape(1, d))


def _inproj_kernel(x_ref, g_ref, w_ref, tab0_ref, tab1_ref, tab2_ref, conv_w_ref, conv_b_ref,
                   qkv0_ref, qkv1_ref, qkv2_ref, qm_ref, km_ref, vm_ref, gates_ref,
                   hs_ref, hp_ref, xs_ref, halo_ref, *, tiles_per_seq):
    tm = TOKEN_TILE
    hf = _rms(x_ref[...]) * g_ref[...]
    h_nat = hf.astype(_BF16)
    for c in range(D_MODEL // LANES):
        hs_ref[c] = hf[:, c * LANES:(c + 1) * LANES]

    lane = lax.broadcasted_iota(jnp.int32, (tm, HEAD_DIM), 1)
    low_half = lane < ROPE_HALF
    q_scale = HEAD_DIM ** -0.5 * math.log2(math.e)

    col = 0
    for qkv_ref, tab_ref, d in zip((qkv0_ref, qkv1_ref, qkv2_ref), (tab0_ref, tab1_ref, tab2_ref), GROUP_DILATIONS):
        rows = tm // d
        if d == 1:
            h_g = h_nat
        else:
            for r in range(d):
                piece = jnp.concatenate([hs_ref[c, pl.ds(r, rows, stride=d), :] for c in range(D_MODEL // LANES)],
                                        axis=1)
                hp_ref[r * rows:(r + 1) * rows, :] = piece.astype(_BF16)
            h_g = hp_ref[...]
        cos = tab_ref[0]
        sin = tab_ref[1]
        for part in range(3):
            p = _dot(h_g, w_ref[:, col:col + GROUP_WIDTH])
            col += GROUP_WIDTH
            for hh in range(GROUP_HEADS):
                x = p[:, hh * HEAD_DIM:(hh + 1) * HEAD_DIM]
                if part < 2:
                    partner = jnp.where(low_half, pltpu.roll(x, HEAD_DIM - ROPE_HALF, 1),
                                        pltpu.roll(x, ROPE_HALF, 1))
                    x = x * cos + partner * sin
                if part == 0:
                    x = x * q_scale
                x = x.astype(_BF16)
                for r in range(d):
                    lo = r * ATT_WIDTH + part * GROUP_WIDTH + hh * HEAD_DIM
                    qkv_ref[:, lo:lo + HEAD_DIM] = x[r * rows:(r + 1) * rows, :]

    @pl.when(pl.program_id(0) % tiles_per_seq == 0)
    def _():
        halo_ref[...] = jnp.zeros(halo_ref.shape, _F32)

    for cc in range(2 * M_WIDTH // CONV_COLS):
        cs = slice(cc * CONV_COLS, (cc + 1) * CONV_COLS)
        p = _dot(h_nat, w_ref[:, col:col + CONV_COLS])
        col += CONV_COLS
        xs_ref[0:CONV_HALO, :] = halo_ref[:, cs]
        xs_ref[CONV_HALO:CONV_HALO + tm, :] = p
        halo_ref[:, cs] = p[tm - CONV_HALO:tm, :]
        y = conv_b_ref[:, cs] + conv_w_ref[CONV_WIDTH - 1:CONV_WIDTH, cs] * p
        for j in range(CONV_WIDTH - 1):
            start = CONV_HALO - (CONV_WIDTH - 1 - j)
            y = y + conv_w_ref[j:j + 1, cs] * xs_ref[start:start + tm, :]
        qk = y * _sigmoid(y)
        if cc < M_WIDTH // CONV_COLS:
            qm_ref[:, cs] = qk.astype(_BF16)
        else:
            km_ref[:, cc * CONV_COLS - M_WIDTH:(cc + 1) * CONV_COLS - M_WIDTH] = (
                qk * M_HEAD_DIM ** -0.5).astype(_BF16)

    vm_ref[...] = _dot(h_nat, w_ref[:, col:col + M_WIDTH]).astype(_BF16)
    col += M_WIDTH
    gates_ref[...] = _dot(h_nat, w_ref[:, col:col + LANES])


def _rope_tables(seq, dilation):
    inv_freq = jnp.power(ROPE_THETA, -(jnp.arange(ROPE_HALF, dtype=_F32) * 2.0 / ROPE_DIMS))
    ang = jnp.arange(seq, dtype=jnp.int32).astype(_F32)[:, None] * inv_freq[None, :]
    cos, sin = jnp.cos(ang), jnp.sin(ang)
    rest = HEAD_DIM - ROPE_DIMS
    tabs = jnp.stack([jnp.concatenate([cos, cos, jnp.ones((seq, rest), _F32)], axis=-1),
                      jnp.concatenate([-sin, sin, jnp.zeros((seq, rest), _F32)], axis=-1)])
    tabs = tabs.reshape(2, seq // TOKEN_TILE, TOKEN_TILE // dilation, dilation, HEAD_DIM)
    return tabs.transpose(0, 1, 3, 2, 4).reshape(2, seq, HEAD_DIM)


def _split_w_in(w_in):
    a, m = ATT_WIDTH, M_WIDTH
    cols = []
    for g in range(N_GROUPS):
        for part in range(3):
            lo = part * a + g * GROUP_WIDTH
            cols.append(w_in[:, lo:lo + GROUP_WIDTH])
    base = 3 * a
    cols.append(w_in[:, base:base + 3 * m])
    gate_lo = base + 4 * m
    branch_lo = gate_lo + 2 * M_HEADS
    cols.append(jnp.pad(w_in[:, gate_lo:branch_lo], ((0, 0), (0, LANES - 2 * M_HEADS))))
    w_proj = jnp.concatenate(cols, axis=1).astype(_BF16)
    w_gates = jnp.concatenate([w_in[:, base + 3 * m:gate_lo], w_in[:, branch_lo:branch_lo + 2 * D_MODEL]],
                              axis=1).astype(_BF16)
    return w_proj, w_gates


def _inproj(x2d, seq, pre_g, w_proj, conv_w, conv_b):
    t, d = x2d.shape
    tiles_per_seq = seq // TOKEN_TILE
    tables = [_rope_tables(seq, dil) for dil in GROUP_DILATIONS]
    row = lambda width: pl.BlockSpec((TOKEN_TILE, width), lambda i: (i, 0))
    cls = lambda dil: pl.BlockSpec((TOKEN_TILE // dil, dil * ATT_WIDTH), lambda i: (i, 0))
    tab = pl.BlockSpec((2, TOKEN_TILE, HEAD_DIM), lambda i: (0, i % tiles_per_seq, 0))
    out_shape = [jax.ShapeDtypeStruct((t // dil, dil * ATT_WIDTH), _BF16) for dil in GROUP_DILATIONS]
    out_shape += [jax.ShapeDtypeStruct((t, M_WIDTH), _BF16)] * 3
    out_shape.append(jax.ShapeDtypeStruct((t, LANES), _F32))
    return pl.pallas_call(
        functools.partial(_inproj_kernel, tiles_per_seq=tiles_per_seq),
        grid=(t // TOKEN_TILE,),
        in_specs=[row(d), _resident((1, d)), _resident(w_proj.shape), tab, tab, tab,
                  _resident((CONV_WIDTH, 2 * M_WIDTH)), _resident((1, 2 * M_WIDTH))],
        out_specs=[cls(dil) for dil in GROUP_DILATIONS] + [row(M_WIDTH)] * 3 + [row(LANES)],
        out_shape=out_shape,
        scratch_shapes=[pltpu.VMEM((d // LANES, TOKEN_TILE, LANES), _F32),
                        pltpu.VMEM((TOKEN_TILE, d), _BF16),
                        pltpu.VMEM((CONV_HALO + TOKEN_TILE, CONV_COLS), _F32),
                        pltpu.VMEM((CONV_HALO, 2 * M_WIDTH), _F32)],
        compiler_params=_params("arbitrary"),
        name="inproj",
    )(x2d, pre_g.reshape(1, d), w_proj, *tables, conv_w, conv_b.reshape(1, -1))


def _attn_kernel(qkv_ref, bias_two_ref, bias_one_ref, o_ref, lse_ref, *scratch, dilation, n_blocks):
    eye = (lax.broadcasted_iota(jnp.int32, (ATT_BLOCK, ATT_BLOCK), 0)
           == lax.broadcasted_iota(jnp.int32, (ATT_BLOCK, ATT_BLOCK), 1)).astype(_BF16)
    lane = lax.broadcasted_iota(jnp.int32, (ATT_BLOCK, LANES), 1)

    def block(row0, key0, n_keys, bias_ref):
        ones = jnp.ones((n_keys, HEAD_DIM), _BF16)
        bias = bias_ref[...]
        for r in range(dilation):
            scores = []
            for hh in range(GROUP_HEADS):
                qc = r * ATT_WIDTH + hh * HEAD_DIM
                q = qkv_ref[pl.ds(row0, ATT_BLOCK), qc:qc + HEAD_DIM]
                k = qkv_ref[pl.ds(key0, n_keys), qc + GROUP_WIDTH:qc + GROUP_WIDTH + HEAD_DIM]
                scores.append(lax.dot_general(jnp.concatenate([q, eye], axis=1), jnp.concatenate([k, bias], axis=1),
                                              _NT, preferred_element_type=_F32))
            tops = [jnp.max(s, axis=1, keepdims=True) for s in scores]
            probs = [jnp.exp2(s - m).astype(_BF16) for s, m in zip(scores, tops)]
            lse_tile = jnp.zeros((ATT_BLOCK, LANES), _F32)
            out_row = row0 * dilation + r
            for hh in range(GROUP_HEADS):
                vc = r * ATT_WIDTH + 2 * GROUP_WIDTH + hh * HEAD_DIM
                v = qkv_ref[pl.ds(key0, n_keys), vc:vc + HEAD_DIM]
                acc = _dot(probs[hh], jnp.concatenate([v, ones], axis=1))
                denom = acc[:, HEAD_DIM:]
                o = acc[:, :HEAD_DIM] * (1.0 / denom)
                lse_tile = jnp.where(lane == hh, tops[hh] + jnp.log2(denom), lse_tile)
                if dilation == 1:
                    o_ref[pl.ds(row0, ATT_BLOCK), hh * HEAD_DIM:(hh + 1) * HEAD_DIM] = o.astype(_BF16)
                else:
                    scratch[0][hh, pl.ds(out_row, ATT_BLOCK, stride=dilation), :] = o
            lse_tile = lse_tile * math.log(2.0)
            if dilation == 1:
                lse_ref[pl.ds(row0, ATT_BLOCK), :] = lse_tile
            else:
                lse_ref[pl.ds(out_row, ATT_BLOCK, stride=dilation), :] = lse_tile

    block(0, 0, ATT_BLOCK, bias_one_ref)

    def body(i, carry):
        row0 = pl.multiple_of(i * ATT_BLOCK, ATT_BLOCK)
        key0 = pl.multiple_of((i - 1) * ATT_BLOCK, ATT_BLOCK)
        block(row0, key0, 2 * ATT_BLOCK, bias_two_ref)
        return carry

    if n_blocks > 1:
        lax.fori_loop(1, n_blocks, body, 0)
    if dilation > 1:
        for hh in range(GROUP_HEADS):
            o_ref[:, hh * HEAD_DIM:(hh + 1) * HEAD_DIM] = scratch[0][hh].astype(_BF16)


def _band_bias():
    qi = np.arange(ATT_BLOCK)[None, :]
    kj = np.arange(2 * ATT_BLOCK)[:, None]
    valid_two = np.where(kj < ATT_BLOCK, kj >= qi, kj - ATT_BLOCK <= qi)
    valid_one = np.arange(ATT_BLOCK)[:, None] <= qi
    to_bias = lambda valid: jnp.asarray(np.where(valid, 0.0, MASK_BIAS), _BF16)
    return to_bias(valid_two), to_bias(valid_one)


def _attention_group(qkv, batch, seq, dilation):
    length = seq // dilation
    view = qkv.reshape(batch, length, dilation * ATT_WIDTH)
    bias_two, bias_one = _band_bias()
    scratch = [] if dilation == 1 else [pltpu.VMEM((GROUP_HEADS, seq, HEAD_DIM), _F32)]
    return pl.pallas_call(
        functools.partial(_attn_kernel, dilation=dilation, n_blocks=length // ATT_BLOCK),
        grid=(batch,),
        in_specs=[pl.BlockSpec((None, length, dilation * ATT_WIDTH), lambda b: (b, 0, 0)),
                  _resident(bias_two.shape), _resident(bias_one.shape)],
        out_specs=[pl.BlockSpec((None, seq, GROUP_WIDTH), lambda b: (b, 0, 0)),
                   pl.BlockSpec((None, seq, LANES), lambda b: (b, 0, 0))],
        out_shape=[jax.ShapeDtypeStruct((batch, seq, GROUP_WIDTH), _BF16),
                   jax.ShapeDtypeStruct((batch, seq, LANES), _F32)],
        scratch_shapes=scratch,
        compiler_params=_params("parallel"),
        name=f"attn_d{dilation}",
    )(view, bias_two, bias_one)


def _mlstm_kernel(qm_ref, km_ref, vm_ref, gates_ref, gate_b_ref, head_g_ref, o_ref, state_ref, m_ref):
    lc = M_CHUNK
    dh = M_HEAD_DIM

    @pl.when(pl.program_id(1) == 0)
    def _():
        state_ref[...] = jnp.zeros(state_ref.shape, _F32)
        m_ref[...] = jnp.zeros(m_ref.shape, _F32)

    gt = gates_ref[...] + gate_b_ref[...]
    lf = jnp.minimum(gt, 0.0) - jnp.log(1.0 + jnp.exp(-jnp.abs(gt)))
    row_t = lax.broadcasted_iota(jnp.int32, (lc, lc), 0)
    col_s = lax.broadcasted_iota(jnp.int32, (lc, lc), 1)
    causal = col_s <= row_t
    b_all = jnp.dot(causal.astype(_F32), lf, preferred_element_type=_F32, precision=lax.Precision.HIGHEST)
    lane = lax.broadcasted_iota(jnp.int32, (lc, LANES), 1)
    ib = jnp.where(lane < M_HEADS, gt, b_all)
    ib_t = ib.T
    ones = jnp.ones((lc, LANES), _BF16)

    for hh in range(M_HEADS):
        hs = slice(hh * dh, (hh + 1) * dh)
        q = qm_ref[:, hs]
        k = km_ref[:, hs]
        v_aug = jnp.concatenate([vm_ref[:, hs], ones], axis=1)
        i_col = ib[:, hh:hh + 1]
        b_col = ib[:, M_HEADS + hh:M_HEADS + hh + 1]
        i_minus_b_row = ib_t[hh:hh + 1, :] - ib_t[M_HEADS + hh:M_HEADS + hh + 1, :]
        m_prev = m_ref[hh][0:1, 0:1]
        state = state_ref[hh]

        d_mat = jnp.where(causal, b_col + i_minus_b_row, -jnp.inf)
        m_inter = b_col + m_prev
        m_t = jnp.maximum(m_inter, jnp.max(d_mat, axis=1, keepdims=True))
        w = jnp.exp(d_mat - m_t) * lax.dot_general(q, k, _NT, preferred_element_type=_F32)
        inter = jnp.exp(m_inter - m_t)
        acc = _dot(w.astype(_BF16), v_aug) + inter * _dot(q, state.astype(_BF16))
        den = acc[:, dh:]
        inv = 1.0 / jnp.maximum(jnp.abs(den), jnp.exp(-m_t))
        h_out = acc[:, :dh] * jnp.concatenate([inv] * (dh // LANES), axis=1)

        b_last = b_col[lc - 1:lc, :]
        g_col = b_last - b_col + i_col
        m_new = jnp.maximum(b_last + m_prev, jnp.max(g_col, axis=0, keepdims=True))
        a_col = jnp.exp(g_col - m_new)
        decay = jnp.exp(b_last + m_prev - m_new)
        av = (a_col * v_aug.astype(_F32)).astype(_BF16)
        state_ref[hh] = decay * state + lax.dot_general(k, av, _TN, preferred_element_type=_F32)
        m_ref[hh] = jnp.broadcast_to(m_new, m_ref.shape[1:])

        o_ref[:, hs] = (_rms(h_out) * head_g_ref[:, hs]).astype(_BF16)


def _mlstm(qm, km, vm, gates, batch, seq, i_bias, f_bias, head_g):
    n_chunks = seq // M_CHUNK
    gate_b = jnp.pad(jnp.concatenate([i_bias, f_bias]), (0, LANES - 2 * M_HEADS)).reshape(1, LANES)
    chunk = lambda width: pl.BlockSpec((None, M_CHUNK, width), lambda b, c: (b, c, 0))
    out = pl.pallas_call(
        _mlstm_kernel,
        grid=(batch, n_chunks),
        in_specs=[chunk(M_WIDTH), chunk(M_WIDTH), chunk(M_WIDTH), chunk(LANES),
                  _resident((1, LANES)), _resident((1, M_WIDTH))],
        out_specs=chunk(M_WIDTH),
        out_shape=jax.ShapeDtypeStruct((batch, seq, M_WIDTH), _BF16),
        scratch_shapes=[pltpu.VMEM((M_HEADS, M_HEAD_DIM, M_HEAD_DIM + LANES), _F32),
                        pltpu.VMEM((M_HEADS, 8, LANES), _F32)],
        compiler_params=_params("parallel", "arbitrary"),
        name="mlstm",
    )(qm.reshape(batch, seq, M_WIDTH), km.reshape(batch, seq, M_WIDTH), vm.reshape(batch, seq, M_WIDTH),
      gates.reshape(batch, seq, LANES), gate_b, head_g.reshape(1, -1))
    return out.reshape(batch * seq, M_WIDTH)


def _merge_kernel(x_ref, o0_ref, o1_ref, o2_ref, l0_ref, l1_ref, l2_ref, hm_ref, pre_g_ref, wg_ref,
                  wa_ref, wm_ref, wo_ref, post_g_ref, out_ref):
    x = x_ref[...]
    h = (_rms(x) * pre_g_ref[...]).astype(_BF16)
    lses = [r[...] for r in (l0_ref, l1_ref, l2_ref)]
    top = jnp.maximum(jnp.maximum(lses[0], lses[1]), lses[2])
    es = [jnp.exp(l - top) for l in lses]
    inv = 1.0 / (es[0] + es[1] + es[2])
    alphas = [e * inv for e in es]
    heads = []
    for hh in range(GROUP_HEADS):
        acc = None
        for alpha, o_ref in zip(alphas, (o0_ref, o1_ref, o2_ref)):
            term = alpha[:, hh:hh + 1] * o_ref[:, hh * HEAD_DIM:(hh + 1) * HEAD_DIM].astype(_F32)
            acc = term if acc is None else acc + term
        heads.append(acc.astype(_BF16))
    att = jnp.concatenate(heads, axis=1)
    a = _sigmoid(_dot(h, wg_ref[:, M_WIDTH:M_WIDTH + D_MODEL])) * _dot(att, wa_ref[...])
    ml = (_sigmoid(_dot(h, wg_ref[:, :M_WIDTH])) * hm_ref[...].astype(_F32)).astype(_BF16)
    m = _sigmoid(_dot(h, wg_ref[:, M_WIDTH + D_MODEL:])) * _dot(ml, wm_ref[...])
    y = _dot((a + m).astype(_BF16), wo_ref[...])
    out_ref[...] = x + _rms(y) * post_g_ref[...]


def _merge(x2d, outs, lses, hm, pre_g, w_gates, w_att, w_ml, w_out, post_g):
    t, d = x2d.shape
    row = lambda width: pl.BlockSpec((TOKEN_TILE, width), lambda i: (i, 0))
    return pl.pallas_call(
        _merge_kernel,
        grid=(t // TOKEN_TILE,),
        in_specs=[row(d)] + [row(GROUP_WIDTH)] * 3 + [row(LANES)] * 3 + [row(M_WIDTH), _resident((1, d)),
                  _resident(w_gates.shape), _resident(w_att.shape), _resident(w_ml.shape),
                  _resident(w_out.shape), _resident((1, d))],
        out_specs=row(d),
        out_shape=jax.ShapeDtypeStruct((t, d), _F32),
        compiler_params=_params("parallel"),
        name="merge",
    )(x2d, *outs, *lses, hm, pre_g.reshape(1, d), w_gates, w_att.astype(_BF16), w_ml.astype(_BF16),
      w_out.astype(_BF16), post_g.reshape(1, d))


def kernel(x, ffn1_pre_g, ffn1_w_gate, ffn1_w_up, ffn1_w_down, ffn1_post_g, mix_pre_g, w_in, conv_w, conv_b, mlstm_i_bias, mlstm_f_bias, mlstm_head_g, w_att_branch, w_mlstm_branch, w_out, mix_post_g, ffn2_pre_g, ffn2_w_gate, ffn2_w_up, ffn2_w_down, ffn2_post_g):
    batch, seq, d = x.shape
    t = batch * seq
    xt = x.reshape(t, d)
    for l in range(ffn1_pre_g.shape[0]):
        xt = _ffn(xt, ffn1_pre_g[l], ffn1_w_gate[l], ffn1_w_up[l], ffn1_w_down[l], ffn1_post_g[l])
        w_proj, w_gates = _split_w_in(w_in[l])
        qkv0, qkv1, qkv2, qm, km, vm, gates = _inproj(xt, seq, mix_pre_g[l], w_proj, conv_w[l], conv_b[l])
        outs, lses = zip(*(_attention_group(qkv, batch, seq, dil)
                           for qkv, dil in zip((qkv0, qkv1, qkv2), GROUP_DILATIONS)))
        outs = [o.reshape(t, GROUP_WIDTH) for o in outs]
        lses = [s.reshape(t, LANES) for s in lses]
        hm = _mlstm(qm, km, vm, gates, batch, seq, mlstm_i_bias[l], mlstm_f_bias[l], mlstm_head_g[l])
        xt = _merge(xt, outs, lses, hm, mix_pre_g[l], w_gates, w_att_branch[l], w_mlstm_branch[l], w_out[l],
                    mix_post_g[l])
        xt = _ffn(xt, ffn2_pre_g[l], ffn2_w_gate[l], ffn2_w_up[l], ffn2_w_down[l], ffn2_post_g[l])
    return xt.reshape(batch, seq, d)
```

```python
import functools
import math

import numpy as np
import jax
import jax.numpy as jnp
from jax import lax
from jax.experimental import pallas as pl
from jax.experimental.pallas import tpu as pltpu

D_MODEL = 1024
N_GROUPS = 3
GROUP_DILATIONS = (1, 4, 16)
ATT_SPAN = 128
GROUP_HEADS = 4
HEAD_DIM = 128
GROUP_WIDTH = GROUP_HEADS * HEAD_DIM
ATT_WIDTH = N_GROUPS * GROUP_WIDTH
ATT_BLOCK = 128
ROPE_THETA = 500000.0
ROPE_DIMS = HEAD_DIM // 4
ROPE_HALF = ROPE_DIMS // 2
M_HEADS = 4
M_WIDTH = D_MODEL
M_HEAD_DIM = M_WIDTH // M_HEADS
M_CHUNK = 256
M_BATCH = 2
CONV_WIDTH = 4
CONV_HALO = 8
CONV_COLS = 512
RMS_EPS = 1e-6
LANES = 128
MASK_BIAS = -1e30
VMEM_LIMIT_BYTES = 58 * 1024 * 1024
TOKEN_TILE = 512

_BF16 = jnp.bfloat16
_F32 = jnp.float32
_NT = (((1,), (1,)), ((), ()))
_TN = (((0,), (0,)), ((), ()))


def _rms(x):
    return x * lax.rsqrt(jnp.mean(x * x, axis=-1, keepdims=True) + RMS_EPS)


def _sigmoid(x):
    return 1.0 / (1.0 + jnp.exp(-x))


def _dot(a, b):
    return jnp.dot(a, b, preferred_element_type=_F32)


def _resident(shape):
    return pl.BlockSpec(shape, lambda *_: (0,) * len(shape), pipeline_mode=pl.Buffered(1))


def _params(*semantics):
    return pltpu.CompilerParams(dimension_semantics=semantics, vmem_limit_bytes=VMEM_LIMIT_BYTES)


def _ffn_kernel(x_ref, pre_g_ref, wg_ref, wu_ref, wd_ref, post_g_ref, o_ref):
    x = x_ref[...]
    h = (_rms(x) * pre_g_ref[...]).astype(_BF16)
    g = _dot(h, wg_ref[...])
    u = _dot(h, wu_ref[...])
    a = (g * _sigmoid(g) * u).astype(_BF16)
    f = _dot(a, wd_ref[...])
    o_ref[...] = x + 0.5 * (_rms(f) * post_g_ref[...])


def _ffn(x2d, pre_g, w_gate, w_up, w_down, post_g):
    t, d = x2d.shape
    f = w_gate.shape[1]
    row = pl.BlockSpec((TOKEN_TILE, d), lambda i: (i, 0))
    return pl.pallas_call(
        _ffn_kernel,
        grid=(t // TOKEN_TILE,),
        in_specs=[row, _resident((1, d)), _resident((d, f)), _resident((d, f)), _resident((f, d)),
                  _resident((1, d))],
        out_specs=row,
        out_shape=jax.ShapeDtypeStruct((t, d), _F32),
        compiler_params=_params("parallel"),
        name="ffn",
    )(x2d, pre_g.reshape(1, d), w_gate.astype(_BF16), w_up.astype(_BF16), w_down.astype(_BF16),
      post_g.reshape(1, d))


def _inproj_kernel(x_ref, g_ref, w_ref, tab0_ref, tab1_ref, tab2_ref, conv_w_ref, conv_b_ref,
                   qkv0_ref, qkv1_ref, qkv2_ref, qm_ref, km_ref, vm_ref, gates_ref,
                   hn_ref, hs_ref, hp_ref, xs_ref, halo_ref, *, tiles_per_seq):
    tm = TOKEN_TILE
    n_slabs = D_MODEL // LANES
    hf = _rms(x_ref[...]) * g_ref[...]
    hn_ref[...] = hf.astype(_BF16)
    for c in range(n_slabs):
        hs_ref[c] = hf[:, c * LANES:(c + 1) * LANES]

    lane = lax.broadcasted_iota(jnp.int32, (tm, HEAD_DIM), 1)
    low_half = lane < ROPE_HALF
    q_scale = HEAD_DIM ** -0.5 * math.log2(math.e)
    conv_col0 = N_GROUPS * ATT_WIDTH
    vm_col0 = conv_col0 + 2 * M_WIDTH
    gate_col0 = vm_col0 + M_WIDTH

    def attention_group(g, qkv_ref, tab_ref, d):
        rows = tm // d
        if d == 1:
            h_ref = hn_ref
        else:
            for r in range(d):
                piece = jnp.concatenate([hs_ref[c, pl.ds(r, rows, stride=d), :] for c in range(n_slabs)], axis=1)
                hp_ref[r * rows:(r + 1) * rows, :] = piece.astype(_BF16)
            h_ref = hp_ref
        cos = tab_ref[0]
        sin = tab_ref[1]
        for part in range(3):
            col = (g * 3 + part) * GROUP_WIDTH
            p = _dot(h_ref[...], w_ref[:, col:col + GROUP_WIDTH])
            for hh in range(GROUP_HEADS):
                x = p[:, hh * HEAD_DIM:(hh + 1) * HEAD_DIM]
                if part < 2:
                    partner = jnp.where(low_half, pltpu.roll(x, HEAD_DIM - ROPE_HALF, 1),
                                        pltpu.roll(x, ROPE_HALF, 1))
                    x = x * cos + partner * sin
                if part == 0:
                    x = x * q_scale
                x = x.astype(_BF16)
                for r in range(d):
                    lo = r * ATT_WIDTH + part * GROUP_WIDTH + hh * HEAD_DIM
                    qkv_ref[:, lo:lo + HEAD_DIM] = x[r * rows:(r + 1) * rows, :]

    def conv_chunk(cc):
        col = conv_col0 + cc * CONV_COLS
        p = _dot(hn_ref[...], w_ref[:, col:col + CONV_COLS])
        for c in range(CONV_COLS // LANES):
            lanes = slice(cc * CONV_COLS + c * LANES, cc * CONV_COLS + (c + 1) * LANES)
            pc = p[:, c * LANES:(c + 1) * LANES]
            xs_ref[c, pl.ds(0, CONV_HALO, stride=2), :] = halo_ref[:, lanes]
            xs_ref[c, pl.ds(2 * CONV_HALO, tm, stride=2), :] = pc
            halo_ref[:, lanes] = pc[tm - CONV_HALO:tm, :]
            y = conv_b_ref[:, lanes] + conv_w_ref[CONV_WIDTH - 1:CONV_WIDTH, lanes] * pc
            for j in range(CONV_WIDTH - 1):
                shift = CONV_WIDTH - 1 - j
                y = y + conv_w_ref[j:j + 1, lanes] * xs_ref[c, pl.ds(2 * (CONV_HALO - shift), tm, stride=2), :]
            qk = y * _sigmoid(y)
            if lanes.start < M_WIDTH:
                qm_ref[:, lanes] = qk.astype(_BF16)
            else:
                km_ref[:, lanes.start - M_WIDTH:lanes.stop - M_WIDTH] = (qk * M_HEAD_DIM ** -0.5).astype(_BF16)

    @pl.when(pl.program_id(0) % tiles_per_seq == 0)
    def _():
        halo_ref[...] = jnp.zeros(halo_ref.shape, _F32)

    vm_ref[...] = _dot(hn_ref[...], w_ref[:, vm_col0:vm_col0 + M_WIDTH]).astype(_BF16)
    gates_ref[...] = _dot(hn_ref[...], w_ref[:, gate_col0:gate_col0 + LANES])
    attention_group(0, qkv0_ref, tab0_ref, GROUP_DILATIONS[0])
    conv_chunk(0)
    conv_chunk(1)
    attention_group(1, qkv1_ref, tab1_ref, GROUP_DILATIONS[1])
    conv_chunk(2)
    conv_chunk(3)
    attention_group(2, qkv2_ref, tab2_ref, GROUP_DILATIONS[2])


def _rope_tables(seq, dilation):
    inv_freq = jnp.power(ROPE_THETA, -(jnp.arange(ROPE_HALF, dtype=_F32) * 2.0 / ROPE_DIMS))
    ang = jnp.arange(seq, dtype=jnp.int32).astype(_F32)[:, None] * inv_freq[None, :]
    cos, sin = jnp.cos(ang), jnp.sin(ang)
    rest = HEAD_DIM - ROPE_DIMS
    tabs = jnp.stack([jnp.concatenate([cos, cos, jnp.ones((seq, rest), _F32)], axis=-1),
                      jnp.concatenate([-sin, sin, jnp.zeros((seq, rest), _F32)], axis=-1)])
    tabs = tabs.reshape(2, seq // TOKEN_TILE, TOKEN_TILE // dilation, dilation, HEAD_DIM)
    return tabs.transpose(0, 1, 3, 2, 4).reshape(2, seq, HEAD_DIM)


def _split_w_in(w_in):
    a, m = ATT_WIDTH, M_WIDTH
    cols = []
    for g in range(N_GROUPS):
        for part in range(3):
            lo = part * a + g * GROUP_WIDTH
            cols.append(w_in[:, lo:lo + GROUP_WIDTH])
    base = 3 * a
    cols.append(w_in[:, base:base + 3 * m])
    gate_lo = base + 4 * m
    branch_lo = gate_lo + 2 * M_HEADS
    cols.append(jnp.pad(w_in[:, gate_lo:branch_lo], ((0, 0), (0, LANES - 2 * M_HEADS))))
    w_proj = jnp.concatenate(cols, axis=1).astype(_BF16)
    w_gates = jnp.concatenate([w_in[:, base + 3 * m:gate_lo], w_in[:, branch_lo:branch_lo + 2 * D_MODEL]],
                              axis=1).astype(_BF16)
    return w_proj, w_gates


def _inproj(x2d, seq, pre_g, w_proj, conv_w, conv_b):
    t, d = x2d.shape
    tiles_per_seq = seq // TOKEN_TILE
    tables = [_rope_tables(seq, dil) for dil in GROUP_DILATIONS]
    row = lambda width: pl.BlockSpec((TOKEN_TILE, width), lambda i: (i, 0))
    cls = lambda dil: pl.BlockSpec((TOKEN_TILE // dil, dil * ATT_WIDTH), lambda i: (i, 0))
    tab = pl.BlockSpec((2, TOKEN_TILE, HEAD_DIM), lambda i: (0, i % tiles_per_seq, 0))
    out_shape = [jax.ShapeDtypeStruct((t // dil, dil * ATT_WIDTH), _BF16) for dil in GROUP_DILATIONS]
    out_shape += [jax.ShapeDtypeStruct((t, M_WIDTH), _BF16)] * 3
    out_shape.append(jax.ShapeDtypeStruct((t, LANES), _F32))
    return pl.pallas_call(
        functools.partial(_inproj_kernel, tiles_per_seq=tiles_per_seq),
        grid=(t // TOKEN_TILE,),
        in_specs=[row(d), _resident((1, d)), _resident(w_proj.shape), tab, tab, tab,
                  _resident((CONV_WIDTH, 2 * M_WIDTH)), _resident((1, 2 * M_WIDTH))],
        out_specs=[cls(dil) for dil in GROUP_DILATIONS] + [row(M_WIDTH)] * 3 + [row(LANES)],
        out_shape=out_shape,
        scratch_shapes=[pltpu.VMEM((TOKEN_TILE, d), _BF16),
                        pltpu.VMEM((d // LANES, TOKEN_TILE, LANES), _F32),
                        pltpu.VMEM((TOKEN_TILE, d), _BF16),
                        pltpu.VMEM((CONV_COLS // LANES, 2 * (CONV_HALO + TOKEN_TILE), LANES), _F32),
                        pltpu.VMEM((CONV_HALO, 2 * M_WIDTH), _F32)],
        compiler_params=_params("arbitrary"),
        name="inproj",
    )(x2d, pre_g.reshape(1, d), w_proj, *tables, conv_w, conv_b.reshape(1, -1))


def _attn_kernel(qkv_ref, bias_two_ref, bias_one_ref, o_ref, lse_ref, *scratch, dilation, n_blocks):
    lane = lax.broadcasted_iota(jnp.int32, (ATT_BLOCK, LANES), 1)

    def block(row0, key0, n_keys, bias_ref):
        ones = jnp.ones((n_keys, HEAD_DIM), _BF16)
        for r in range(dilation):
            scores = []
            for hh in range(GROUP_HEADS):
                qc = r * ATT_WIDTH + hh * HEAD_DIM
                q = qkv_ref[pl.ds(row0, ATT_BLOCK), qc:qc + HEAD_DIM]
                k = qkv_ref[pl.ds(key0, n_keys), qc + GROUP_WIDTH:qc + GROUP_WIDTH + HEAD_DIM]
                scores.append(lax.dot_general(q, k, _NT, preferred_element_type=_F32) + bias_ref[...])
            tops = [jnp.max(s, axis=1, keepdims=True) for s in scores]
            probs = [jnp.exp2(s - m).astype(_BF16) for s, m in zip(scores, tops)]
            lse_tile = jnp.zeros((ATT_BLOCK, LANES), _F32)
            out_row = row0 * dilation + r
            for hh in range(GROUP_HEADS):
                vc = r * ATT_WIDTH + 2 * GROUP_WIDTH + hh * HEAD_DIM
                v = qkv_ref[pl.ds(key0, n_keys), vc:vc + HEAD_DIM]
                acc = _dot(probs[hh], jnp.concatenate([v, ones], axis=1))
                denom = acc[:, HEAD_DIM:]
                o = acc[:, :HEAD_DIM] * (1.0 / denom)
                lse_tile = jnp.where(lane == hh, tops[hh] + jnp.log2(denom), lse_tile)
                if dilation == 1:
                    o_ref[pl.ds(row0, ATT_BLOCK), hh * HEAD_DIM:(hh + 1) * HEAD_DIM] = o.astype(_BF16)
                else:
                    scratch[0][hh, pl.ds(out_row, ATT_BLOCK, stride=dilation), :] = o
            lse_tile = lse_tile * math.log(2.0)
            if dilation == 1:
                lse_ref[pl.ds(row0, ATT_BLOCK), :] = lse_tile
            else:
                lse_ref[pl.ds(out_row, ATT_BLOCK, stride=dilation), :] = lse_tile

    block(0, 0, ATT_BLOCK, bias_one_ref)

    def body(i, carry):
        row0 = pl.multiple_of(i * ATT_BLOCK, ATT_BLOCK)
        key0 = pl.multiple_of((i - 1) * ATT_BLOCK, ATT_BLOCK)
        block(row0, key0, 2 * ATT_BLOCK, bias_two_ref)
        return carry

    if n_blocks > 1:
        lax.fori_loop(1, n_blocks, body, 0)
    if dilation > 1:
        for hh in range(GROUP_HEADS):
            o_ref[:, hh * HEAD_DIM:(hh + 1) * HEAD_DIM] = scratch[0][hh].astype(_BF16)


def _band_bias():
    qi = np.arange(ATT_BLOCK)[:, None]
    kj = np.arange(2 * ATT_BLOCK)[None, :]
    valid_two = np.where(kj < ATT_BLOCK, kj >= qi, kj - ATT_BLOCK <= qi)
    valid_one = np.arange(ATT_BLOCK)[None, :] <= qi
    to_bias = lambda valid: jnp.asarray(np.where(valid, 0.0, MASK_BIAS), _F32)
    return to_bias(valid_two), to_bias(valid_one)


def _attention_group(qkv, batch, seq, dilation):
    length = seq // dilation
    view = qkv.reshape(batch, length, dilation * ATT_WIDTH)
    bias_two, bias_one = _band_bias()
    scratch = [] if dilation == 1 else [pltpu.VMEM((GROUP_HEADS, seq, HEAD_DIM), _F32)]
    return pl.pallas_call(
        functools.partial(_attn_kernel, dilation=dilation, n_blocks=length // ATT_BLOCK),
        grid=(batch,),
        in_specs=[pl.BlockSpec((None, length, dilation * ATT_WIDTH), lambda b: (b, 0, 0)),
                  _resident(bias_two.shape), _resident(bias_one.shape)],
        out_specs=[pl.BlockSpec((None, seq, GROUP_WIDTH), lambda b: (b, 0, 0)),
                   pl.BlockSpec((None, seq, LANES), lambda b: (b, 0, 0))],
        out_shape=[jax.ShapeDtypeStruct((batch, seq, GROUP_WIDTH), _BF16),
                   jax.ShapeDtypeStruct((batch, seq, LANES), _F32)],
        scratch_shapes=scratch,
        compiler_params=_params("parallel"),
        name=f"attn_d{dilation}",
    )(view, bias_two, bias_one)


def _mlstm_kernel(qm_ref, km_ref, vm_ref, gates_ref, gate_b_ref, head_g_ref, o_ref, state_ref, m_ref):
    lc = M_CHUNK
    dh = M_HEAD_DIM
    wide = lambda a, n: jnp.concatenate([a] * n, axis=1)

    @pl.when(pl.program_id(1) == 0)
    def _():
        state_ref[...] = jnp.zeros(state_ref.shape, _F32)
        m_ref[...] = jnp.zeros(m_ref.shape, _F32)

    causal = (lax.broadcasted_iota(jnp.int32, (lc, lc), 1) <= lax.broadcasted_iota(jnp.int32, (lc, lc), 0))
    tri = causal.astype(_BF16)
    lane = lax.broadcasted_iota(jnp.int32, (lc, LANES), 1)
    ones = jnp.ones((lc, LANES), _BF16)

    for bb in range(M_BATCH):
        gt = gates_ref[bb] + gate_b_ref[...]
        lf = jnp.minimum(gt, 0.0) - jnp.log(1.0 + jnp.exp(-jnp.abs(gt)))
        hi = lf.astype(_BF16)
        rest = lf - hi.astype(_F32)
        mid = rest.astype(_BF16)
        low = (rest - mid.astype(_F32)).astype(_BF16)
        b_all = _dot(tri, hi) + _dot(tri, mid) + _dot(tri, low)
        ib = jnp.where(lane < M_HEADS, gt, b_all)
        ib_t = ib.T

        for hh in range(M_HEADS):
            hs = slice(hh * dh, (hh + 1) * dh)
            q = qm_ref[bb, :, hs]
            k = km_ref[bb, :, hs]
            v_aug = jnp.concatenate([vm_ref[bb, :, hs], ones], axis=1)
            i_rep = jnp.broadcast_to(ib[:, hh:hh + 1], (lc, LANES))
            b_rep = jnp.broadcast_to(ib[:, M_HEADS + hh:M_HEADS + hh + 1], (lc, LANES))
            u_row = ib_t[hh:hh + 1, :] - ib_t[M_HEADS + hh:M_HEADS + hh + 1, :]
            m_prev = m_ref[bb, hh][0:1, :]
            state = state_ref[bb, hh]

            e = jnp.where(causal, u_row, -jnp.inf)
            big_m = jnp.maximum(m_prev, jnp.broadcast_to(jnp.max(e, axis=1, keepdims=True), (lc, LANES)))
            w = jnp.exp(e - wide(big_m, lc // LANES)) * lax.dot_general(q, k, _NT, preferred_element_type=_F32)
            inter = jnp.exp(m_prev - big_m)
            acc = _dot(w.astype(_BF16), v_aug) + wide(inter, dh // LANES + 1) * _dot(q, state.astype(_BF16))
            den = acc[:, dh:]
            inv = 1.0 / jnp.maximum(jnp.abs(den), jnp.exp(-(b_rep + big_m)))
            h_out = acc[:, :dh] * wide(inv, dh // LANES)

            b_last = b_rep[lc - 1:lc, :]
            g = b_last - b_rep + i_rep
            m_new = jnp.maximum(b_last + m_prev, jnp.max(g, axis=0, keepdims=True))
            a = jnp.exp(g - m_new)
            decay = jnp.exp(b_last + m_prev - m_new)
            av = (wide(a, dh // LANES + 1) * v_aug.astype(_F32)).astype(_BF16)
            state_ref[bb, hh] = (wide(decay, dh // LANES + 1) * state
                                 + lax.dot_general(k, av, _TN, preferred_element_type=_F32))
            m_ref[bb, hh] = jnp.broadcast_to(m_new, m_ref.shape[2:])

            o_ref[bb, :, hs] = (_rms(h_out) * head_g_ref[:, hs]).astype(_BF16)


def _mlstm(qm, km, vm, gates, batch, seq, i_bias, f_bias, head_g):
    n_chunks = seq // M_CHUNK
    gate_b = jnp.pad(jnp.concatenate([i_bias, f_bias]), (0, LANES - 2 * M_HEADS)).reshape(1, LANES)
    chunk = lambda width: pl.BlockSpec((M_BATCH, M_CHUNK, width), lambda b, c: (b, c, 0))
    out = pl.pallas_call(
        _mlstm_kernel,
        grid=(batch // M_BATCH, n_chunks),
        in_specs=[chunk(M_WIDTH), chunk(M_WIDTH), chunk(M_WIDTH), chunk(LANES),
                  _resident((1, LANES)), _resident((1, M_WIDTH))],
        out_specs=chunk(M_WIDTH),
        out_shape=jax.ShapeDtypeStruct((batch, seq, M_WIDTH), _BF16),
        scratch_shapes=[pltpu.VMEM((M_BATCH, M_HEADS, M_HEAD_DIM, M_HEAD_DIM + LANES), _F32),
                        pltpu.VMEM((M_BATCH, M_HEADS, 8, LANES), _F32)],
        compiler_params=_params("parallel", "arbitrary"),
        name="mlstm",
    )(qm.reshape(batch, seq, M_WIDTH), km.reshape(batch, seq, M_WIDTH), vm.reshape(batch, seq, M_WIDTH),
      gates.reshape(batch, seq, LANES), gate_b, head_g.reshape(1, -1))
    return out.reshape(batch * seq, M_WIDTH)


def _merge_kernel(x_ref, o0_ref, o1_ref, o2_ref, l0_ref, l1_ref, l2_ref, hm_ref, pre_g_ref, wg_ref,
                  wa_ref, wm_ref, wo_ref, post_g_ref, out_ref):
    x = x_ref[...]
    h = (_rms(x) * pre_g_ref[...]).astype(_BF16)
    lses = [r[...] for r in (l0_ref, l1_ref, l2_ref)]
    top = jnp.maximum(jnp.maximum(lses[0], lses[1]), lses[2])
    es = [jnp.exp(l - top) for l in lses]
    inv = 1.0 / (es[0] + es[1] + es[2])
    alphas = [e * inv for e in es]
    heads = []
    for hh in range(GROUP_HEADS):
        acc = None
        for alpha, o_ref in zip(alphas, (o0_ref, o1_ref, o2_ref)):
            term = alpha[:, hh:hh + 1] * o_ref[:, hh * HEAD_DIM:(hh + 1) * HEAD_DIM].astype(_F32)
            acc = term if acc is None else acc + term
        heads.append(acc.astype(_BF16))
    att = jnp.concatenate(heads, axis=1)
    a = _sigmoid(_dot(h, wg_ref[:, M_WIDTH:M_WIDTH + D_MODEL])) * _dot(att, wa_ref[...])
    ml = (_sigmoid(_dot(h, wg_ref[:, :M_WIDTH])) * hm_ref[...].astype(_F32)).astype(_BF16)
    m = _sigmoid(_dot(h, wg_ref[:, M_WIDTH + D_MODEL:])) * _dot(ml, wm_ref[...])
    y = _dot((a + m).astype(_BF16), wo_ref[...])
    out_ref[...] = x + _rms(y) * post_g_ref[...]


def _merge(x2d, outs, lses, hm, pre_g, w_gates, w_att, w_ml, w_out, post_g):
    t, d = x2d.shape
    row = lambda width: pl.BlockSpec((TOKEN_TILE, width), lambda i: (i, 0))
    return pl.pallas_call(
        _merge_kernel,
        grid=(t // TOKEN_TILE,),
        in_specs=[row(d)] + [row(GROUP_WIDTH)] * 3 + [row(LANES)] * 3 + [row(M_WIDTH), _resident((1, d)),
                  _resident(w_gates.shape), _resident(w_att.shape), _resident(w_ml.shape),
                  _resident(w_out.shape), _resident((1, d))],
        out_specs=row(d),
        out_shape=jax.ShapeDtypeStruct((t, d), _F32),
        compiler_params=_params("parallel"),
        name="merge",
    )(x2d, *outs, *lses, hm, pre_g.reshape(1, d), w_gates, w_att.astype(_BF16), w_ml.astype(_BF16),
      w_out.astype(_BF16), post_g.reshape(1, d))


def kernel(x, ffn1_pre_g, ffn1_w_gate, ffn1_w_up, ffn1_w_down, ffn1_post_g, mix_pre_g, w_in, conv_w, conv_b, mlstm_i_bias, mlstm_f_bias, mlstm_head_g, w_att_branch, w_mlstm_branch, w_out, mix_post_g, ffn2_pre_g, ffn2_w_gate, ffn2_w_up, ffn2_w_down, ffn2_post_g):
    batch, seq, d = x.shape
    t = batch * seq
    xt = x.reshape(t, d)
    for l in range(ffn1_pre_g.shape[0]):
        xt = _ffn(xt, ffn1_pre_g[l], ffn1_w_gate[l], ffn1_w_up[l], ffn1_w_down[l], ffn1_post_g[l])
        w_proj, w_gates = _split_w_in(w_in[l])
        qkv0, qkv1, qkv2, qm, km, vm, gates = _inproj(xt, seq, mix_pre_g[l], w_proj, conv_w[l], conv_b[l])
        outs, lses = zip(*(_attention_group(qkv, batch, seq, dil)
                           for qkv, dil in zip((qkv0, qkv1, qkv2), GROUP_DILATIONS)))
        outs = [o.reshape(t, GROUP_WIDTH) for o in outs]
        lses = [s.reshape(t, LANES) for s in lses]
        hm = _mlstm(qm, km, vm, gates, batch, seq, mlstm_i_bias[l], mlstm_f_bias[l], mlstm_head_g[l])
        xt = _merge(xt, outs, lses, hm, mix_pre_g[l], w_gates, w_att_branch[l], w_mlstm_branch[l], w_out[l],
                    mix_post_g[l])
        xt = _ffn(xt, ffn2_pre_g[l], ffn2_w_gate[l], ffn2_w_up[l], ffn2_w_down[l], ffn2_post_g[l])
    return xt.reshape(batch, seq, d)
```

```python
import functools
import math

import numpy as np
import jax
import jax.numpy as jnp
from jax import lax
from jax.experimental import pallas as pl
from jax.experimental.pallas import tpu as pltpu

D_MODEL = 1024
N_GROUPS = 3
GROUP_DILATIONS = (1, 4, 16)
ATT_SPAN = 128
GROUP_HEADS = 4
HEAD_DIM = 128
GROUP_WIDTH = GROUP_HEADS * HEAD_DIM
ATT_WIDTH = N_GROUPS * GROUP_WIDTH
ATT_BLOCK = 128
ROPE_THETA = 500000.0
ROPE_DIMS = HEAD_DIM // 4
ROPE_HALF = ROPE_DIMS // 2
M_HEADS = 4
M_WIDTH = D_MODEL
M_HEAD_DIM = M_WIDTH // M_HEADS
M_CHUNK = 256
M_BATCH = 2
CONV_WIDTH = 4
CONV_HALO = 8
CONV_COLS = 512
RMS_EPS = 1e-6
LANES = 128
MASK_BIAS = -1e30
VMEM_LIMIT_BYTES = 58 * 1024 * 1024
TOKEN_TILE = 512
FFN_SPLIT = 1536
WEIGHT_CHUNKS = 16

_BF16 = jnp.bfloat16
_F32 = jnp.float32
_NT = (((1,), (1,)), ((), ()))
_TN = (((0,), (0,)), ((), ()))


def _rms(x):
    return x * lax.rsqrt(jnp.mean(x * x, axis=-1, keepdims=True) + RMS_EPS)


def _sigmoid(x):
    return 1.0 / (1.0 + jnp.exp(-x))


def _dot(a, b):
    return jnp.dot(a, b, preferred_element_type=_F32)


def _resident(shape):
    return pl.BlockSpec(shape, lambda *_: (0,) * len(shape), pipeline_mode=pl.Buffered(1))


def _params(*semantics):
    return pltpu.CompilerParams(dimension_semantics=semantics, vmem_limit_bytes=VMEM_LIMIT_BYTES)


def _cast_rows_in(src_hbm, dst_ref, stage_ref, sem_ref):
    chunk = stage_ref.shape[1]
    n = dst_ref.shape[0] // chunk

    def copy(c):
        return pltpu.make_async_copy(src_hbm.at[pl.ds(c * chunk, chunk), :], stage_ref.at[c % 2], sem_ref.at[c % 2])

    copy(0).start()
    for c in range(n):
        if c + 1 < n:
            copy(c + 1).start()
        copy(c).wait()
        dst_ref[c * chunk:(c + 1) * chunk, :] = stage_ref[c % 2].astype(_BF16)


def _ffn_kernel(x0_ref, xn_ref, xr_ref, pre_g_ref, post_g_ref, wg_hbm, wu_hbm, wd_hbm, o_ref,
                wg_ref, wu_ref, wd_ref, stage_in_ref, stage_out_ref, sem_ref, hn_ref, f_ref):
    i = pl.program_id(0)
    n = pl.num_programs(0) - 1
    d_ff = wg_ref.shape[1]

    def prep(slot, x_ref):
        hn_ref[slot] = (_rms(x_ref[...]) * pre_g_ref[...]).astype(_BF16)

    def matmuls(slot, vpu_filler=None):
        f = None
        for lo, hi in ((0, FFN_SPLIT), (FFN_SPLIT, d_ff)):
            h = hn_ref[slot]
            g = _dot(h, wg_ref[:, lo:hi])
            u = _dot(h, wu_ref[:, lo:hi])
            if lo == 0 and vpu_filler is not None:
                vpu_filler()
            part = _dot((g * _sigmoid(g) * u).astype(_BF16), wd_ref[lo:hi, :])
            f = part if f is None else f + part
        f_ref[slot] = f

    def finish(slot):
        o_ref[...] = xr_ref[...] + 0.5 * (_rms(f_ref[slot]) * post_g_ref[...])

    @pl.when(i == 0)
    def _():
        _cast_rows_in(wg_hbm, wg_ref, stage_in_ref, sem_ref)
        _cast_rows_in(wu_hbm, wu_ref, stage_in_ref, sem_ref)
        _cast_rows_in(wd_hbm, wd_ref, stage_out_ref, sem_ref)
        prep(0, x0_ref)
        prep(1, xn_ref)
        matmuls(0)

    @pl.when((i > 0) & (i < n))
    def _():
        cur = i % 2

        def norms():
            finish(1 - cur)
            prep(1 - cur, xn_ref)

        matmuls(cur, norms)

    @pl.when(i == n)
    def _():
        finish((n - 1) % 2)


def _ffn(x2d, pre_g, w_gate, w_up, w_down, post_g):
    t, d = x2d.shape
    f = w_gate.shape[1]
    n = t // TOKEN_TILE
    tile = lambda index: pl.BlockSpec((TOKEN_TILE, d), index)
    hbm = pl.BlockSpec(memory_space=pl.ANY)
    return pl.pallas_call(
        _ffn_kernel,
        grid=(n + 1,),
        in_specs=[pl.BlockSpec((TOKEN_TILE, d), lambda i: (0, 0), pipeline_mode=pl.Buffered(1)),
                  tile(lambda i: (jnp.minimum(i + 1, n - 1), 0)), tile(lambda i: (jnp.maximum(i - 1, 0), 0)),
                  _resident((1, d)), _resident((1, d)), hbm, hbm, hbm],
        out_specs=tile(lambda i: (jnp.maximum(i - 1, 0), 0)),
        out_shape=jax.ShapeDtypeStruct((t, d), _F32),
        scratch_shapes=[pltpu.VMEM((d, f), _BF16), pltpu.VMEM((d, f), _BF16), pltpu.VMEM((f, d), _BF16),
                        pltpu.VMEM((2, d // WEIGHT_CHUNKS, f), _F32), pltpu.VMEM((2, f // WEIGHT_CHUNKS, d), _F32),
                        pltpu.SemaphoreType.DMA((2,)),
                        pltpu.VMEM((2, TOKEN_TILE, d), _BF16), pltpu.VMEM((2, TOKEN_TILE, d), _F32)],
        compiler_params=_params("arbitrary"),
        name="ffn",
    )(x2d, x2d, x2d, pre_g.reshape(1, d), post_g.reshape(1, d), w_gate, w_up, w_down)


def _inproj_kernel(x_ref, g_ref, w_ref, tab0_ref, tab1_ref, tab2_ref, conv_w_ref, conv_b_ref,
                   qkv0_ref, qkv1_ref, qkv2_ref, qm_ref, km_ref, vm_ref, gates_ref,
                   hn_ref, hs_ref, hp_ref, xs_ref, halo_ref, *, tiles_per_seq):
    tm = TOKEN_TILE
    n_slabs = D_MODEL // LANES
    hf = _rms(x_ref[...]) * g_ref[...]
    hn_ref[...] = hf.astype(_BF16)
    for c in range(n_slabs):
        hs_ref[c] = hf[:, c * LANES:(c + 1) * LANES]

    lane = lax.broadcasted_iota(jnp.int32, (tm, HEAD_DIM), 1)
    low_half = lane < ROPE_HALF
    q_scale = HEAD_DIM ** -0.5 * math.log2(math.e)
    conv_col0 = N_GROUPS * ATT_WIDTH
    vm_col0 = conv_col0 + 2 * M_WIDTH
    gate_col0 = vm_col0 + M_WIDTH

    def attention_group(g, qkv_ref, tab_ref, d):
        rows = tm // d
        if d == 1:
            h_ref = hn_ref
        else:
            for r in range(d):
                piece = jnp.concatenate([hs_ref[c, pl.ds(r, rows, stride=d), :] for c in range(n_slabs)], axis=1)
                hp_ref[r * rows:(r + 1) * rows, :] = piece.astype(_BF16)
            h_ref = hp_ref
        cos = tab_ref[0]
        sin = tab_ref[1]
        for part in range(3):
            col = (g * 3 + part) * GROUP_WIDTH
            p = _dot(h_ref[...], w_ref[:, col:col + GROUP_WIDTH])
            for hh in range(GROUP_HEADS):
                x = p[:, hh * HEAD_DIM:(hh + 1) * HEAD_DIM]
                if part < 2:
                    partner = jnp.where(low_half, pltpu.roll(x, HEAD_DIM - ROPE_HALF, 1),
                                        pltpu.roll(x, ROPE_HALF, 1))
                    x = x * cos + partner * sin
                if part == 0:
                    x = x * q_scale
                x = x.astype(_BF16)
                for r in range(d):
                    lo = r * ATT_WIDTH + part * GROUP_WIDTH + hh * HEAD_DIM
                    qkv_ref[:, lo:lo + HEAD_DIM] = x[r * rows:(r + 1) * rows, :]

    def conv_chunk(cc):
        col = conv_col0 + cc * CONV_COLS
        p = _dot(hn_ref[...], w_ref[:, col:col + CONV_COLS])
        for c in range(CONV_COLS // LANES):
            lanes = slice(cc * CONV_COLS + c * LANES, cc * CONV_COLS + (c + 1) * LANES)
            pc = p[:, c * LANES:(c + 1) * LANES]
            xs_ref[c, pl.ds(0, CONV_HALO, stride=2), :] = halo_ref[:, lanes]
            xs_ref[c, pl.ds(2 * CONV_HALO, tm, stride=2), :] = pc
            halo_ref[:, lanes] = pc[tm - CONV_HALO:tm, :]
            y = conv_b_ref[:, lanes] + conv_w_ref[CONV_WIDTH - 1:CONV_WIDTH, lanes] * pc
            for j in range(CONV_WIDTH - 1):
                shift = CONV_WIDTH - 1 - j
                y = y + conv_w_ref[j:j + 1, lanes] * xs_ref[c, pl.ds(2 * (CONV_HALO - shift), tm, stride=2), :]
            qk = y * _sigmoid(y)
            if lanes.start < M_WIDTH:
                qm_ref[:, lanes] = qk.astype(_BF16)
            else:
                km_ref[:, lanes.start - M_WIDTH:lanes.stop - M_WIDTH] = (qk * M_HEAD_DIM ** -0.5).astype(_BF16)

    @pl.when(pl.program_id(0) % tiles_per_seq == 0)
    def _():
        halo_ref[...] = jnp.zeros(halo_ref.shape, _F32)

    vm_ref[...] = _dot(hn_ref[...], w_ref[:, vm_col0:vm_col0 + M_WIDTH]).astype(_BF16)
    gates_ref[...] = _dot(hn_ref[...], w_ref[:, gate_col0:gate_col0 + LANES])
    attention_group(0, qkv0_ref, tab0_ref, GROUP_DILATIONS[0])
    conv_chunk(0)
    conv_chunk(1)
    attention_group(1, qkv1_ref, tab1_ref, GROUP_DILATIONS[1])
    conv_chunk(2)
    conv_chunk(3)
    attention_group(2, qkv2_ref, tab2_ref, GROUP_DILATIONS[2])


def _rope_tables(seq, dilation):
    inv_freq = jnp.power(ROPE_THETA, -(jnp.arange(ROPE_HALF, dtype=_F32) * 2.0 / ROPE_DIMS))
    ang = jnp.arange(seq, dtype=jnp.int32).astype(_F32)[:, None] * inv_freq[None, :]
    cos, sin = jnp.cos(ang), jnp.sin(ang)
    rest = HEAD_DIM - ROPE_DIMS
    tabs = jnp.stack([jnp.concatenate([cos, cos, jnp.ones((seq, rest), _F32)], axis=-1),
                      jnp.concatenate([-sin, sin, jnp.zeros((seq, rest), _F32)], axis=-1)])
    tabs = tabs.reshape(2, seq // TOKEN_TILE, TOKEN_TILE // dilation, dilation, HEAD_DIM)
    return tabs.transpose(0, 1, 3, 2, 4).reshape(2, seq, HEAD_DIM)


def _split_w_in(w_in):
    a, m = ATT_WIDTH, M_WIDTH
    cols = []
    for g in range(N_GROUPS):
        for part in range(3):
            lo = part * a + g * GROUP_WIDTH
            cols.append(w_in[:, lo:lo + GROUP_WIDTH])
    base = 3 * a
    cols.append(w_in[:, base:base + 3 * m])
    gate_lo = base + 4 * m
    branch_lo = gate_lo + 2 * M_HEADS
    cols.append(jnp.pad(w_in[:, gate_lo:branch_lo], ((0, 0), (0, LANES - 2 * M_HEADS))))
    w_proj = jnp.concatenate(cols, axis=1).astype(_BF16)
    w_gates = jnp.concatenate([w_in[:, base + 3 * m:gate_lo], w_in[:, branch_lo:branch_lo + 2 * D_MODEL]],
                              axis=1).astype(_BF16)
    return w_proj, w_gates


def _inproj(x2d, seq, pre_g, w_proj, conv_w, conv_b):
    t, d = x2d.shape
    tiles_per_seq = seq // TOKEN_TILE
    tables = [_rope_tables(seq, dil) for dil in GROUP_DILATIONS]
    row = lambda width: pl.BlockSpec((TOKEN_TILE, width), lambda i: (i, 0))
    cls = lambda dil: pl.BlockSpec((TOKEN_TILE // dil, dil * ATT_WIDTH), lambda i: (i, 0))
    tab = pl.BlockSpec((2, TOKEN_TILE, HEAD_DIM), lambda i: (0, i % tiles_per_seq, 0))
    out_shape = [jax.ShapeDtypeStruct((t // dil, dil * ATT_WIDTH), _BF16) for dil in GROUP_DILATIONS]
    out_shape += [jax.ShapeDtypeStruct((t, M_WIDTH), _BF16)] * 3
    out_shape.append(jax.ShapeDtypeStruct((t, LANES), _F32))
    return pl.pallas_call(
        functools.partial(_inproj_kernel, tiles_per_seq=tiles_per_seq),
        grid=(t // TOKEN_TILE,),
        in_specs=[row(d), _resident((1, d)), _resident(w_proj.shape), tab, tab, tab,
                  _resident((CONV_WIDTH, 2 * M_WIDTH)), _resident((1, 2 * M_WIDTH))],
        out_specs=[cls(dil) for dil in GROUP_DILATIONS] + [row(M_WIDTH)] * 3 + [row(LANES)],
        out_shape=out_shape,
        scratch_shapes=[pltpu.VMEM((TOKEN_TILE, d), _BF16),
                        pltpu.VMEM((d // LANES, TOKEN_TILE, LANES), _F32),
                        pltpu.VMEM((TOKEN_TILE, d), _BF16),
                        pltpu.VMEM((CONV_COLS // LANES, 2 * (CONV_HALO + TOKEN_TILE), LANES), _F32),
                        pltpu.VMEM((CONV_HALO, 2 * M_WIDTH), _F32)],
        compiler_params=_params("arbitrary"),
        name="inproj",
    )(x2d, pre_g.reshape(1, d), w_proj, *tables, conv_w, conv_b.reshape(1, -1))


def _attn_kernel(qkv_ref, bias_two_ref, bias_one_ref, o_ref, lse_ref, *scratch, dilation, n_blocks):
    lane = lax.broadcasted_iota(jnp.int32, (ATT_BLOCK, LANES), 1)

    def block(row0, key0, n_keys, bias_ref):
        ones = jnp.ones((n_keys, HEAD_DIM), _BF16)
        for r in range(dilation):
            scores = []
            for hh in range(GROUP_HEADS):
                qc = r * ATT_WIDTH + hh * HEAD_DIM
                q = qkv_ref[pl.ds(row0, ATT_BLOCK), qc:qc + HEAD_DIM]
                k = qkv_ref[pl.ds(key0, n_keys), qc + GROUP_WIDTH:qc + GROUP_WIDTH + HEAD_DIM]
                scores.append(lax.dot_general(q, k, _NT, preferred_element_type=_F32) + bias_ref[...])
            tops = [jnp.max(s, axis=1, keepdims=True) for s in scores]
            probs = [jnp.exp2(s - m).astype(_BF16) for s, m in zip(scores, tops)]
            lse_tile = jnp.zeros((ATT_BLOCK, LANES), _F32)
            out_row = row0 * dilation + r
            for hh in range(GROUP_HEADS):
                vc = r * ATT_WIDTH + 2 * GROUP_WIDTH + hh * HEAD_DIM
                v = qkv_ref[pl.ds(key0, n_keys), vc:vc + HEAD_DIM]
                acc = _dot(probs[hh], jnp.concatenate([v, ones], axis=1))
                denom = acc[:, HEAD_DIM:]
                o = acc[:, :HEAD_DIM] * (1.0 / denom)
                lse_tile = jnp.where(lane == hh, tops[hh] + jnp.log2(denom), lse_tile)
                if dilation == 1:
                    o_ref[pl.ds(row0, ATT_BLOCK), hh * HEAD_DIM:(hh + 1) * HEAD_DIM] = o.astype(_BF16)
                else:
                    scratch[0][hh, pl.ds(out_row, ATT_BLOCK, stride=dilation), :] = o
            lse_tile = lse_tile * math.log(2.0)
            if dilation == 1:
                lse_ref[pl.ds(row0, ATT_BLOCK), :] = lse_tile
            else:
                lse_ref[pl.ds(out_row, ATT_BLOCK, stride=dilation), :] = lse_tile

    block(0, 0, ATT_BLOCK, bias_one_ref)

    def body(i, carry):
        row0 = pl.multiple_of(i * ATT_BLOCK, ATT_BLOCK)
        key0 = pl.multiple_of((i - 1) * ATT_BLOCK, ATT_BLOCK)
        block(row0, key0, 2 * ATT_BLOCK, bias_two_ref)
        return carry

    if n_blocks > 1:
        lax.fori_loop(1, n_blocks, body, 0)
    if dilation > 1:
        for hh in range(GROUP_HEADS):
            o_ref[:, hh * HEAD_DIM:(hh + 1) * HEAD_DIM] = scratch[0][hh].astype(_BF16)


def _band_bias():
    qi = np.arange(ATT_BLOCK)[:, None]
    kj = np.arange(2 * ATT_BLOCK)[None, :]
    valid_two = np.where(kj < ATT_BLOCK, kj >= qi, kj - ATT_BLOCK <= qi)
    valid_one = np.arange(ATT_BLOCK)[None, :] <= qi
    to_bias = lambda valid: jnp.asarray(np.where(valid, 0.0, MASK_BIAS), _F32)
    return to_bias(valid_two), to_bias(valid_one)


def _attention_group(qkv, batch, seq, dilation):
    length = seq // dilation
    view = qkv.reshape(batch, length, dilation * ATT_WIDTH)
    bias_two, bias_one = _band_bias()
    scratch = [] if dilation == 1 else [pltpu.VMEM((GROUP_HEADS, seq, HEAD_DIM), _F32)]
    return pl.pallas_call(
        functools.partial(_attn_kernel, dilation=dilation, n_blocks=length // ATT_BLOCK),
        grid=(batch,),
        in_specs=[pl.BlockSpec((None, length, dilation * ATT_WIDTH), lambda b: (b, 0, 0)),
                  _resident(bias_two.shape), _resident(bias_one.shape)],
        out_specs=[pl.BlockSpec((None, seq, GROUP_WIDTH), lambda b: (b, 0, 0)),
                   pl.BlockSpec((None, seq, LANES), lambda b: (b, 0, 0))],
        out_shape=[jax.ShapeDtypeStruct((batch, seq, GROUP_WIDTH), _BF16),
                   jax.ShapeDtypeStruct((batch, seq, LANES), _F32)],
        scratch_shapes=scratch,
        compiler_params=_params("parallel"),
        name=f"attn_d{dilation}",
    )(view, bias_two, bias_one)


def _mlstm_kernel(qm_ref, km_ref, vm_ref, gates_ref, gate_b_ref, head_g_ref, o_ref, state_ref, m_ref):
    lc = M_CHUNK
    dh = M_HEAD_DIM
    wide = lambda a, n: jnp.concatenate([a] * n, axis=1)

    @pl.when(pl.program_id(1) == 0)
    def _():
        state_ref[...] = jnp.zeros(state_ref.shape, _F32)
        m_ref[...] = jnp.zeros(m_ref.shape, _F32)

    causal = (lax.broadcasted_iota(jnp.int32, (lc, lc), 1) <= lax.broadcasted_iota(jnp.int32, (lc, lc), 0))
    tri = causal.astype(_BF16)
    lane = lax.broadcasted_iota(jnp.int32, (lc, LANES), 1)
    ones = jnp.ones((lc, LANES), _BF16)

    for bb in range(M_BATCH):
        gt = gates_ref[bb] + gate_b_ref[...]
        lf = jnp.minimum(gt, 0.0) - jnp.log(1.0 + jnp.exp(-jnp.abs(gt)))
        hi = lf.astype(_BF16)
        rest = lf - hi.astype(_F32)
        mid = rest.astype(_BF16)
        low = (rest - mid.astype(_F32)).astype(_BF16)
        b_all = _dot(tri, hi) + _dot(tri, mid) + _dot(tri, low)
        ib = jnp.where(lane < M_HEADS, gt, b_all)
        ib_t = ib.T

        for hh in range(M_HEADS):
            hs = slice(hh * dh, (hh + 1) * dh)
            q = qm_ref[bb, :, hs]
            k = km_ref[bb, :, hs]
            v_aug = jnp.concatenate([vm_ref[bb, :, hs], ones], axis=1)
            i_rep = jnp.broadcast_to(ib[:, hh:hh + 1], (lc, LANES))
            b_rep = jnp.broadcast_to(ib[:, M_HEADS + hh:M_HEADS + hh + 1], (lc, LANES))
            u_row = ib_t[hh:hh + 1, :] - ib_t[M_HEADS + hh:M_HEADS + hh + 1, :]
            m_prev = m_ref[bb, hh][0:1, :]
            state = state_ref[bb, hh]

            e = jnp.where(causal, u_row, -jnp.inf)
            big_m = jnp.maximum(m_prev, jnp.broadcast_to(jnp.max(e, axis=1, keepdims=True), (lc, LANES)))
            w = jnp.exp(e - wide(big_m, lc // LANES)) * lax.dot_general(q, k, _NT, preferred_element_type=_F32)
            inter = jnp.exp(m_prev - big_m)
            acc = _dot(w.astype(_BF16), v_aug) + wide(inter, dh // LANES + 1) * _dot(q, state.astype(_BF16))
            den = acc[:, dh:]
            inv = 1.0 / jnp.maximum(jnp.abs(den), jnp.exp(-(b_rep + big_m)))
            h_out = acc[:, :dh] * wide(inv, dh // LANES)

            b_last = b_rep[lc - 1:lc, :]
            g = b_last - b_rep + i_rep
            m_new = jnp.maximum(b_last + m_prev, jnp.max(g, axis=0, keepdims=True))
            a = jnp.exp(g - m_new)
            decay = jnp.exp(b_last + m_prev - m_new)
            av = (wide(a, dh // LANES + 1) * v_aug.astype(_F32)).astype(_BF16)
            state_ref[bb, hh] = (wide(decay, dh // LANES + 1) * state
                                 + lax.dot_general(k, av, _TN, preferred_element_type=_F32))
            m_ref[bb, hh] = jnp.broadcast_to(m_new, m_ref.shape[2:])

            o_ref[bb, :, hs] = (_rms(h_out) * head_g_ref[:, hs]).astype(_BF16)


def _mlstm(qm, km, vm, gates, batch, seq, i_bias, f_bias, head_g):
    n_chunks = seq // M_CHUNK
    gate_b = jnp.pad(jnp.concatenate([i_bias, f_bias]), (0, LANES - 2 * M_HEADS)).reshape(1, LANES)
    chunk = lambda width: pl.BlockSpec((M_BATCH, M_CHUNK, width), lambda b, c: (b, c, 0))
    out = pl.pallas_call(
        _mlstm_kernel,
        grid=(batch // M_BATCH, n_chunks),
        in_specs=[chunk(M_WIDTH), chunk(M_WIDTH), chunk(M_WIDTH), chunk(LANES),
                  _resident((1, LANES)), _resident((1, M_WIDTH))],
        out_specs=chunk(M_WIDTH),
        out_shape=jax.ShapeDtypeStruct((batch, seq, M_WIDTH), _BF16),
        scratch_shapes=[pltpu.VMEM((M_BATCH, M_HEADS, M_HEAD_DIM, M_HEAD_DIM + LANES), _F32),
                        pltpu.VMEM((M_BATCH, M_HEADS, 8, LANES), _F32)],
        compiler_params=_params("parallel", "arbitrary"),
        name="mlstm",
    )(qm.reshape(batch, seq, M_WIDTH), km.reshape(batch, seq, M_WIDTH), vm.reshape(batch, seq, M_WIDTH),
      gates.reshape(batch, seq, LANES), gate_b, head_g.reshape(1, -1))
    return out.reshape(batch * seq, M_WIDTH)


def _merge_kernel(x_ref, o0_ref, o1_ref, o2_ref, l0_ref, l1_ref, l2_ref, hm_ref, pre_g_ref, wg_ref,
                  wa_ref, wm_ref, wo_ref, post_g_ref, out_ref):
    x = x_ref[...]
    h = (_rms(x) * pre_g_ref[...]).astype(_BF16)
    lses = [r[...] for r in (l0_ref, l1_ref, l2_ref)]
    top = jnp.maximum(jnp.maximum(lses[0], lses[1]), lses[2])
    es = [jnp.exp(l - top) for l in lses]
    inv = 1.0 / (es[0] + es[1] + es[2])
    alphas = [e * inv for e in es]
    heads = []
    for hh in range(GROUP_HEADS):
        acc = None
        for alpha, o_ref in zip(alphas, (o0_ref, o1_ref, o2_ref)):
            term = alpha[:, hh:hh + 1] * o_ref[:, hh * HEAD_DIM:(hh + 1) * HEAD_DIM].astype(_F32)
            acc = term if acc is None else acc + term
        heads.append(acc.astype(_BF16))
    att = jnp.concatenate(heads, axis=1)
    a = _sigmoid(_dot(h, wg_ref[:, M_WIDTH:M_WIDTH + D_MODEL])) * _dot(att, wa_ref[...])
    ml = (_sigmoid(_dot(h, wg_ref[:, :M_WIDTH])) * hm_ref[...].astype(_F32)).astype(_BF16)
    m = _sigmoid(_dot(h, wg_ref[:, M_WIDTH + D_MODEL:])) * _dot(ml, wm_ref[...])
    y = _dot((a + m).astype(_BF16), wo_ref[...])
    out_ref[...] = x + _rms(y) * post_g_ref[...]


def _merge(x2d, outs, lses, hm, pre_g, w_gates, w_att, w_ml, w_out, post_g):
    t, d = x2d.shape
    row = lambda width: pl.BlockSpec((TOKEN_TILE, width), lambda i: (i, 0))
    return pl.pallas_call(
        _merge_kernel,
        grid=(t // TOKEN_TILE,),
        in_specs=[row(d)] + [row(GROUP_WIDTH)] * 3 + [row(LANES)] * 3 + [row(M_WIDTH), _resident((1, d)),
                  _resident(w_gates.shape), _resident(w_att.shape), _resident(w_ml.shape),
                  _resident(w_out.shape), _resident((1, d))],
        out_specs=row(d),
        out_shape=jax.ShapeDtypeStruct((t, d), _F32),
        compiler_params=_params("parallel"),
        name="merge",
    )(x2d, *outs, *lses, hm, pre_g.reshape(1, d), w_gates, w_att.astype(_BF16), w_ml.astype(_BF16),
      w_out.astype(_BF16), post_g.reshape(1, d))


def kernel(x, ffn1_pre_g, ffn1_w_gate, ffn1_w_up, ffn1_w_down, ffn1_post_g, mix_pre_g, w_in, conv_w, conv_b, mlstm_i_bias, mlstm_f_bias, mlstm_head_g, w_att_branch, w_mlstm_branch, w_out, mix_post_g, ffn2_pre_g, ffn2_w_gate, ffn2_w_up, ffn2_w_down, ffn2_post_g):
    batch, seq, d = x.shape
    t = batch * seq
    xt = x.reshape(t, d)
    for l in range(ffn1_pre_g.shape[0]):
        xt = _ffn(xt, ffn1_pre_g[l], ffn1_w_gate[l], ffn1_w_up[l], ffn1_w_down[l], ffn1_post_g[l])
        w_proj, w_gates = _split_w_in(w_in[l])
        qkv0, qkv1, qkv2, qm, km, vm, gates = _inproj(xt, seq, mix_pre_g[l], w_proj, conv_w[l], conv_b[l])
        outs, lses = zip(*(_attention_group(qkv, batch, seq, dil)
                           for qkv, dil in zip((qkv0, qkv1, qkv2), GROUP_DILATIONS)))
        outs = [o.reshape(t, GROUP_WIDTH) for o in outs]
        lses = [s.reshape(t, LANES) for s in lses]
        hm = _mlstm(qm, km, vm, gates, batch, seq, mlstm_i_bias[l], mlstm_f_bias[l], mlstm_head_g[l])
        xt = _merge(xt, outs, lses, hm, mix_pre_g[l], w_gates, w_att_branch[l], w_mlstm_branch[l], w_out[l],
                    mix_post_g[l])
        xt = _ffn(xt, ffn2_pre_g[l], ffn2_w_gate[l], ffn2_w_up[l], ffn2_w_down[l], ffn2_post_g[l])
    return xt.reshape(batch, seq, d)
```

```python
import functools
import math

import numpy as np
import jax
import jax.numpy as jnp
from jax import lax
from jax.experimental import pallas as pl
from jax.experimental.pallas import tpu as pltpu

D_MODEL = 1024
N_GROUPS = 3
GROUP_DILATIONS = (1, 4, 16)
ATT_SPAN = 128
GROUP_HEADS = 4
HEAD_DIM = 128
GROUP_WIDTH = GROUP_HEADS * HEAD_DIM
ATT_WIDTH = N_GROUPS * GROUP_WIDTH
ATT_BLOCK = 128
ROPE_THETA = 500000.0
ROPE_DIMS = HEAD_DIM // 4
ROPE_HALF = ROPE_DIMS // 2
M_HEADS = 4
M_WIDTH = D_MODEL
M_HEAD_DIM = M_WIDTH // M_HEADS
M_CHUNK = 256
M_BATCH = 2
CONV_WIDTH = 4
CONV_HALO = 8
CONV_COLS = 512
RMS_EPS = 1e-6
LANES = 128
MASK_BIAS = -1e30
VMEM_LIMIT_BYTES = 58 * 1024 * 1024
TOKEN_TILE = 512
STAGE_SHAPE = (4, 256, 1024)

_BF16 = jnp.bfloat16
_F32 = jnp.float32
_NT = (((1,), (1,)), ((), ()))
_TN = (((0,), (0,)), ((), ()))


def _rms(x):
    return x * lax.rsqrt(jnp.mean(x * x, axis=-1, keepdims=True) + RMS_EPS)


def _sigmoid(x):
    return 1.0 / (1.0 + jnp.exp(-x))


def _dot(a, b):
    return jnp.dot(a, b, preferred_element_type=_F32)


def _resident(shape):
    return pl.BlockSpec(shape, lambda *_: (0,) * len(shape), pipeline_mode=pl.Buffered(1))


def _params(*semantics):
    return pltpu.CompilerParams(dimension_semantics=semantics, vmem_limit_bytes=VMEM_LIMIT_BYTES)


def _cast_weight_in(src_hbm, dst_ref, stage_ref, sem_ref, windows=None):
    slots, stage_rows, stage_cols = stage_ref.shape
    rows, cols = dst_ref.shape
    if windows is None:
        windows = [(c, c, min(stage_cols, cols - c), min(stage_cols, cols - c)) for c in range(0, cols, stage_cols)]
    jobs = [(r,) + win for r in range(0, rows, stage_rows) for win in windows]

    def copy(k):
        r, src, _, w, _ = jobs[k]
        return pltpu.make_async_copy(src_hbm.at[pl.ds(r, stage_rows), pl.ds(src, w)],
                                     stage_ref.at[k % slots, :, pl.ds(0, w)], sem_ref.at[k % slots])

    for k in range(min(slots - 1, len(jobs))):
        copy(k).start()
    for k, (r, _, dst, w, valid) in enumerate(jobs):
        if k + slots - 1 < len(jobs):
            copy(k + slots - 1).start()
        copy(k).wait()
        block = stage_ref[k % slots, :, 0:w]
        if valid < w:
            block = jnp.where(lax.broadcasted_iota(jnp.int32, block.shape, 1) < valid, block, 0.0)
        dst_ref[r:r + stage_rows, dst:dst + w] = block.astype(_BF16)


def _ffn_kernel(x_ref, pre_g_ref, post_g_ref, wg_hbm, wu_hbm, wd_hbm, o_ref,
                wg_ref, wu_ref, wd_ref, stage_ref, sem_ref):
    @pl.when(pl.program_id(0) == 0)
    def _():
        _cast_weight_in(wg_hbm, wg_ref, stage_ref, sem_ref)
        _cast_weight_in(wu_hbm, wu_ref, stage_ref, sem_ref)
        _cast_weight_in(wd_hbm, wd_ref, stage_ref, sem_ref)

    x = x_ref[...]
    h = (_rms(x) * pre_g_ref[...]).astype(_BF16)
    g = _dot(h, wg_ref[...])
    u = _dot(h, wu_ref[...])
    a = (g * _sigmoid(g) * u).astype(_BF16)
    f = _dot(a, wd_ref[...])
    o_ref[...] = x + 0.5 * (_rms(f) * post_g_ref[...])


def _weight_scratch(*shapes):
    return ([pltpu.VMEM(shape, _BF16) for shape in shapes]
            + [pltpu.VMEM(STAGE_SHAPE, _F32), pltpu.SemaphoreType.DMA((STAGE_SHAPE[0],))])


_HBM = pl.BlockSpec(memory_space=pl.ANY)


def _ffn(x2d, pre_g, w_gate, w_up, w_down, post_g):
    t, d = x2d.shape
    row = pl.BlockSpec((TOKEN_TILE, d), lambda i: (i, 0))
    return pl.pallas_call(
        _ffn_kernel,
        grid=(t // TOKEN_TILE,),
        in_specs=[row, _resident((1, d)), _resident((1, d)), _HBM, _HBM, _HBM],
        out_specs=row,
        out_shape=jax.ShapeDtypeStruct((t, d), _F32),
        scratch_shapes=_weight_scratch(w_gate.shape, w_up.shape, w_down.shape),
        compiler_params=_params("arbitrary"),
        name="ffn",
    )(x2d, pre_g.reshape(1, d), post_g.reshape(1, d), w_gate, w_up, w_down)


def _inproj_kernel(x_ref, g_ref, w_hbm, tab0_ref, tab1_ref, tab2_ref, conv_w_ref, conv_b_ref,
                   qkv0_ref, qkv1_ref, qkv2_ref, qm_ref, km_ref, vm_ref, gates_ref,
                   w_ref, stage_ref, sem_ref, hn_ref, hs_ref, hp1_ref, hp2_ref, xs_ref, halo_ref, *, tiles_per_seq):
    tm = TOKEN_TILE
    n_slabs = D_MODEL // LANES

    @pl.when(pl.program_id(0) == 0)
    def _():
        _cast_weight_in(w_hbm, w_ref, stage_ref, sem_ref, _inproj_windows())

    @pl.when(pl.program_id(0) % tiles_per_seq == 0)
    def _():
        halo_ref[...] = jnp.zeros(halo_ref.shape, _F32)

    hf = _rms(x_ref[...]) * g_ref[...]
    hn_ref[...] = hf.astype(_BF16)
    for c in range(n_slabs):
        hs_ref[c] = hf[:, c * LANES:(c + 1) * LANES]

    lane = lax.broadcasted_iota(jnp.int32, (tm, HEAD_DIM), 1)
    low_half = lane < ROPE_HALF
    q_scale = HEAD_DIM ** -0.5 * math.log2(math.e)
    conv_col0 = N_GROUPS * ATT_WIDTH
    vm_col0 = conv_col0 + 2 * M_WIDTH
    gate_col0 = vm_col0 + M_WIDTH

    def regroup(hp_ref, d):
        rows = tm // d
        for r in range(d):
            piece = jnp.concatenate([hs_ref[c, pl.ds(r, rows, stride=d), :] for c in range(n_slabs)], axis=1)
            hp_ref[r * rows:(r + 1) * rows, :] = piece.astype(_BF16)

    def attention_part(g, part, h_ref):
        col = (g * 3 + part) * GROUP_WIDTH
        return _dot(h_ref[...], w_ref[:, col:col + GROUP_WIDTH])

    def attention_store(p, part, qkv_ref, tab_ref, d):
        rows = tm // d
        for hh in range(GROUP_HEADS):
            x = p[:, hh * HEAD_DIM:(hh + 1) * HEAD_DIM]
            if part < 2:
                partner = jnp.where(low_half, pltpu.roll(x, HEAD_DIM - ROPE_HALF, 1), pltpu.roll(x, ROPE_HALF, 1))
                x = x * tab_ref[0] + partner * tab_ref[1]
            if part == 0:
                x = x * q_scale
            x = x.astype(_BF16)
            for r in range(d):
                lo = r * ATT_WIDTH + part * GROUP_WIDTH + hh * HEAD_DIM
                qkv_ref[:, lo:lo + HEAD_DIM] = x[r * rows:(r + 1) * rows, :]

    def conv_dot(cc):
        col = conv_col0 + cc * CONV_COLS
        return _dot(hn_ref[...], w_ref[:, col:col + CONV_COLS])

    def conv_store(p, cc):
        for c in range(CONV_COLS // LANES):
            lanes = slice(cc * CONV_COLS + c * LANES, cc * CONV_COLS + (c + 1) * LANES)
            pc = p[:, c * LANES:(c + 1) * LANES]
            xs_ref[c, pl.ds(0, CONV_HALO, stride=2), :] = halo_ref[:, lanes]
            xs_ref[c, pl.ds(2 * CONV_HALO, tm, stride=2), :] = pc
            halo_ref[:, lanes] = pc[tm - CONV_HALO:tm, :]
            y = conv_b_ref[:, lanes] + conv_w_ref[CONV_WIDTH - 1:CONV_WIDTH, lanes] * pc
            for j in range(CONV_WIDTH - 1):
                shift = CONV_WIDTH - 1 - j
                y = y + conv_w_ref[j:j + 1, lanes] * xs_ref[c, pl.ds(2 * (CONV_HALO - shift), tm, stride=2), :]
            qk = y * _sigmoid(y)
            if lanes.start < M_WIDTH:
                qm_ref[:, lanes] = qk.astype(_BF16)
            else:
                km_ref[:, lanes.start - M_WIDTH:lanes.stop - M_WIDTH] = (qk * M_HEAD_DIM ** -0.5).astype(_BF16)

    def vm_store(p):
        vm_ref[...] = p.astype(_BF16)

    def gates_store(p):
        gates_ref[...] = p

    hps = (hn_ref, hp1_ref, hp2_ref)
    qkvs = (qkv0_ref, qkv1_ref, qkv2_ref)
    tabs = (tab0_ref, tab1_ref, tab2_ref)
    sections = [(lambda: _dot(hn_ref[...], w_ref[:, vm_col0:vm_col0 + M_WIDTH]), vm_store),
                (lambda: _dot(hn_ref[...], w_ref[:, gate_col0:gate_col0 + LANES]),
                 lambda p: (gates_store(p), regroup(hp1_ref, GROUP_DILATIONS[1])))]
    for g in range(N_GROUPS):
        for part in range(3):
            sections.append((functools.partial(attention_part, g, part, hps[g]),
                             functools.partial(attention_store, part=part, qkv_ref=qkvs[g], tab_ref=tabs[g],
                                               d=GROUP_DILATIONS[g])))
        if g == 0:
            sections.append((functools.partial(conv_dot, 0),
                             lambda p: (conv_store(p, 0), regroup(hp2_ref, GROUP_DILATIONS[2]))))
            sections.append((functools.partial(conv_dot, 1), functools.partial(conv_store, cc=1)))
        if g == 1:
            sections.append((functools.partial(conv_dot, 2), functools.partial(conv_store, cc=2)))
            sections.append((functools.partial(conv_dot, 3), functools.partial(conv_store, cc=3)))
    pending = None
    for matmul, epilogue in sections:
        p = matmul()
        if pending is not None:
            pending()
        pending = functools.partial(epilogue, p)
    pending()


def _inproj_windows():
    a, m = ATT_WIDTH, M_WIDTH
    wins = []
    for g in range(N_GROUPS):
        for part in range(3):
            wins.append((part * a + g * GROUP_WIDTH, (g * 3 + part) * GROUP_WIDTH, GROUP_WIDTH, GROUP_WIDTH))
    base = 3 * a
    for c in range(0, 3 * m, STAGE_SHAPE[2]):
        wins.append((base + c, base + c, STAGE_SHAPE[2], STAGE_SHAPE[2]))
    wins.append((base + 4 * m, base + 3 * m, LANES, 2 * M_HEADS))
    return wins


def _rope_tables(seq, dilation):
    inv_freq = jnp.power(ROPE_THETA, -(jnp.arange(ROPE_HALF, dtype=_F32) * 2.0 / ROPE_DIMS))
    ang = jnp.arange(seq, dtype=jnp.int32).astype(_F32)[:, None] * inv_freq[None, :]
    cos, sin = jnp.cos(ang), jnp.sin(ang)
    rest = HEAD_DIM - ROPE_DIMS
    tabs = jnp.stack([jnp.concatenate([cos, cos, jnp.ones((seq, rest), _F32)], axis=-1),
                      jnp.concatenate([-sin, sin, jnp.zeros((seq, rest), _F32)], axis=-1)])
    tabs = tabs.reshape(2, seq // TOKEN_TILE, TOKEN_TILE // dilation, dilation, HEAD_DIM)
    return tabs.transpose(0, 1, 3, 2, 4).reshape(2, seq, HEAD_DIM)


def _inproj(x2d, seq, pre_g, w_in, conv_w, conv_b):
    t, d = x2d.shape
    tiles_per_seq = seq // TOKEN_TILE
    w_cols = N_GROUPS * ATT_WIDTH + 3 * M_WIDTH + LANES
    tables = [_rope_tables(seq, dil) for dil in GROUP_DILATIONS]
    row = lambda width: pl.BlockSpec((TOKEN_TILE, width), lambda i: (i, 0))
    cls = lambda dil: pl.BlockSpec((TOKEN_TILE // dil, dil * ATT_WIDTH), lambda i: (i, 0))
    tab = pl.BlockSpec((2, TOKEN_TILE, HEAD_DIM), lambda i: (0, i % tiles_per_seq, 0))
    out_shape = [jax.ShapeDtypeStruct((t // dil, dil * ATT_WIDTH), _BF16) for dil in GROUP_DILATIONS]
    out_shape += [jax.ShapeDtypeStruct((t, M_WIDTH), _BF16)] * 3
    out_shape.append(jax.ShapeDtypeStruct((t, LANES), _F32))
    return pl.pallas_call(
        functools.partial(_inproj_kernel, tiles_per_seq=tiles_per_seq),
        grid=(t // TOKEN_TILE,),
        in_specs=[row(d), _resident((1, d)), _HBM, tab, tab, tab,
                  _resident((CONV_WIDTH, 2 * M_WIDTH)), _resident((1, 2 * M_WIDTH))],
        out_specs=[cls(dil) for dil in GROUP_DILATIONS] + [row(M_WIDTH)] * 3 + [row(LANES)],
        out_shape=out_shape,
        scratch_shapes=_weight_scratch((d, w_cols)) + [
            pltpu.VMEM((TOKEN_TILE, d), _BF16),
            pltpu.VMEM((d // LANES, TOKEN_TILE, LANES), _F32),
            pltpu.VMEM((TOKEN_TILE, d), _BF16),
            pltpu.VMEM((TOKEN_TILE, d), _BF16),
            pltpu.VMEM((CONV_COLS // LANES, 2 * (CONV_HALO + TOKEN_TILE), LANES), _F32),
            pltpu.VMEM((CONV_HALO, 2 * M_WIDTH), _F32)],
        compiler_params=_params("arbitrary"),
        name="inproj",
    )(x2d, pre_g.reshape(1, d), w_in, *tables, conv_w, conv_b.reshape(1, -1))


def _attn_kernel(qkv_ref, bias_two_ref, bias_one_ref, o_ref, lse_ref, *scratch, dilation, n_blocks):
    lane = lax.broadcasted_iota(jnp.int32, (ATT_BLOCK, LANES), 1)

    def block(row0, key0, n_keys, bias_ref):
        ones = jnp.ones((n_keys, HEAD_DIM), _BF16)
        for r in range(dilation):
            scores = []
            for hh in range(GROUP_HEADS):
                qc = r * ATT_WIDTH + hh * HEAD_DIM
                q = qkv_ref[pl.ds(row0, ATT_BLOCK), qc:qc + HEAD_DIM]
                k = qkv_ref[pl.ds(key0, n_keys), qc + GROUP_WIDTH:qc + GROUP_WIDTH + HEAD_DIM]
                scores.append(lax.dot_general(q, k, _NT, preferred_element_type=_F32) + bias_ref[...])
            tops = [jnp.max(s, axis=1, keepdims=True) for s in scores]
            probs = [jnp.exp2(s - m).astype(_BF16) for s, m in zip(scores, tops)]
            lse_tile = jnp.zeros((ATT_BLOCK, LANES), _F32)
            out_row = row0 * dilation + r
            for hh in range(GROUP_HEADS):
                vc = r * ATT_WIDTH + 2 * GROUP_WIDTH + hh * HEAD_DIM
                v = qkv_ref[pl.ds(key0, n_keys), vc:vc + HEAD_DIM]
                acc = _dot(probs[hh], jnp.concatenate([v, ones], axis=1))
                denom = acc[:, HEAD_DIM:]
                o = acc[:, :HEAD_DIM] * (1.0 / denom)
                lse_tile = jnp.where(lane == hh, tops[hh] + jnp.log2(denom), lse_tile)
                if dilation == 1:
                    o_ref[pl.ds(row0, ATT_BLOCK), hh * HEAD_DIM:(hh + 1) * HEAD_DIM] = o.astype(_BF16)
                else:
                    scratch[0][hh, pl.ds(out_row, ATT_BLOCK, stride=dilation), :] = o
            lse_tile = lse_tile * math.log(2.0)
            if dilation == 1:
                lse_ref[pl.ds(row0, ATT_BLOCK), :] = lse_tile
            else:
                lse_ref[pl.ds(out_row, ATT_BLOCK, stride=dilation), :] = lse_tile

    block(0, 0, ATT_BLOCK, bias_one_ref)

    def body(i, carry):
        row0 = pl.multiple_of(i * ATT_BLOCK, ATT_BLOCK)
        key0 = pl.multiple_of((i - 1) * ATT_BLOCK, ATT_BLOCK)
        block(row0, key0, 2 * ATT_BLOCK, bias_two_ref)
        return carry

    if n_blocks > 1:
        lax.fori_loop(1, n_blocks, body, 0)
    if dilation > 1:
        for hh in range(GROUP_HEADS):
            o_ref[:, hh * HEAD_DIM:(hh + 1) * HEAD_DIM] = scratch[0][hh].astype(_BF16)


def _band_bias():
    qi = np.arange(ATT_BLOCK)[:, None]
    kj = np.arange(2 * ATT_BLOCK)[None, :]
    valid_two = np.where(kj < ATT_BLOCK, kj >= qi, kj - ATT_BLOCK <= qi)
    valid_one = np.arange(ATT_BLOCK)[None, :] <= qi
    to_bias = lambda valid: jnp.asarray(np.where(valid, 0.0, MASK_BIAS), _F32)
    return to_bias(valid_two), to_bias(valid_one)


def _attention_group(qkv, batch, seq, dilation):
    length = seq // dilation
    view = qkv.reshape(batch, length, dilation * ATT_WIDTH)
    bias_two, bias_one = _band_bias()
    scratch = [] if dilation == 1 else [pltpu.VMEM((GROUP_HEADS, seq, HEAD_DIM), _F32)]
    return pl.pallas_call(
        functools.partial(_attn_kernel, dilation=dilation, n_blocks=length // ATT_BLOCK),
        grid=(batch,),
        in_specs=[pl.BlockSpec((None, length, dilation * ATT_WIDTH), lambda b: (b, 0, 0)),
                  _resident(bias_two.shape), _resident(bias_one.shape)],
        out_specs=[pl.BlockSpec((None, seq, GROUP_WIDTH), lambda b: (b, 0, 0)),
                   pl.BlockSpec((None, seq, LANES), lambda b: (b, 0, 0))],
        out_shape=[jax.ShapeDtypeStruct((batch, seq, GROUP_WIDTH), _BF16),
                   jax.ShapeDtypeStruct((batch, seq, LANES), _F32)],
        scratch_shapes=scratch,
        compiler_params=_params("parallel"),
        name=f"attn_d{dilation}",
    )(view, bias_two, bias_one)


def _mlstm_kernel(qm_ref, km_ref, vm_ref, gates_ref, gate_b_ref, head_g_ref, o_ref, state_ref, m_ref):
    lc = M_CHUNK
    dh = M_HEAD_DIM
    wide = lambda a, n: jnp.concatenate([a] * n, axis=1)

    @pl.when(pl.program_id(1) == 0)
    def _():
        state_ref[...] = jnp.zeros(state_ref.shape, _F32)
        m_ref[...] = jnp.zeros(m_ref.shape, _F32)

    causal = (lax.broadcasted_iota(jnp.int32, (lc, lc), 1) <= lax.broadcasted_iota(jnp.int32, (lc, lc), 0))
    tri = causal.astype(_BF16)
    lane = lax.broadcasted_iota(jnp.int32, (lc, LANES), 1)
    ones = jnp.ones((lc, LANES), _BF16)

    for bb in range(M_BATCH):
        gt = gates_ref[bb] + gate_b_ref[...]
        lf = jnp.minimum(gt, 0.0) - jnp.log(1.0 + jnp.exp(-jnp.abs(gt)))
        hi = lf.astype(_BF16)
        rest = lf - hi.astype(_F32)
        mid = rest.astype(_BF16)
        low = (rest - mid.astype(_F32)).astype(_BF16)
        b_all = _dot(tri, hi) + _dot(tri, mid) + _dot(tri, low)
        ib = jnp.where(lane < M_HEADS, gt, b_all)
        ib_t = ib.T

        for hh in range(M_HEADS):
            hs = slice(hh * dh, (hh + 1) * dh)
            q = qm_ref[bb, :, hs]
            k = km_ref[bb, :, hs]
            v_aug = jnp.concatenate([vm_ref[bb, :, hs], ones], axis=1)
            i_rep = jnp.broadcast_to(ib[:, hh:hh + 1], (lc, LANES))
            b_rep = jnp.broadcast_to(ib[:, M_HEADS + hh:M_HEADS + hh + 1], (lc, LANES))
            u_row = ib_t[hh:hh + 1, :] - ib_t[M_HEADS + hh:M_HEADS + hh + 1, :]
            m_prev = m_ref[bb, hh][0:1, :]
            state = state_ref[bb, hh]

            e = jnp.where(causal, u_row, -jnp.inf)
            big_m = jnp.maximum(m_prev, jnp.broadcast_to(jnp.max(e, axis=1, keepdims=True), (lc, LANES)))
            w = jnp.exp(e - wide(big_m, lc // LANES)) * lax.dot_general(q, k, _NT, preferred_element_type=_F32)
            inter = jnp.exp(m_prev - big_m)
            acc = _dot(w.astype(_BF16), v_aug) + wide(inter, dh // LANES + 1) * _dot(q, state.astype(_BF16))
            den = acc[:, dh:]
            inv = 1.0 / jnp.maximum(jnp.abs(den), jnp.exp(-(b_rep + big_m)))
            h_out = acc[:, :dh] * wide(inv, dh // LANES)

            b_last = b_rep[lc - 1:lc, :]
            g = b_last - b_rep + i_rep
            m_new = jnp.maximum(b_last + m_prev, jnp.max(g, axis=0, keepdims=True))
            a = jnp.exp(g - m_new)
            decay = jnp.exp(b_last + m_prev - m_new)
            av = (wide(a, dh // LANES + 1) * v_aug.astype(_F32)).astype(_BF16)
            state_ref[bb, hh] = (wide(decay, dh // LANES + 1) * state
                                 + lax.dot_general(k, av, _TN, preferred_element_type=_F32))
            m_ref[bb, hh] = jnp.broadcast_to(m_new, m_ref.shape[2:])

            o_ref[bb, :, hs] = (_rms(h_out) * head_g_ref[:, hs]).astype(_BF16)


def _mlstm(qm, km, vm, gates, batch, seq, i_bias, f_bias, head_g):
    n_chunks = seq // M_CHUNK
    gate_b = jnp.pad(jnp.concatenate([i_bias, f_bias]), (0, LANES - 2 * M_HEADS)).reshape(1, LANES)
    chunk = lambda width: pl.BlockSpec((M_BATCH, M_CHUNK, width), lambda b, c: (b, c, 0))
    out = pl.pallas_call(
        _mlstm_kernel,
        grid=(batch // M_BATCH, n_chunks),
        in_specs=[chunk(M_WIDTH), chunk(M_WIDTH), chunk(M_WIDTH), chunk(LANES),
                  _resident((1, LANES)), _resident((1, M_WIDTH))],
        out_specs=chunk(M_WIDTH),
        out_shape=jax.ShapeDtypeStruct((batch, seq, M_WIDTH), _BF16),
        scratch_shapes=[pltpu.VMEM((M_BATCH, M_HEADS, M_HEAD_DIM, M_HEAD_DIM + LANES), _F32),
                        pltpu.VMEM((M_BATCH, M_HEADS, 8, LANES), _F32)],
        compiler_params=_params("parallel", "arbitrary"),
        name="mlstm",
    )(qm.reshape(batch, seq, M_WIDTH), km.reshape(batch, seq, M_WIDTH), vm.reshape(batch, seq, M_WIDTH),
      gates.reshape(batch, seq, LANES), gate_b, head_g.reshape(1, -1))
    return out.reshape(batch * seq, M_WIDTH)


def _merge_kernel(x_ref, o0_ref, o1_ref, o2_ref, l0_ref, l1_ref, l2_ref, hm_ref, pre_g_ref, post_g_ref,
                  w_in_hbm, wga_hbm, wgm_hbm, wa_hbm, wm_hbm, wo_hbm, out_ref,
                  wg_ref, wa_ref, wm_ref, wo_ref, stage_ref, sem_ref):
    @pl.when(pl.program_id(0) == 0)
    def _():
        o_gate_col0 = N_GROUPS * ATT_WIDTH + 3 * M_WIDTH
        _cast_weight_in(w_in_hbm, wg_ref, stage_ref, sem_ref, [(o_gate_col0, 0, M_WIDTH, M_WIDTH)])
        _cast_weight_in(wga_hbm, wg_ref, stage_ref, sem_ref, [(0, M_WIDTH, D_MODEL, D_MODEL)])
        _cast_weight_in(wgm_hbm, wg_ref, stage_ref, sem_ref, [(0, M_WIDTH + D_MODEL, D_MODEL, D_MODEL)])
        _cast_weight_in(wa_hbm, wa_ref, stage_ref, sem_ref)
        _cast_weight_in(wm_hbm, wm_ref, stage_ref, sem_ref)
        _cast_weight_in(wo_hbm, wo_ref, stage_ref, sem_ref)

    x = x_ref[...]
    h = (_rms(x) * pre_g_ref[...]).astype(_BF16)
    lses = [r[...] for r in (l0_ref, l1_ref, l2_ref)]
    top = jnp.maximum(jnp.maximum(lses[0], lses[1]), lses[2])
    es = [jnp.exp(l - top) for l in lses]
    inv = 1.0 / (es[0] + es[1] + es[2])
    alphas = [e * inv for e in es]
    heads = []
    for hh in range(GROUP_HEADS):
        acc = None
        for alpha, o_ref in zip(alphas, (o0_ref, o1_ref, o2_ref)):
            term = alpha[:, hh:hh + 1] * o_ref[:, hh * HEAD_DIM:(hh + 1) * HEAD_DIM].astype(_F32)
            acc = term if acc is None else acc + term
        heads.append(acc.astype(_BF16))
    att = jnp.concatenate(heads, axis=1)
    a = _sigmoid(_dot(h, wg_ref[:, M_WIDTH:M_WIDTH + D_MODEL])) * _dot(att, wa_ref[...])
    ml = (_sigmoid(_dot(h, wg_ref[:, :M_WIDTH])) * hm_ref[...].astype(_F32)).astype(_BF16)
    m = _sigmoid(_dot(h, wg_ref[:, M_WIDTH + D_MODEL:])) * _dot(ml, wm_ref[...])
    y = _dot((a + m).astype(_BF16), wo_ref[...])
    out_ref[...] = x + _rms(y) * post_g_ref[...]


def _merge(x2d, outs, lses, hm, pre_g, w_in, w_att, w_ml, w_out, post_g):
    t, d = x2d.shape
    branch_col0 = N_GROUPS * ATT_WIDTH + 4 * M_WIDTH + 2 * M_HEADS
    w_ga = w_in[:, branch_col0:branch_col0 + D_MODEL]
    w_gm = w_in[:, branch_col0 + D_MODEL:branch_col0 + 2 * D_MODEL]
    row = lambda width: pl.BlockSpec((TOKEN_TILE, width), lambda i: (i, 0))
    return pl.pallas_call(
        _merge_kernel,
        grid=(t // TOKEN_TILE,),
        in_specs=[row(d)] + [row(GROUP_WIDTH)] * 3 + [row(LANES)] * 3 + [row(M_WIDTH), _resident((1, d)),
                  _resident((1, d))] + [_HBM] * 6,
        out_specs=row(d),
        out_shape=jax.ShapeDtypeStruct((t, d), _F32),
        scratch_shapes=_weight_scratch((d, M_WIDTH + 2 * D_MODEL), w_att.shape, w_ml.shape, w_out.shape),
        compiler_params=_params("arbitrary"),
        name="merge",
    )(x2d, *outs, *lses, hm, pre_g.reshape(1, d), post_g.reshape(1, d), w_in, w_ga, w_gm, w_att, w_ml, w_out)


def kernel(x, ffn1_pre_g, ffn1_w_gate, ffn1_w_up, ffn1_w_down, ffn1_post_g, mix_pre_g, w_in, conv_w, conv_b, mlstm_i_bias, mlstm_f_bias, mlstm_head_g, w_att_branch, w_mlstm_branch, w_out, mix_post_g, ffn2_pre_g, ffn2_w_gate, ffn2_w_up, ffn2_w_down, ffn2_post_g):
    batch, seq, d = x.shape
    t = batch * seq
    xt = x.reshape(t, d)
    for l in range(ffn1_pre_g.shape[0]):
        xt = _ffn(xt, ffn1_pre_g[l], ffn1_w_gate[l], ffn1_w_up[l], ffn1_w_down[l], ffn1_post_g[l])
        qkv0, qkv1, qkv2, qm, km, vm, gates = _inproj(xt, seq, mix_pre_g[l], w_in[l], conv_w[l], conv_b[l])
        outs, lses = zip(*(_attention_group(qkv, batch, seq, dil)
                           for qkv, dil in zip((qkv0, qkv1, qkv2), GROUP_DILATIONS)))
        outs = [o.reshape(t, GROUP_WIDTH) for o in outs]
        lses = [s.reshape(t, LANES) for s in lses]
        hm = _mlstm(qm, km, vm, gates, batch, seq, mlstm_i_bias[l], mlstm_f_bias[l], mlstm_head_g[l])
        xt = _merge(xt, outs, lses, hm, mix_pre_g[l], w_in[l], w_att_branch[l], w_mlstm_branch[l], w_out[l],
                    mix_post_g[l])
        xt = _ffn(xt, ffn2_pre_g[l], ffn2_w_gate[l], ffn2_w_up[l], ffn2_w_down[l], ffn2_post_g[l])
    return xt.reshape(batch, seq, d)
```

```python
import functools
import math

import numpy as np
import jax
import jax.numpy as jnp
from jax import lax
from jax.experimental import pallas as pl
from jax.experimental.pallas import tpu as pltpu

D_MODEL = 1024
N_GROUPS = 3
GROUP_DILATIONS = (1, 4, 16)
ATT_SPAN = 128
GROUP_HEADS = 4
HEAD_DIM = 128
GROUP_WIDTH = GROUP_HEADS * HEAD_DIM
ATT_WIDTH = N_GROUPS * GROUP_WIDTH
ATT_BLOCK = 128
ROPE_THETA = 500000.0
ROPE_DIMS = HEAD_DIM // 4
ROPE_HALF = ROPE_DIMS // 2
M_HEADS = 4
M_WIDTH = D_MODEL
M_HEAD_DIM = M_WIDTH // M_HEADS
M_CHUNK = 256
M_BATCH = 2
CONV_WIDTH = 4
CONV_HALO = 8
CONV_COLS = 512
RMS_EPS = 1e-6
LANES = 128
MASK_BIAS = -1e30
VMEM_LIMIT_BYTES = 58 * 1024 * 1024
TOKEN_TILE = 512
STAGE_SHAPE = (4, 256, 1024)

_BF16 = jnp.bfloat16
_F32 = jnp.float32
_NT = (((1,), (1,)), ((), ()))
_TN = (((0,), (0,)), ((), ()))


def _rms(x):
    return x * lax.rsqrt(jnp.mean(x * x, axis=-1, keepdims=True) + RMS_EPS)


def _sigmoid(x):
    return 1.0 / (1.0 + jnp.exp(-x))


def _dot(a, b):
    return jnp.dot(a, b, preferred_element_type=_F32)


def _resident(shape):
    return pl.BlockSpec(shape, lambda *_: (0,) * len(shape), pipeline_mode=pl.Buffered(1))


def _params(*semantics):
    return pltpu.CompilerParams(dimension_semantics=semantics, vmem_limit_bytes=VMEM_LIMIT_BYTES)


def _cast_weight_in(src_hbm, dst_ref, stage_ref, sem_ref, jobs=None):
    slots, stage_rows, stage_cols = stage_ref.shape
    if jobs is None:
        rows, cols = dst_ref.shape
        jobs = [(r, c, r, c, min(stage_rows, rows - r), min(stage_cols, cols - c))
                for r in range(0, rows, stage_rows) for c in range(0, cols, stage_cols)]

    def copy(k):
        src_row, src_col, _, _, nr, nc = jobs[k]
        return pltpu.make_async_copy(src_hbm.at[pl.ds(src_row, nr), pl.ds(src_col, nc)],
                                     stage_ref.at[k % slots, pl.ds(0, nr), pl.ds(0, nc)], sem_ref.at[k % slots])

    for k in range(min(slots - 1, len(jobs))):
        copy(k).start()
    for k, (_, _, dst_row, dst_col, nr, nc) in enumerate(jobs):
        if k + slots - 1 < len(jobs):
            copy(k + slots - 1).start()
        copy(k).wait()
        dst_ref[dst_row:dst_row + nr, dst_col:dst_col + nc] = stage_ref[k % slots, 0:nr, 0:nc].astype(_BF16)


def _row_jobs(windows, stage_rows):
    return [(src + r, 0, dst + r, 0, min(stage_rows, n - r), STAGE_SHAPE[2])
            for src, dst, n in windows for r in range(0, n, stage_rows)]


def _ffn_kernel(x_ref, pre_g_ref, post_g_ref, wg_hbm, wu_hbm, wd_hbm, o_ref,
                wg_ref, wu_ref, wd_ref, stage_ref, sem_ref):
    @pl.when(pl.program_id(0) == 0)
    def _():
        _cast_weight_in(wg_hbm, wg_ref, stage_ref, sem_ref)
        _cast_weight_in(wu_hbm, wu_ref, stage_ref, sem_ref)
        _cast_weight_in(wd_hbm, wd_ref, stage_ref, sem_ref)

    x = x_ref[...]
    h = (_rms(x) * pre_g_ref[...]).astype(_BF16)
    g = _dot(h, wg_ref[...])
    u = _dot(h, wu_ref[...])
    a = (g * _sigmoid(g) * u).astype(_BF16)
    f = _dot(a, wd_ref[...])
    o_ref[...] = x + 0.5 * (_rms(f) * post_g_ref[...])


def _weight_scratch(*shapes):
    return ([pltpu.VMEM(shape, _BF16) for shape in shapes]
            + [pltpu.VMEM(STAGE_SHAPE, _F32), pltpu.SemaphoreType.DMA((STAGE_SHAPE[0],))])


_HBM = pl.BlockSpec(memory_space=pl.ANY)


def _ffn(x2d, pre_g, w_gate, w_up, w_down, post_g):
    t, d = x2d.shape
    row = pl.BlockSpec((TOKEN_TILE, d), lambda i: (i, 0))
    return pl.pallas_call(
        _ffn_kernel,
        grid=(t // TOKEN_TILE,),
        in_specs=[row, _resident((1, d)), _resident((1, d)), _HBM, _HBM, _HBM],
        out_specs=row,
        out_shape=jax.ShapeDtypeStruct((t, d), _F32),
        scratch_shapes=_weight_scratch(w_gate.shape, w_up.shape, w_down.shape),
        compiler_params=_params("arbitrary"),
        name="ffn",
    )(x2d, pre_g.reshape(1, d), post_g.reshape(1, d), w_gate, w_up, w_down)


def _inproj_kernel(x_ref, g_ref, w_hbm, tab0_ref, tab1_ref, tab2_ref, conv_w_ref, conv_b_ref,
                   qkv0_ref, qkv1_ref, qkv2_ref, qm_ref, km_ref, vm_ref, gates_ref,
                   w_ref, stage_ref, sem_ref, hn_ref, hs_ref, hp1_ref, hp2_ref, xs_ref, halo_ref, *, tiles_per_seq):
    tm = TOKEN_TILE
    n_slabs = D_MODEL // LANES
    conv_col0 = N_GROUPS * ATT_WIDTH
    vm_col0 = conv_col0 + 2 * M_WIDTH
    gate_col0 = vm_col0 + M_WIDTH

    def project(h_ref, col, width):
        return lax.dot_general(h_ref[...], w_ref[col:col + width, :], _NT, preferred_element_type=_F32)

    @pl.when(pl.program_id(0) == 0)
    def _():
        w_ref[gate_col0 + 2 * M_HEADS:gate_col0 + LANES, :] = jnp.zeros((LANES - 2 * M_HEADS, D_MODEL), _BF16)
        _cast_weight_in(w_hbm, w_ref, stage_ref, sem_ref, _row_jobs(_inproj_windows(), STAGE_SHAPE[1]))

    @pl.when(pl.program_id(0) % tiles_per_seq == 0)
    def _():
        halo_ref[...] = jnp.zeros(halo_ref.shape, _F32)

    hf = _rms(x_ref[...]) * g_ref[...]
    hn_ref[...] = hf.astype(_BF16)
    for c in range(n_slabs):
        hs_ref[c] = hf[:, c * LANES:(c + 1) * LANES]

    lane = lax.broadcasted_iota(jnp.int32, (tm, HEAD_DIM), 1)
    low_half = lane < ROPE_HALF
    q_scale = HEAD_DIM ** -0.5 * math.log2(math.e)

    def regroup(hp_ref, d):
        rows = tm // d
        for r in range(d):
            piece = jnp.concatenate([hs_ref[c, pl.ds(r, rows, stride=d), :] for c in range(n_slabs)], axis=1)
            hp_ref[r * rows:(r + 1) * rows, :] = piece.astype(_BF16)

    def attention_part(g, part, h_ref):
        return project(h_ref, (g * 3 + part) * GROUP_WIDTH, GROUP_WIDTH)

    def attention_store(p, part, qkv_ref, tab_ref, d):
        rows = tm // d
        for hh in range(GROUP_HEADS):
            x = p[:, hh * HEAD_DIM:(hh + 1) * HEAD_DIM]
            if part < 2:
                partner = jnp.where(low_half, pltpu.roll(x, HEAD_DIM - ROPE_HALF, 1), pltpu.roll(x, ROPE_HALF, 1))
                x = x * tab_ref[0] + partner * tab_ref[1]
            if part == 0:
                x = x * q_scale
            x = x.astype(_BF16)
            for r in range(d):
                lo = r * ATT_WIDTH + part * GROUP_WIDTH + hh * HEAD_DIM
                qkv_ref[:, lo:lo + HEAD_DIM] = x[r * rows:(r + 1) * rows, :]

    def conv_dot(cc):
        return project(hn_ref, conv_col0 + cc * CONV_COLS, CONV_COLS)

    def conv_store(p, cc):
        for c in range(CONV_COLS // LANES):
            lanes = slice(cc * CONV_COLS + c * LANES, cc * CONV_COLS + (c + 1) * LANES)
            pc = p[:, c * LANES:(c + 1) * LANES]
            xs_ref[c, pl.ds(0, CONV_HALO, stride=2), :] = halo_ref[:, lanes]
            xs_ref[c, pl.ds(2 * CONV_HALO, tm, stride=2), :] = pc
            halo_ref[:, lanes] = pc[tm - CONV_HALO:tm, :]
            y = conv_b_ref[:, lanes] + conv_w_ref[CONV_WIDTH - 1:CONV_WIDTH, lanes] * pc
            for j in range(CONV_WIDTH - 1):
                shift = CONV_WIDTH - 1 - j
                y = y + conv_w_ref[j:j + 1, lanes] * xs_ref[c, pl.ds(2 * (CONV_HALO - shift), tm, stride=2), :]
            qk = y * _sigmoid(y)
            if lanes.start < M_WIDTH:
                qm_ref[:, lanes] = qk.astype(_BF16)
            else:
                km_ref[:, lanes.start - M_WIDTH:lanes.stop - M_WIDTH] = (qk * M_HEAD_DIM ** -0.5).astype(_BF16)

    def vm_store(p):
        vm_ref[...] = p.astype(_BF16)

    def gates_store(p):
        gates_ref[...] = p

    hps = (hn_ref, hp1_ref, hp2_ref)
    qkvs = (qkv0_ref, qkv1_ref, qkv2_ref)
    tabs = (tab0_ref, tab1_ref, tab2_ref)
    sections = [(lambda: project(hn_ref, vm_col0, M_WIDTH), vm_store),
                (lambda: project(hn_ref, gate_col0, LANES),
                 lambda p: (gates_store(p), regroup(hp1_ref, GROUP_DILATIONS[1])))]
    for g in range(N_GROUPS):
        for part in range(3):
            sections.append((functools.partial(attention_part, g, part, hps[g]),
                             functools.partial(attention_store, part=part, qkv_ref=qkvs[g], tab_ref=tabs[g],
                                               d=GROUP_DILATIONS[g])))
        if g == 0:
            sections.append((functools.partial(conv_dot, 0),
                             lambda p: (conv_store(p, 0), regroup(hp2_ref, GROUP_DILATIONS[2]))))
            sections.append((functools.partial(conv_dot, 1), functools.partial(conv_store, cc=1)))
        if g == 1:
            sections.append((functools.partial(conv_dot, 2), functools.partial(conv_store, cc=2)))
            sections.append((functools.partial(conv_dot, 3), functools.partial(conv_store, cc=3)))
    pending = None
    for matmul, epilogue in sections:
        p = matmul()
        if pending is not None:
            pending()
        pending = functools.partial(epilogue, p)
    pending()


def _inproj_windows():
    a, m = ATT_WIDTH, M_WIDTH
    wins = []
    for g in range(N_GROUPS):
        for part in range(3):
            wins.append((part * a + g * GROUP_WIDTH, (g * 3 + part) * GROUP_WIDTH, GROUP_WIDTH))
    base = 3 * a
    wins.append((base, base, 3 * m))
    wins.append((base + 4 * m, base + 3 * m, 2 * M_HEADS))
    return wins


def _rope_tables(seq, dilation):
    inv_freq = jnp.power(ROPE_THETA, -(jnp.arange(ROPE_HALF, dtype=_F32) * 2.0 / ROPE_DIMS))
    ang = jnp.arange(seq, dtype=jnp.int32).astype(_F32)[:, None] * inv_freq[None, :]
    cos, sin = jnp.cos(ang), jnp.sin(ang)
    rest = HEAD_DIM - ROPE_DIMS
    tabs = jnp.stack([jnp.concatenate([cos, cos, jnp.ones((seq, rest), _F32)], axis=-1),
                      jnp.concatenate([-sin, sin, jnp.zeros((seq, rest), _F32)], axis=-1)])
    tabs = tabs.reshape(2, seq // TOKEN_TILE, TOKEN_TILE // dilation, dilation, HEAD_DIM)
    return tabs.transpose(0, 1, 3, 2, 4).reshape(2, seq, HEAD_DIM)


def _inproj(x2d, seq, pre_g, w_in, conv_w, conv_b):
    t, d = x2d.shape
    tiles_per_seq = seq // TOKEN_TILE
    w_cols = N_GROUPS * ATT_WIDTH + 3 * M_WIDTH + LANES
    tables = [_rope_tables(seq, dil) for dil in GROUP_DILATIONS]
    row = lambda width: pl.BlockSpec((TOKEN_TILE, width), lambda i: (i, 0))
    cls = lambda dil: pl.BlockSpec((TOKEN_TILE // dil, dil * ATT_WIDTH), lambda i: (i, 0))
    tab = pl.BlockSpec((2, TOKEN_TILE, HEAD_DIM), lambda i: (0, i % tiles_per_seq, 0))
    out_shape = [jax.ShapeDtypeStruct((t // dil, dil * ATT_WIDTH), _BF16) for dil in GROUP_DILATIONS]
    out_shape += [jax.ShapeDtypeStruct((t, M_WIDTH), _BF16)] * 3
    out_shape.append(jax.ShapeDtypeStruct((t, LANES), _F32))
    return pl.pallas_call(
        functools.partial(_inproj_kernel, tiles_per_seq=tiles_per_seq),
        grid=(t // TOKEN_TILE,),
        in_specs=[row(d), _resident((1, d)), _HBM, tab, tab, tab,
                  _resident((CONV_WIDTH, 2 * M_WIDTH)), _resident((1, 2 * M_WIDTH))],
        out_specs=[cls(dil) for dil in GROUP_DILATIONS] + [row(M_WIDTH)] * 3 + [row(LANES)],
        out_shape=out_shape,
        scratch_shapes=_weight_scratch((w_cols, d)) + [
            pltpu.VMEM((TOKEN_TILE, d), _BF16),
            pltpu.VMEM((d // LANES, TOKEN_TILE, LANES), _F32),
            pltpu.VMEM((TOKEN_TILE, d), _BF16),
            pltpu.VMEM((TOKEN_TILE, d), _BF16),
            pltpu.VMEM((CONV_COLS // LANES, 2 * (CONV_HALO + TOKEN_TILE), LANES), _F32),
            pltpu.VMEM((CONV_HALO, 2 * M_WIDTH), _F32)],
        compiler_params=_params("arbitrary"),
        name="inproj",
    )(x2d, pre_g.reshape(1, d), w_in.T, *tables, conv_w, conv_b.reshape(1, -1))


def _attn_kernel(qkv_ref, bias_two_ref, bias_one_ref, o_ref, lse_ref, *scratch, dilation, n_blocks):
    lane = lax.broadcasted_iota(jnp.int32, (ATT_BLOCK, LANES), 1)

    def block(row0, key0, n_keys, bias_ref):
        ones = jnp.ones((n_keys, HEAD_DIM), _BF16)
        for r in range(dilation):
            scores = []
            for hh in range(GROUP_HEADS):
                qc = r * ATT_WIDTH + hh * HEAD_DIM
                q = qkv_ref[pl.ds(row0, ATT_BLOCK), qc:qc + HEAD_DIM]
                k = qkv_ref[pl.ds(key0, n_keys), qc + GROUP_WIDTH:qc + GROUP_WIDTH + HEAD_DIM]
                scores.append(lax.dot_general(q, k, _NT, preferred_element_type=_F32) + bias_ref[...])
            tops = [jnp.max(s, axis=1, keepdims=True) for s in scores]
            probs = [jnp.exp2(s - m).astype(_BF16) for s, m in zip(scores, tops)]
            lse_tile = jnp.zeros((ATT_BLOCK, LANES), _F32)
            out_row = row0 * dilation + r
            for hh in range(GROUP_HEADS):
                vc = r * ATT_WIDTH + 2 * GROUP_WIDTH + hh * HEAD_DIM
                v = qkv_ref[pl.ds(key0, n_keys), vc:vc + HEAD_DIM]
                acc = _dot(probs[hh], jnp.concatenate([v, ones], axis=1))
                denom = acc[:, HEAD_DIM:]
                o = acc[:, :HEAD_DIM] * (1.0 / denom)
                lse_tile = jnp.where(lane == hh, tops[hh] + jnp.log2(denom), lse_tile)
                if dilation == 1:
                    o_ref[pl.ds(row0, ATT_BLOCK), hh * HEAD_DIM:(hh + 1) * HEAD_DIM] = o.astype(_BF16)
                else:
                    scratch[0][hh, pl.ds(out_row, ATT_BLOCK, stride=dilation), :] = o
            lse_tile = lse_tile * math.log(2.0)
            if dilation == 1:
                lse_ref[pl.ds(row0, ATT_BLOCK), :] = lse_tile
            else:
                lse_ref[pl.ds(out_row, ATT_BLOCK, stride=dilation), :] = lse_tile

    block(0, 0, ATT_BLOCK, bias_one_ref)

    def body(i, carry):
        row0 = pl.multiple_of(i * ATT_BLOCK, ATT_BLOCK)
        key0 = pl.multiple_of((i - 1) * ATT_BLOCK, ATT_BLOCK)
        block(row0, key0, 2 * ATT_BLOCK, bias_two_ref)
        return carry

    if n_blocks > 1:
        lax.fori_loop(1, n_blocks, body, 0)
    if dilation > 1:
        for hh in range(GROUP_HEADS):
            o_ref[:, hh * HEAD_DIM:(hh + 1) * HEAD_DIM] = scratch[0][hh].astype(_BF16)


def _band_bias():
    qi = np.arange(ATT_BLOCK)[:, None]
    kj = np.arange(2 * ATT_BLOCK)[None, :]
    valid_two = np.where(kj < ATT_BLOCK, kj >= qi, kj - ATT_BLOCK <= qi)
    valid_one = np.arange(ATT_BLOCK)[None, :] <= qi
    to_bias = lambda valid: jnp.asarray(np.where(valid, 0.0, MASK_BIAS), _F32)
    return to_bias(valid_two), to_bias(valid_one)


def _attention_group(qkv, batch, seq, dilation):
    length = seq // dilation
    view = qkv.reshape(batch, length, dilation * ATT_WIDTH)
    bias_two, bias_one = _band_bias()
    scratch = [] if dilation == 1 else [pltpu.VMEM((GROUP_HEADS, seq, HEAD_DIM), _F32)]
    return pl.pallas_call(
        functools.partial(_attn_kernel, dilation=dilation, n_blocks=length // ATT_BLOCK),
        grid=(batch,),
        in_specs=[pl.BlockSpec((None, length, dilation * ATT_WIDTH), lambda b: (b, 0, 0)),
                  _resident(bias_two.shape), _resident(bias_one.shape)],
        out_specs=[pl.BlockSpec((None, seq, GROUP_WIDTH), lambda b: (b, 0, 0)),
                   pl.BlockSpec((None, seq, LANES), lambda b: (b, 0, 0))],
        out_shape=[jax.ShapeDtypeStruct((batch, seq, GROUP_WIDTH), _BF16),
                   jax.ShapeDtypeStruct((batch, seq, LANES), _F32)],
        scratch_shapes=scratch,
        compiler_params=_params("parallel"),
        name=f"attn_d{dilation}",
    )(view, bias_two, bias_one)


def _mlstm_kernel(qm_ref, km_ref, vm_ref, gates_ref, gate_b_ref, head_g_ref, o_ref, state_ref, m_ref):
    lc = M_CHUNK
    dh = M_HEAD_DIM
    wide = lambda a, n: jnp.concatenate([a] * n, axis=1)

    @pl.when(pl.program_id(1) == 0)
    def _():
        state_ref[...] = jnp.zeros(state_ref.shape, _F32)
        m_ref[...] = jnp.zeros(m_ref.shape, _F32)

    causal = (lax.broadcasted_iota(jnp.int32, (lc, lc), 1) <= lax.broadcasted_iota(jnp.int32, (lc, lc), 0))
    tri = causal.astype(_BF16)
    lane = lax.broadcasted_iota(jnp.int32, (lc, LANES), 1)
    ones = jnp.ones((lc, LANES), _BF16)

    for bb in range(M_BATCH):
        gt = gates_ref[bb] + gate_b_ref[...]
        lf = jnp.minimum(gt, 0.0) - jnp.log(1.0 + jnp.exp(-jnp.abs(gt)))
        hi = lf.astype(_BF16)
        rest = lf - hi.astype(_F32)
        mid = rest.astype(_BF16)
        low = (rest - mid.astype(_F32)).astype(_BF16)
        b_all = _dot(tri, hi) + _dot(tri, mid) + _dot(tri, low)
        ib = jnp.where(lane < M_HEADS, gt, b_all)
        ib_t = ib.T

        for hh in range(M_HEADS):
            hs = slice(hh * dh, (hh + 1) * dh)
            q = qm_ref[bb, :, hs]
            k = km_ref[bb, :, hs]
            v_aug = jnp.concatenate([vm_ref[bb, :, hs], ones], axis=1)
            i_rep = jnp.broadcast_to(ib[:, hh:hh + 1], (lc, LANES))
            b_rep = jnp.broadcast_to(ib[:, M_HEADS + hh:M_HEADS + hh + 1], (lc, LANES))
            u_row = ib_t[hh:hh + 1, :] - ib_t[M_HEADS + hh:M_HEADS + hh + 1, :]
            m_prev = m_ref[bb, hh][0:1, :]
            state = state_ref[bb, hh]

            e = jnp.where(causal, u_row, -jnp.inf)
            big_m = jnp.maximum(m_prev, jnp.broadcast_to(jnp.max(e, axis=1, keepdims=True), (lc, LANES)))
            w = jnp.exp(e - wide(big_m, lc // LANES)) * lax.dot_general(q, k, _NT, preferred_element_type=_F32)
            inter = jnp.exp(m_prev - big_m)
            acc = _dot(w.astype(_BF16), v_aug) + wide(inter, dh // LANES + 1) * _dot(q, state.astype(_BF16))
            den = acc[:, dh:]
            inv = 1.0 / jnp.maximum(jnp.abs(den), jnp.exp(-(b_rep + big_m)))
            h_out = acc[:, :dh] * wide(inv, dh // LANES)

            b_last = b_rep[lc - 1:lc, :]
            g = b_last - b_rep + i_rep
            m_new = jnp.maximum(b_last + m_prev, jnp.max(g, axis=0, keepdims=True))
            a = jnp.exp(g - m_new)
            decay = jnp.exp(b_last + m_prev - m_new)
            av = (wide(a, dh // LANES + 1) * v_aug.astype(_F32)).astype(_BF16)
            state_ref[bb, hh] = (wide(decay, dh // LANES + 1) * state
                                 + lax.dot_general(k, av, _TN, preferred_element_type=_F32))
            m_ref[bb, hh] = jnp.broadcast_to(m_new, m_ref.shape[2:])

            o_ref[bb, :, hs] = (_rms(h_out) * head_g_ref[:, hs]).astype(_BF16)


def _mlstm(qm, km, vm, gates, batch, seq, i_bias, f_bias, head_g):
    n_chunks = seq // M_CHUNK
    gate_b = jnp.pad(jnp.concatenate([i_bias, f_bias]), (0, LANES - 2 * M_HEADS)).reshape(1, LANES)
    chunk = lambda width: pl.BlockSpec((M_BATCH, M_CHUNK, width), lambda b, c: (b, c, 0))
    out = pl.pallas_call(
        _mlstm_kernel,
        grid=(batch // M_BATCH, n_chunks),
        in_specs=[chunk(M_WIDTH), chunk(M_WIDTH), chunk(M_WIDTH), chunk(LANES),
                  _resident((1, LANES)), _resident((1, M_WIDTH))],
        out_specs=chunk(M_WIDTH),
        out_shape=jax.ShapeDtypeStruct((batch, seq, M_WIDTH), _BF16),
        scratch_shapes=[pltpu.VMEM((M_BATCH, M_HEADS, M_HEAD_DIM, M_HEAD_DIM + LANES), _F32),
                        pltpu.VMEM((M_BATCH, M_HEADS, 8, LANES), _F32)],
        compiler_params=_params("parallel", "arbitrary"),
        name="mlstm",
    )(qm.reshape(batch, seq, M_WIDTH), km.reshape(batch, seq, M_WIDTH), vm.reshape(batch, seq, M_WIDTH),
      gates.reshape(batch, seq, LANES), gate_b, head_g.reshape(1, -1))
    return out.reshape(batch * seq, M_WIDTH)


def _merge_kernel(x_ref, o0_ref, o1_ref, o2_ref, l0_ref, l1_ref, l2_ref, hm_ref, pre_g_ref, post_g_ref,
                  w_in_hbm, wa_hbm, wm_hbm, wo_hbm, out_ref,
                  wg_ref, wa_ref, wm_ref, wo_ref, stage_ref, sem_ref):
    def gate(h, col):
        return _sigmoid(lax.dot_general(h, wg_ref[col:col + D_MODEL, :], _NT, preferred_element_type=_F32))

    @pl.when(pl.program_id(0) == 0)
    def _():
        o_col0 = N_GROUPS * ATT_WIDTH + 3 * M_WIDTH
        branch_col0 = o_col0 + M_WIDTH + 2 * M_HEADS
        _cast_weight_in(w_in_hbm, wg_ref, stage_ref, sem_ref,
                        _row_jobs([(o_col0, 0, M_WIDTH), (branch_col0, M_WIDTH, 2 * D_MODEL)], STAGE_SHAPE[1]))
        _cast_weight_in(wa_hbm, wa_ref, stage_ref, sem_ref)
        _cast_weight_in(wm_hbm, wm_ref, stage_ref, sem_ref)
        _cast_weight_in(wo_hbm, wo_ref, stage_ref, sem_ref)

    x = x_ref[...]
    h = (_rms(x) * pre_g_ref[...]).astype(_BF16)
    lses = [r[...] for r in (l0_ref, l1_ref, l2_ref)]
    top = jnp.maximum(jnp.maximum(lses[0], lses[1]), lses[2])
    es = [jnp.exp(l - top) for l in lses]
    inv = 1.0 / (es[0] + es[1] + es[2])
    alphas = [e * inv for e in es]
    heads = []
    for hh in range(GROUP_HEADS):
        acc = None
        for alpha, o_ref in zip(alphas, (o0_ref, o1_ref, o2_ref)):
            term = alpha[:, hh:hh + 1] * o_ref[:, hh * HEAD_DIM:(hh + 1) * HEAD_DIM].astype(_F32)
            acc = term if acc is None else acc + term
        heads.append(acc.astype(_BF16))
    att = jnp.concatenate(heads, axis=1)
    a = gate(h, M_WIDTH) * _dot(att, wa_ref[...])
    ml = (gate(h, 0) * hm_ref[...].astype(_F32)).astype(_BF16)
    m = gate(h, M_WIDTH + D_MODEL) * _dot(ml, wm_ref[...])
    y = _dot((a + m).astype(_BF16), wo_ref[...])
    out_ref[...] = x + _rms(y) * post_g_ref[...]


def _merge(x2d, outs, lses, hm, pre_g, w_in, w_att, w_ml, w_out, post_g):
    t, d = x2d.shape
    row = lambda width: pl.BlockSpec((TOKEN_TILE, width), lambda i: (i, 0))
    return pl.pallas_call(
        _merge_kernel,
        grid=(t // TOKEN_TILE,),
        in_specs=[row(d)] + [row(GROUP_WIDTH)] * 3 + [row(LANES)] * 3 + [row(M_WIDTH), _resident((1, d)),
                  _resident((1, d))] + [_HBM] * 4,
        out_specs=row(d),
        out_shape=jax.ShapeDtypeStruct((t, d), _F32),
        scratch_shapes=_weight_scratch((M_WIDTH + 2 * D_MODEL, d), w_att.shape, w_ml.shape, w_out.shape),
        compiler_params=_params("arbitrary"),
        name="merge",
    )(x2d, *outs, *lses, hm, pre_g.reshape(1, d), post_g.reshape(1, d), w_in.T, w_att, w_ml, w_out)


def kernel(x, ffn1_pre_g, ffn1_w_gate, ffn1_w_up, ffn1_w_down, ffn1_post_g, mix_pre_g, w_in, conv_w, conv_b, mlstm_i_bias, mlstm_f_bias, mlstm_head_g, w_att_branch, w_mlstm_branch, w_out, mix_post_g, ffn2_pre_g, ffn2_w_gate, ffn2_w_up, ffn2_w_down, ffn2_post_g):
    batch, seq, d = x.shape
    t = batch * seq
    xt = x.reshape(t, d)
    for l in range(ffn1_pre_g.shape[0]):
        xt = _ffn(xt, ffn1_pre_g[l], ffn1_w_gate[l], ffn1_w_up[l], ffn1_w_down[l], ffn1_post_g[l])
        qkv0, qkv1, qkv2, qm, km, vm, gates = _inproj(xt, seq, mix_pre_g[l], w_in[l], conv_w[l], conv_b[l])
        outs, lses = zip(*(_attention_group(qkv, batch, seq, dil)
                           for qkv, dil in zip((qkv0, qkv1, qkv2), GROUP_DILATIONS)))
        outs = [o.reshape(t, GROUP_WIDTH) for o in outs]
        lses = [s.reshape(t, LANES) for s in lses]
        hm = _mlstm(qm, km, vm, gates, batch, seq, mlstm_i_bias[l], mlstm_f_bias[l], mlstm_head_g[l])
        xt = _merge(xt, outs, lses, hm, mix_pre_g[l], w_in[l], w_att_branch[l], w_mlstm_branch[l], w_out[l],
                    mix_post_g[l])
        xt = _ffn(xt, ffn2_pre_g[l], ffn2_w_gate[l], ffn2_w_up[l], ffn2_w_down[l], ffn2_post_g[l])
    return xt.reshape(batch, seq, d)
```

```python
import functools
import math

import numpy as np
import jax
import jax.numpy as jnp
from jax import lax
from jax.experimental import pallas as pl
from jax.experimental.pallas import tpu as pltpu

D_MODEL = 1024
N_GROUPS = 3
GROUP_DILATIONS = (1, 4, 16)
ATT_SPAN = 128
GROUP_HEADS = 4
HEAD_DIM = 128
GROUP_WIDTH = GROUP_HEADS * HEAD_DIM
ATT_WIDTH = N_GROUPS * GROUP_WIDTH
ATT_BLOCK = 128
ROPE_THETA = 500000.0
ROPE_DIMS = HEAD_DIM // 4
ROPE_HALF = ROPE_DIMS // 2
M_HEADS = 4
M_WIDTH = D_MODEL
M_HEAD_DIM = M_WIDTH // M_HEADS
M_CHUNK = 256
ATT_BLOCKS_PER_ITER = 4
M_BATCH = 2
CONV_WIDTH = 4
CONV_HALO = 8
CONV_COLS = 256
RMS_EPS = 1e-6
LANES = 128
MASK_BIAS = -1e30
VMEM_LIMIT_BYTES = 58 * 1024 * 1024
TOKEN_TILE = 512
STAGE_SHAPE = (4, 256, 1024)

_BF16 = jnp.bfloat16
_F32 = jnp.float32
_NT = (((1,), (1,)), ((), ()))
_TN = (((0,), (0,)), ((), ()))


def _rms(x):
    return x * lax.rsqrt(jnp.mean(x * x, axis=-1, keepdims=True) + RMS_EPS)


def _sigmoid(x):
    return 1.0 / (1.0 + jnp.exp(-x))


def _dot(a, b):
    return jnp.dot(a, b, preferred_element_type=_F32)


def _resident(shape):
    return pl.BlockSpec(shape, lambda *_: (0,) * len(shape), pipeline_mode=pl.Buffered(1))


def _params(*semantics):
    return pltpu.CompilerParams(dimension_semantics=semantics, vmem_limit_bytes=VMEM_LIMIT_BYTES)


def _cast_weight_in(src_hbm, dst_ref, stage_ref, sem_ref, jobs=None):
    slots, stage_rows, stage_cols = stage_ref.shape
    if jobs is None:
        rows, cols = dst_ref.shape
        jobs = [(r, c, r, c, min(stage_rows, rows - r), min(stage_cols, cols - c))
                for r in range(0, rows, stage_rows) for c in range(0, cols, stage_cols)]

    def copy(k):
        src_row, src_col, _, _, nr, nc = jobs[k]
        return pltpu.make_async_copy(src_hbm.at[pl.ds(src_row, nr), pl.ds(src_col, nc)],
                                     stage_ref.at[k % slots, pl.ds(0, nr), pl.ds(0, nc)], sem_ref.at[k % slots])

    for k in range(min(slots - 1, len(jobs))):
        copy(k).start()
    for k, (_, _, dst_row, dst_col, nr, nc) in enumerate(jobs):
        if k + slots - 1 < len(jobs):
            copy(k + slots - 1).start()
        copy(k).wait()
        dst_ref[dst_row:dst_row + nr, dst_col:dst_col + nc] = stage_ref[k % slots, 0:nr, 0:nc].astype(_BF16)


def _row_jobs(windows, stage_rows):
    return [(src + r, 0, dst + r, 0, min(stage_rows, n - r), STAGE_SHAPE[2])
            for src, dst, n in windows for r in range(0, n, stage_rows)]


def _ffn_kernel(x_ref, pre_g_ref, post_g_ref, wg_hbm, wu_hbm, wd_hbm, o_ref,
                wg_ref, wu_ref, wd_ref, stage_ref, sem_ref):
    @pl.when(pl.program_id(0) == 0)
    def _():
        _cast_weight_in(wg_hbm, wg_ref, stage_ref, sem_ref)
        _cast_weight_in(wu_hbm, wu_ref, stage_ref, sem_ref)
        _cast_weight_in(wd_hbm, wd_ref, stage_ref, sem_ref)

    x = x_ref[...]
    h = (_rms(x) * pre_g_ref[...]).astype(_BF16)
    g = _dot(h, wg_ref[...])
    u = _dot(h, wu_ref[...])
    a = (g * _sigmoid(g) * u).astype(_BF16)
    f = _dot(a, wd_ref[...])
    o_ref[...] = x + 0.5 * (_rms(f) * post_g_ref[...])


def _weight_scratch(*shapes):
    return ([pltpu.VMEM(shape, _BF16) for shape in shapes]
            + [pltpu.VMEM(STAGE_SHAPE, _F32), pltpu.SemaphoreType.DMA((STAGE_SHAPE[0],))])


_HBM = pl.BlockSpec(memory_space=pl.ANY)


def _ffn(x2d, pre_g, w_gate, w_up, w_down, post_g):
    t, d = x2d.shape
    row = pl.BlockSpec((TOKEN_TILE, d), lambda i: (i, 0))
    return pl.pallas_call(
        _ffn_kernel,
        grid=(t // TOKEN_TILE,),
        in_specs=[row, _resident((1, d)), _resident((1, d)), _HBM, _HBM, _HBM],
        out_specs=row,
        out_shape=jax.ShapeDtypeStruct((t, d), _F32),
        scratch_shapes=_weight_scratch(w_gate.shape, w_up.shape, w_down.shape),
        compiler_params=_params("arbitrary"),
        name="ffn",
    )(x2d, pre_g.reshape(1, d), post_g.reshape(1, d), w_gate, w_up, w_down)


def _inproj_kernel(x_ref, g_ref, w_hbm, tab0_ref, tab1_ref, tab2_ref, conv_w_ref, conv_b_ref,
                   qkv0_ref, qkv1_ref, qkv2_ref, qm_ref, km_ref, vm_ref, gates_ref,
                   w_ref, stage_ref, sem_ref, hn_ref, hs_ref, hp1_ref, hp2_ref, xs_ref, halo_ref, *, tiles_per_seq):
    tm = TOKEN_TILE
    n_slabs = D_MODEL // LANES
    conv_col0 = N_GROUPS * ATT_WIDTH
    vm_col0 = conv_col0 + 2 * M_WIDTH
    gate_col0 = vm_col0 + M_WIDTH

    def project(h_ref, col, width):
        return lax.dot_general(h_ref[...], w_ref[col:col + width, :], _NT, preferred_element_type=_F32)

    @pl.when(pl.program_id(0) == 0)
    def _():
        w_ref[gate_col0 + 2 * M_HEADS:gate_col0 + LANES, :] = jnp.zeros((LANES - 2 * M_HEADS, D_MODEL), _BF16)
        _cast_weight_in(w_hbm, w_ref, stage_ref, sem_ref, _row_jobs(_inproj_windows(), STAGE_SHAPE[1]))

    @pl.when(pl.program_id(0) % tiles_per_seq == 0)
    def _():
        halo_ref[...] = jnp.zeros(halo_ref.shape, _F32)

    hf = _rms(x_ref[...]) * g_ref[...]
    hn_ref[...] = hf.astype(_BF16)
    for c in range(n_slabs):
        hs_ref[c] = hf[:, c * LANES:(c + 1) * LANES]

    lane = lax.broadcasted_iota(jnp.int32, (tm, HEAD_DIM), 1)
    low_half = lane < ROPE_HALF
    q_scale = HEAD_DIM ** -0.5 * math.log2(math.e)

    def regroup(hp_ref, d):
        rows = tm // d
        for r in range(d):
            piece = jnp.concatenate([hs_ref[c, pl.ds(r, rows, stride=d), :] for c in range(n_slabs)], axis=1)
            hp_ref[r * rows:(r + 1) * rows, :] = piece.astype(_BF16)

    def attention_part(g, part, h_ref):
        return project(h_ref, (g * 3 + part) * GROUP_WIDTH, GROUP_WIDTH)

    def attention_store(p, part, qkv_ref, tab_ref, d):
        rows = tm // d
        for hh in range(GROUP_HEADS):
            x = p[:, hh * HEAD_DIM:(hh + 1) * HEAD_DIM]
            if part < 2:
                partner = jnp.where(low_half, pltpu.roll(x, HEAD_DIM - ROPE_HALF, 1), pltpu.roll(x, ROPE_HALF, 1))
                x = x * tab_ref[0] + partner * tab_ref[1]
            if part == 0:
                x = x * q_scale
            x = x.astype(_BF16)
            for r in range(d):
                lo = r * ATT_WIDTH + part * GROUP_WIDTH + hh * HEAD_DIM
                qkv_ref[:, lo:lo + HEAD_DIM] = x[r * rows:(r + 1) * rows, :]

    def conv_dot(cc):
        return project(hn_ref, conv_col0 + cc * CONV_COLS, CONV_COLS)

    def conv_store(p, cc):
        for c in range(CONV_COLS // LANES):
            lanes = slice(cc * CONV_COLS + c * LANES, cc * CONV_COLS + (c + 1) * LANES)
            pc = p[:, c * LANES:(c + 1) * LANES]
            xs_ref[c, pl.ds(0, CONV_HALO, stride=2), :] = halo_ref[:, lanes]
            xs_ref[c, pl.ds(2 * CONV_HALO, tm, stride=2), :] = pc
            halo_ref[:, lanes] = pc[tm - CONV_HALO:tm, :]
            y = conv_b_ref[:, lanes] + conv_w_ref[CONV_WIDTH - 1:CONV_WIDTH, lanes] * pc
            for j in range(CONV_WIDTH - 1):
                shift = CONV_WIDTH - 1 - j
                y = y + conv_w_ref[j:j + 1, lanes] * xs_ref[c, pl.ds(2 * (CONV_HALO - shift), tm, stride=2), :]
            qk = y + y * jnp.tanh(y)
            if lanes.start < M_WIDTH:
                qm_ref[:, lanes] = qk.astype(_BF16)
            else:
                km_ref[:, lanes.start - M_WIDTH:lanes.stop - M_WIDTH] = (qk * M_HEAD_DIM ** -0.5).astype(_BF16)

    def vm_store(p):
        vm_ref[...] = p.astype(_BF16)

    def gates_store(p):
        gates_ref[...] = p

    hps = (hn_ref, hp1_ref, hp2_ref)
    qkvs = (qkv0_ref, qkv1_ref, qkv2_ref)
    tabs = (tab0_ref, tab1_ref, tab2_ref)
    attention = [(functools.partial(attention_part, g, part, hps[g]),
                  functools.partial(attention_store, part=part, qkv_ref=qkvs[g], tab_ref=tabs[g], d=GROUP_DILATIONS[g]))
                 for g in range(N_GROUPS) for part in range(3)]
    convs = [(functools.partial(conv_dot, cc), functools.partial(conv_store, cc=cc))
             for cc in range(2 * M_WIDTH // CONV_COLS)]
    sections = [(lambda: project(hn_ref, vm_col0, M_WIDTH), vm_store),
                (lambda: project(hn_ref, gate_col0, LANES),
                 lambda p: (gates_store(p), regroup(hp1_ref, GROUP_DILATIONS[1]))),
                attention[0], convs[0], attention[1], convs[1], attention[2],
                (convs[2][0], lambda p: (conv_store(p, 2), regroup(hp2_ref, GROUP_DILATIONS[2]))),
                attention[3], convs[3], attention[4], convs[4], attention[5], convs[5],
                attention[6], convs[6], attention[7], convs[7], attention[8]]
    pending = None
    for matmul, epilogue in sections:
        p = matmul()
        if pending is not None:
            pending()
        pending = functools.partial(epilogue, p)
    pending()


def _inproj_windows():
    a, m = ATT_WIDTH, M_WIDTH
    wins = []
    for g in range(N_GROUPS):
        for part in range(3):
            wins.append((part * a + g * GROUP_WIDTH, (g * 3 + part) * GROUP_WIDTH, GROUP_WIDTH))
    base = 3 * a
    wins.append((base, base, 3 * m))
    wins.append((base + 4 * m, base + 3 * m, 2 * M_HEADS))
    return wins


def _rope_tables(seq, dilation):
    inv_freq = jnp.power(ROPE_THETA, -(jnp.arange(ROPE_HALF, dtype=_F32) * 2.0 / ROPE_DIMS))
    ang = jnp.arange(seq, dtype=jnp.int32).astype(_F32)[:, None] * inv_freq[None, :]
    cos, sin = jnp.cos(ang), jnp.sin(ang)
    rest = HEAD_DIM - ROPE_DIMS
    tabs = jnp.stack([jnp.concatenate([cos, cos, jnp.ones((seq, rest), _F32)], axis=-1),
                      jnp.concatenate([-sin, sin, jnp.zeros((seq, rest), _F32)], axis=-1)])
    tabs = tabs.reshape(2, seq // TOKEN_TILE, TOKEN_TILE // dilation, dilation, HEAD_DIM)
    return tabs.transpose(0, 1, 3, 2, 4).reshape(2, seq, HEAD_DIM)


def _inproj(x2d, seq, pre_g, w_in, conv_w, conv_b):
    t, d = x2d.shape
    tiles_per_seq = seq // TOKEN_TILE
    w_cols = N_GROUPS * ATT_WIDTH + 3 * M_WIDTH + LANES
    tables = [_rope_tables(seq, dil) for dil in GROUP_DILATIONS]
    row = lambda width: pl.BlockSpec((TOKEN_TILE, width), lambda i: (i, 0))
    cls = lambda dil: pl.BlockSpec((TOKEN_TILE // dil, dil * ATT_WIDTH), lambda i: (i, 0))
    tab = pl.BlockSpec((2, TOKEN_TILE, HEAD_DIM), lambda i: (0, i % tiles_per_seq, 0))
    out_shape = [jax.ShapeDtypeStruct((t // dil, dil * ATT_WIDTH), _BF16) for dil in GROUP_DILATIONS]
    out_shape += [jax.ShapeDtypeStruct((t, M_WIDTH), _BF16)] * 3
    out_shape.append(jax.ShapeDtypeStruct((t, LANES), _F32))
    return pl.pallas_call(
        functools.partial(_inproj_kernel, tiles_per_seq=tiles_per_seq),
        grid=(t // TOKEN_TILE,),
        in_specs=[row(d), _resident((1, d)), _HBM, tab, tab, tab,
                  _resident((CONV_WIDTH, 2 * M_WIDTH)), _resident((1, 2 * M_WIDTH))],
        out_specs=[cls(dil) for dil in GROUP_DILATIONS] + [row(M_WIDTH)] * 3 + [row(LANES)],
        out_shape=out_shape,
        scratch_shapes=_weight_scratch((w_cols, d)) + [
            pltpu.VMEM((TOKEN_TILE, d), _BF16),
            pltpu.VMEM((d // LANES, TOKEN_TILE, LANES), _F32),
            pltpu.VMEM((TOKEN_TILE, d), _BF16),
            pltpu.VMEM((TOKEN_TILE, d), _BF16),
            pltpu.VMEM((CONV_COLS // LANES, 2 * (CONV_HALO + TOKEN_TILE), LANES), _F32),
            pltpu.VMEM((CONV_HALO, 2 * M_WIDTH), _F32)],
        compiler_params=_params("arbitrary"),
        name="inproj",
    )(x2d, pre_g.reshape(1, d), w_in.T, *tables, 0.5 * conv_w, 0.5 * conv_b.reshape(1, -1))


def _attn_kernel(qkv_ref, bias_two_ref, bias_one_ref, o_ref, lse_ref, *scratch, dilation, n_blocks):
    lane = lax.broadcasted_iota(jnp.int32, (ATT_BLOCK, LANES), 1)

    def block(row0, key0, n_keys, bias_ref):
        ones = jnp.ones((n_keys, HEAD_DIM), _BF16)
        for r in range(dilation):
            scores = []
            for hh in range(GROUP_HEADS):
                qc = r * ATT_WIDTH + hh * HEAD_DIM
                q = qkv_ref[pl.ds(row0, ATT_BLOCK), qc:qc + HEAD_DIM]
                k = qkv_ref[pl.ds(key0, n_keys), qc + GROUP_WIDTH:qc + GROUP_WIDTH + HEAD_DIM]
                scores.append(lax.dot_general(q, k, _NT, preferred_element_type=_F32) + bias_ref[...])
            tops = [jnp.max(s, axis=1, keepdims=True) for s in scores]
            probs = [jnp.exp2(s - m).astype(_BF16) for s, m in zip(scores, tops)]
            lse_tile = jnp.zeros((ATT_BLOCK, LANES), _F32)
            out_row = row0 * dilation + r
            for hh in range(GROUP_HEADS):
                vc = r * ATT_WIDTH + 2 * GROUP_WIDTH + hh * HEAD_DIM
                v = qkv_ref[pl.ds(key0, n_keys), vc:vc + HEAD_DIM]
                acc = _dot(probs[hh], jnp.concatenate([v, ones], axis=1))
                denom = acc[:, HEAD_DIM:]
                o = acc[:, :HEAD_DIM] * (1.0 / denom)
                lse_tile = jnp.where(lane == hh, tops[hh] + jnp.log2(denom), lse_tile)
                if dilation == 1:
                    o_ref[pl.ds(row0, ATT_BLOCK), hh * HEAD_DIM:(hh + 1) * HEAD_DIM] = o.astype(_BF16)
                else:
                    scratch[0][hh, pl.ds(out_row, ATT_BLOCK, stride=dilation), :] = o
            lse_tile = lse_tile * math.log(2.0)
            if dilation == 1:
                lse_ref[pl.ds(row0, ATT_BLOCK), :] = lse_tile
            else:
                lse_ref[pl.ds(out_row, ATT_BLOCK, stride=dilation), :] = lse_tile

    def two_key_block(i):
        row0 = i * ATT_BLOCK
        if not isinstance(i, int):
            row0 = pl.multiple_of(row0, ATT_BLOCK)
        block(row0, row0 - ATT_BLOCK, 2 * ATT_BLOCK, bias_two_ref)

    block(0, 0, ATT_BLOCK, bias_one_ref)
    per_iter = ATT_BLOCKS_PER_ITER if dilation == 1 else 1
    for i in range(1, min(per_iter, n_blocks)):
        two_key_block(i)

    def body(j, carry):
        for i in range(per_iter):
            two_key_block(j * per_iter + i)
        return carry

    if n_blocks > per_iter:
        lax.fori_loop(1, n_blocks // per_iter, body, 0)
    if dilation > 1:
        for hh in range(GROUP_HEADS):
            o_ref[:, hh * HEAD_DIM:(hh + 1) * HEAD_DIM] = scratch[0][hh].astype(_BF16)


def _band_bias():
    qi = np.arange(ATT_BLOCK)[:, None]
    kj = np.arange(2 * ATT_BLOCK)[None, :]
    valid_two = np.where(kj < ATT_BLOCK, kj >= qi, kj - ATT_BLOCK <= qi)
    valid_one = np.arange(ATT_BLOCK)[None, :] <= qi
    to_bias = lambda valid: jnp.asarray(np.where(valid, 0.0, MASK_BIAS), _F32)
    return to_bias(valid_two), to_bias(valid_one)


def _attention_group(qkv, batch, seq, dilation):
    length = seq // dilation
    view = qkv.reshape(batch, length, dilation * ATT_WIDTH)
    bias_two, bias_one = _band_bias()
    scratch = [] if dilation == 1 else [pltpu.VMEM((GROUP_HEADS, seq, HEAD_DIM), _F32)]
    return pl.pallas_call(
        functools.partial(_attn_kernel, dilation=dilation, n_blocks=length // ATT_BLOCK),
        grid=(batch,),
        in_specs=[pl.BlockSpec((None, length, dilation * ATT_WIDTH), lambda b: (b, 0, 0)),
                  _resident(bias_two.shape), _resident(bias_one.shape)],
        out_specs=[pl.BlockSpec((None, seq, GROUP_WIDTH), lambda b: (b, 0, 0)),
                   pl.BlockSpec((None, seq, LANES), lambda b: (b, 0, 0))],
        out_shape=[jax.ShapeDtypeStruct((batch, seq, GROUP_WIDTH), _BF16),
                   jax.ShapeDtypeStruct((batch, seq, LANES), _F32)],
        scratch_shapes=scratch,
        compiler_params=_params("parallel"),
        name=f"attn_d{dilation}",
    )(view, bias_two, bias_one)


def _mlstm_kernel(qm_ref, km_ref, vm_ref, gates_ref, gate_b_ref, head_g_ref, o_ref, state_ref, m_ref):
    lc = M_CHUNK
    dh = M_HEAD_DIM
    wide = lambda a, n: jnp.concatenate([a] * n, axis=1)

    @pl.when(pl.program_id(1) == 0)
    def _():
        state_ref[...] = jnp.zeros(state_ref.shape, _F32)
        m_ref[...] = jnp.zeros(m_ref.shape, _F32)

    causal = (lax.broadcasted_iota(jnp.int32, (lc, lc), 1) <= lax.broadcasted_iota(jnp.int32, (lc, lc), 0))
    tri = causal.astype(_BF16)
    lane = lax.broadcasted_iota(jnp.int32, (lc, LANES), 1)
    ones = jnp.ones((lc, LANES), _BF16)

    for bb in range(M_BATCH):
        gt = gates_ref[bb] + gate_b_ref[...]
        lf = jnp.minimum(gt, 0.0) - jnp.log(1.0 + jnp.exp(-jnp.abs(gt)))
        hi = lf.astype(_BF16)
        rest = lf - hi.astype(_F32)
        mid = rest.astype(_BF16)
        low = (rest - mid.astype(_F32)).astype(_BF16)
        b_all = _dot(tri, hi) + _dot(tri, mid) + _dot(tri, low)
        ib = jnp.where(lane < M_HEADS, gt, b_all)
        ib_t = ib.T

        for hh in range(M_HEADS):
            hs = slice(hh * dh, (hh + 1) * dh)
            q = qm_ref[bb, :, hs]
            k = km_ref[bb, :, hs]
            v_aug = jnp.concatenate([vm_ref[bb, :, hs], ones], axis=1)
            i_rep = jnp.broadcast_to(ib[:, hh:hh + 1], (lc, LANES))
            b_rep = jnp.broadcast_to(ib[:, M_HEADS + hh:M_HEADS + hh + 1], (lc, LANES))
            u_row = ib_t[hh:hh + 1, :] - ib_t[M_HEADS + hh:M_HEADS + hh + 1, :]
            m_prev = m_ref[bb, hh][0:1, :]
            state = state_ref[bb, hh]

            e = jnp.where(causal, u_row, -jnp.inf)
            big_m = jnp.maximum(m_prev, jnp.broadcast_to(jnp.max(e, axis=1, keepdims=True), (lc, LANES)))
            w = jnp.exp(e - wide(big_m, lc // LANES)) * lax.dot_general(q, k, _NT, preferred_element_type=_F32)
            inter = jnp.exp(m_prev - big_m)
            acc = _dot(w.astype(_BF16), v_aug) + wide(inter, dh // LANES + 1) * _dot(q, state.astype(_BF16))
            den = acc[:, dh:]
            inv = 1.0 / jnp.maximum(jnp.abs(den), jnp.exp(-(b_rep + big_m)))
            h_out = acc[:, :dh] * wide(inv, dh // LANES)

            b_last = b_rep[lc - 1:lc, :]
            g = b_last - b_rep + i_rep
            m_new = jnp.maximum(b_last + m_prev, jnp.max(g, axis=0, keepdims=True))
            a = jnp.exp(g - m_new)
            decay = jnp.exp(b_last + m_prev - m_new)
            av = (wide(a, dh // LANES + 1) * v_aug.astype(_F32)).astype(_BF16)
            state_ref[bb, hh] = (wide(decay, dh // LANES + 1) * state
                                 + lax.dot_general(k, av, _TN, preferred_element_type=_F32))
            m_ref[bb, hh] = jnp.broadcast_to(m_new, m_ref.shape[2:])

            o_ref[bb, :, hs] = (_rms(h_out) * head_g_ref[:, hs]).astype(_BF16)


def _mlstm(qm, km, vm, gates, batch, seq, i_bias, f_bias, head_g):
    n_chunks = seq // M_CHUNK
    gate_b = jnp.pad(jnp.concatenate([i_bias, f_bias]), (0, LANES - 2 * M_HEADS)).reshape(1, LANES)
    chunk = lambda width: pl.BlockSpec((M_BATCH, M_CHUNK, width), lambda b, c: (b, c, 0))
    out = pl.pallas_call(
        _mlstm_kernel,
        grid=(batch // M_BATCH, n_chunks),
        in_specs=[chunk(M_WIDTH), chunk(M_WIDTH), chunk(M_WIDTH), chunk(LANES),
                  _resident((1, LANES)), _resident((1, M_WIDTH))],
        out_specs=chunk(M_WIDTH),
        out_shape=jax.ShapeDtypeStruct((batch, seq, M_WIDTH), _BF16),
        scratch_shapes=[pltpu.VMEM((M_BATCH, M_HEADS, M_HEAD_DIM, M_HEAD_DIM + LANES), _F32),
                        pltpu.VMEM((M_BATCH, M_HEADS, 8, LANES), _F32)],
        compiler_params=_params("parallel", "arbitrary"),
        name="mlstm",
    )(qm.reshape(batch, seq, M_WIDTH), km.reshape(batch, seq, M_WIDTH), vm.reshape(batch, seq, M_WIDTH),
      gates.reshape(batch, seq, LANES), gate_b, head_g.reshape(1, -1))
    return out.reshape(batch * seq, M_WIDTH)


def _merge_kernel(x_ref, o0_ref, o1_ref, o2_ref, l0_ref, l1_ref, l2_ref, hm_ref, pre_g_ref, post_g_ref,
                  w_in_hbm, wa_hbm, wm_hbm, wo_hbm, out_ref,
                  wg_ref, wa_ref, wm_ref, wo_ref, stage_ref, sem_ref):
    def gate(h, col):
        return _sigmoid(lax.dot_general(h, wg_ref[col:col + D_MODEL, :], _NT, preferred_element_type=_F32))

    @pl.when(pl.program_id(0) == 0)
    def _():
        o_col0 = N_GROUPS * ATT_WIDTH + 3 * M_WIDTH
        branch_col0 = o_col0 + M_WIDTH + 2 * M_HEADS
        _cast_weight_in(w_in_hbm, wg_ref, stage_ref, sem_ref,
                        _row_jobs([(o_col0, 0, M_WIDTH), (branch_col0, M_WIDTH, 2 * D_MODEL)], STAGE_SHAPE[1]))
        _cast_weight_in(wa_hbm, wa_ref, stage_ref, sem_ref)
        _cast_weight_in(wm_hbm, wm_ref, stage_ref, sem_ref)
        _cast_weight_in(wo_hbm, wo_ref, stage_ref, sem_ref)

    x = x_ref[...]
    h = (_rms(x) * pre_g_ref[...]).astype(_BF16)
    lses = [r[...] for r in (l0_ref, l1_ref, l2_ref)]
    top = jnp.maximum(jnp.maximum(lses[0], lses[1]), lses[2])
    es = [jnp.exp(l - top) for l in lses]
    inv = 1.0 / (es[0] + es[1] + es[2])
    alphas = [e * inv for e in es]
    heads = []
    for hh in range(GROUP_HEADS):
        acc = None
        for alpha, o_ref in zip(alphas, (o0_ref, o1_ref, o2_ref)):
            term = alpha[:, hh:hh + 1] * o_ref[:, hh * HEAD_DIM:(hh + 1) * HEAD_DIM].astype(_F32)
            acc = term if acc is None else acc + term
        heads.append(acc.astype(_BF16))
    att = jnp.concatenate(heads, axis=1)
    a = gate(h, M_WIDTH) * _dot(att, wa_ref[...])
    ml = (gate(h, 0) * hm_ref[...].astype(_F32)).astype(_BF16)
    m = gate(h, M_WIDTH + D_MODEL) * _dot(ml, wm_ref[...])
    y = _dot((a + m).astype(_BF16), wo_ref[...])
    out_ref[...] = x + _rms(y) * post_g_ref[...]


def _merge(x2d, outs, lses, hm, pre_g, w_in, w_att, w_ml, w_out, post_g):
    t, d = x2d.shape
    row = lambda width: pl.BlockSpec((TOKEN_TILE, width), lambda i: (i, 0))
    return pl.pallas_call(
        _merge_kernel,
        grid=(t // TOKEN_TILE,),
        in_specs=[row(d)] + [row(GROUP_WIDTH)] * 3 + [row(LANES)] * 3 + [row(M_WIDTH), _resident((1, d)),
                  _resident((1, d))] + [_HBM] * 4,
        out_specs=row(d),
        out_shape=jax.ShapeDtypeStruct((t, d), _F32),
        scratch_shapes=_weight_scratch((M_WIDTH + 2 * D_MODEL, d), w_att.shape, w_ml.shape, w_out.shape),
        compiler_params=_params("arbitrary"),
        name="merge",
    )(x2d, *outs, *lses, hm, pre_g.reshape(1, d), post_g.reshape(1, d), w_in.T, w_att, w_ml, w_out)


def kernel(x, ffn1_pre_g, ffn1_w_gate, ffn1_w_up, ffn1_w_down, ffn1_post_g, mix_pre_g, w_in, conv_w, conv_b, mlstm_i_bias, mlstm_f_bias, mlstm_head_g, w_att_branch, w_mlstm_branch, w_out, mix_post_g, ffn2_pre_g, ffn2_w_gate, ffn2_w_up, ffn2_w_down, ffn2_post_g):
    batch, seq, d = x.shape
    t = batch * seq
    xt = x.reshape(t, d)
    for l in range(ffn1_pre_g.shape[0]):
        xt = _ffn(xt, ffn1_pre_g[l], ffn1_w_gate[l], ffn1_w_up[l], ffn1_w_down[l], ffn1_post_g[l])
        qkv0, qkv1, qkv2, qm, km, vm, gates = _inproj(xt, seq, mix_pre_g[l], w_in[l], conv_w[l], conv_b[l])
        outs, lses = zip(*(_attention_group(qkv, batch, seq, dil)
                           for qkv, dil in zip((qkv0, qkv1, qkv2), GROUP_DILATIONS)))
        outs = [o.reshape(t, GROUP_WIDTH) for o in outs]
        lses = [s.reshape(t, LANES) for s in lses]
        hm = _mlstm(qm, km, vm, gates, batch, seq, mlstm_i_bias[l], mlstm_f_bias[l], mlstm_head_g[l])
        xt = _merge(xt, outs, lses, hm, mix_pre_g[l], w_in[l], w_att_branch[l], w_mlstm_branch[l], w_out[l],
                    mix_post_g[l])
        xt = _ffn(xt, ffn2_pre_g[l], ffn2_w_gate[l], ffn2_w_up[l], ffn2_w_down[l], ffn2_post_g[l])
    return xt.reshape(batch, seq, d)
```

```python
import functools
import math

import numpy as np
import jax
import jax.numpy as jnp
from jax import lax
from jax.experimental import pallas as pl
from jax.experimental.pallas import tpu as pltpu

D_MODEL = 1024
N_GROUPS = 3
GROUP_DILATIONS = (1, 4, 16)
ATT_SPAN = 128
GROUP_HEADS = 4
HEAD_DIM = 128
GROUP_WIDTH = GROUP_HEADS * HEAD_DIM
ATT_WIDTH = N_GROUPS * GROUP_WIDTH
ATT_BLOCK = 128
ROPE_THETA = 500000.0
ROPE_DIMS = HEAD_DIM // 4
ROPE_HALF = ROPE_DIMS // 2
M_HEADS = 4
M_WIDTH = D_MODEL
M_HEAD_DIM = M_WIDTH // M_HEADS
M_CHUNK = 256
ATT_BLOCKS_PER_ITER = 4
M_BATCH = 4
CONV_WIDTH = 4
CONV_HALO = 8
CONV_COLS = 256
RMS_EPS = 1e-6
LANES = 128
MASK_BIAS = -1e30
VMEM_LIMIT_BYTES = 58 * 1024 * 1024
TOKEN_TILE = 512
STAGE_SHAPE = (4, 256, 1024)

_BF16 = jnp.bfloat16
_F32 = jnp.float32
_NT = (((1,), (1,)), ((), ()))
_TN = (((0,), (0,)), ((), ()))


def _rms(x):
    return x * lax.rsqrt(jnp.mean(x * x, axis=-1, keepdims=True) + RMS_EPS)


def _sigmoid(x):
    return 1.0 / (1.0 + jnp.exp(-x))


def _dot(a, b):
    return jnp.dot(a, b, preferred_element_type=_F32)


def _resident(shape):
    return pl.BlockSpec(shape, lambda *_: (0,) * len(shape), pipeline_mode=pl.Buffered(1))


def _params(*semantics):
    return pltpu.CompilerParams(dimension_semantics=semantics, vmem_limit_bytes=VMEM_LIMIT_BYTES)


def _cast_weight_in(src_hbm, dst_ref, stage_ref, sem_ref, jobs=None):
    slots, stage_rows, stage_cols = stage_ref.shape
    if jobs is None:
        rows, cols = dst_ref.shape
        jobs = [(r, c, r, c, min(stage_rows, rows - r), min(stage_cols, cols - c))
                for r in range(0, rows, stage_rows) for c in range(0, cols, stage_cols)]

    def copy(k):
        src_row, src_col, _, _, nr, nc = jobs[k]
        return pltpu.make_async_copy(src_hbm.at[pl.ds(src_row, nr), pl.ds(src_col, nc)],
                                     stage_ref.at[k % slots, pl.ds(0, nr), pl.ds(0, nc)], sem_ref.at[k % slots])

    for k in range(min(slots - 1, len(jobs))):
        copy(k).start()
    for k, (_, _, dst_row, dst_col, nr, nc) in enumerate(jobs):
        if k + slots - 1 < len(jobs):
            copy(k + slots - 1).start()
        copy(k).wait()
        dst_ref[dst_row:dst_row + nr, dst_col:dst_col + nc] = stage_ref[k % slots, 0:nr, 0:nc].astype(_BF16)


def _row_jobs(windows, stage_rows):
    return [(src + r, 0, dst + r, 0, min(stage_rows, n - r), STAGE_SHAPE[2])
            for src, dst, n in windows for r in range(0, n, stage_rows)]


def _ffn_kernel(x_ref, pre_g_ref, post_g_ref, wg_hbm, wu_hbm, wd_hbm, o_ref,
                wg_ref, wu_ref, wd_ref, stage_ref, sem_ref):
    @pl.when(pl.program_id(0) == 0)
    def _():
        _cast_weight_in(wg_hbm, wg_ref, stage_ref, sem_ref)
        _cast_weight_in(wu_hbm, wu_ref, stage_ref, sem_ref)
        _cast_weight_in(wd_hbm, wd_ref, stage_ref, sem_ref)

    x = x_ref[...]
    h = (_rms(x) * pre_g_ref[...]).astype(_BF16)
    g = _dot(h, wg_ref[...])
    u = _dot(h, wu_ref[...])
    a = (g * _sigmoid(g) * u).astype(_BF16)
    f = _dot(a, wd_ref[...])
    o_ref[...] = x + 0.5 * (_rms(f) * post_g_ref[...])


def _weight_scratch(*shapes):
    return ([pltpu.VMEM(shape, _BF16) for shape in shapes]
            + [pltpu.VMEM(STAGE_SHAPE, _F32), pltpu.SemaphoreType.DMA((STAGE_SHAPE[0],))])


_HBM = pl.BlockSpec(memory_space=pl.ANY)


def _ffn(x2d, pre_g, w_gate, w_up, w_down, post_g):
    t, d = x2d.shape
    row = pl.BlockSpec((TOKEN_TILE, d), lambda i: (i, 0))
    return pl.pallas_call(
        _ffn_kernel,
        grid=(t // TOKEN_TILE,),
        in_specs=[row, _resident((1, d)), _resident((1, d)), _HBM, _HBM, _HBM],
        out_specs=row,
        out_shape=jax.ShapeDtypeStruct((t, d), _F32),
        scratch_shapes=_weight_scratch(w_gate.shape, w_up.shape, w_down.shape),
        compiler_params=_params("arbitrary"),
        name="ffn",
    )(x2d, pre_g.reshape(1, d), post_g.reshape(1, d), w_gate, w_up, w_down)


def _inproj_kernel(x_ref, g_ref, w_hbm, tab0_ref, tab1_ref, tab2_ref, conv_w_ref, conv_b_ref,
                   qkv0_ref, qkv1_ref, qkv2_ref, qm_ref, km_ref, vm_ref, gates_ref,
                   w_ref, stage_ref, sem_ref, hn_ref, hs_ref, hp1_ref, hp2_ref, xs_ref, halo_ref, *, tiles_per_seq):
    tm = TOKEN_TILE
    n_slabs = D_MODEL // LANES
    conv_col0 = N_GROUPS * ATT_WIDTH
    vm_col0 = conv_col0 + 2 * M_WIDTH
    gate_col0 = vm_col0 + M_WIDTH

    def project(h_ref, col, width):
        return lax.dot_general(h_ref[...], w_ref[col:col + width, :], _NT, preferred_element_type=_F32)

    @pl.when(pl.program_id(0) == 0)
    def _():
        w_ref[gate_col0 + 2 * M_HEADS:gate_col0 + LANES, :] = jnp.zeros((LANES - 2 * M_HEADS, D_MODEL), _BF16)
        _cast_weight_in(w_hbm, w_ref, stage_ref, sem_ref, _row_jobs(_inproj_windows(), STAGE_SHAPE[1]))

    @pl.when(pl.program_id(0) % tiles_per_seq == 0)
    def _():
        halo_ref[...] = jnp.zeros(halo_ref.shape, _F32)

    hf = _rms(x_ref[...]) * g_ref[...]
    hn_ref[...] = hf.astype(_BF16)
    for c in range(n_slabs):
        hs_ref[c] = hf[:, c * LANES:(c + 1) * LANES]

    lane = lax.broadcasted_iota(jnp.int32, (tm, HEAD_DIM), 1)
    low_half = lane < ROPE_HALF
    q_scale = HEAD_DIM ** -0.5 * math.log2(math.e)

    def regroup(hp_ref, d):
        rows = tm // d
        for r in range(d):
            piece = jnp.concatenate([hs_ref[c, pl.ds(r, rows, stride=d), :] for c in range(n_slabs)], axis=1)
            hp_ref[r * rows:(r + 1) * rows, :] = piece.astype(_BF16)

    def attention_part(g, part, h_ref):
        return project(h_ref, (g * 3 + part) * GROUP_WIDTH, GROUP_WIDTH)

    def attention_store(p, part, qkv_ref, tab_ref, d):
        rows = tm // d
        for hh in range(GROUP_HEADS):
            x = p[:, hh * HEAD_DIM:(hh + 1) * HEAD_DIM]
            if part < 2:
                partner = jnp.where(low_half, pltpu.roll(x, HEAD_DIM - ROPE_HALF, 1), pltpu.roll(x, ROPE_HALF, 1))
                x = x * tab_ref[0] + partner * tab_ref[1]
            if part == 0:
                x = x * q_scale
            x = x.astype(_BF16)
            for r in range(d):
                lo = r * ATT_WIDTH + part * GROUP_WIDTH + hh * HEAD_DIM
                qkv_ref[:, lo:lo + HEAD_DIM] = x[r * rows:(r + 1) * rows, :]

    def conv_dot(cc):
        return project(hn_ref, conv_col0 + cc * CONV_COLS, CONV_COLS)

    def conv_store(p, cc):
        for c in range(CONV_COLS // LANES):
            lanes = slice(cc * CONV_COLS + c * LANES, cc * CONV_COLS + (c + 1) * LANES)
            pc = p[:, c * LANES:(c + 1) * LANES]
            xs_ref[c, pl.ds(0, CONV_HALO, stride=2), :] = halo_ref[:, lanes]
            xs_ref[c, pl.ds(2 * CONV_HALO, tm, stride=2), :] = pc
            halo_ref[:, lanes] = pc[tm - CONV_HALO:tm, :]
            y = conv_b_ref[:, lanes] + conv_w_ref[CONV_WIDTH - 1:CONV_WIDTH, lanes] * pc
            for j in range(CONV_WIDTH - 1):
                shift = CONV_WIDTH - 1 - j
                y = y + conv_w_ref[j:j + 1, lanes] * xs_ref[c, pl.ds(2 * (CONV_HALO - shift), tm, stride=2), :]
            qk = y + y * jnp.tanh(y)
            if lanes.start < M_WIDTH:
                qm_ref[:, lanes] = qk.astype(_BF16)
            else:
                km_ref[:, lanes.start - M_WIDTH:lanes.stop - M_WIDTH] = (qk * M_HEAD_DIM ** -0.5).astype(_BF16)

    def vm_store(p):
        vm_ref[...] = p.astype(_BF16)

    def gates_store(p):
        gates_ref[...] = p

    hps = (hn_ref, hp1_ref, hp2_ref)
    qkvs = (qkv0_ref, qkv1_ref, qkv2_ref)
    tabs = (tab0_ref, tab1_ref, tab2_ref)
    attention = [(functools.partial(attention_part, g, part, hps[g]),
                  functools.partial(attention_store, part=part, qkv_ref=qkvs[g], tab_ref=tabs[g], d=GROUP_DILATIONS[g]))
                 for g in range(N_GROUPS) for part in range(3)]
    convs = [(functools.partial(conv_dot, cc), functools.partial(conv_store, cc=cc))
             for cc in range(2 * M_WIDTH // CONV_COLS)]
    sections = [(lambda: project(hn_ref, vm_col0, M_WIDTH), vm_store),
                (lambda: project(hn_ref, gate_col0, LANES),
                 lambda p: (gates_store(p), regroup(hp1_ref, GROUP_DILATIONS[1]))),
                attention[0], convs[0], attention[1], convs[1], attention[2],
                (convs[2][0], lambda p: (conv_store(p, 2), regroup(hp2_ref, GROUP_DILATIONS[2]))),
                attention[3], convs[3], attention[4], convs[4], attention[5], convs[5],
                attention[6], convs[6], attention[7], convs[7], attention[8]]
    pending = None
    for matmul, epilogue in sections:
        p = matmul()
        if pending is not None:
            pending()
        pending = functools.partial(epilogue, p)
    pending()


def _inproj_windows():
    a, m = ATT_WIDTH, M_WIDTH
    wins = []
    for g in range(N_GROUPS):
        for part in range(3):
            wins.append((part * a + g * GROUP_WIDTH, (g * 3 + part) * GROUP_WIDTH, GROUP_WIDTH))
    base = 3 * a
    wins.append((base, base, 3 * m))
    wins.append((base + 4 * m, base + 3 * m, 2 * M_HEADS))
    return wins


def _rope_tables(seq, dilation):
    inv_freq = jnp.power(ROPE_THETA, -(jnp.arange(ROPE_HALF, dtype=_F32) * 2.0 / ROPE_DIMS))
    ang = jnp.arange(seq, dtype=jnp.int32).astype(_F32)[:, None] * inv_freq[None, :]
    cos, sin = jnp.cos(ang), jnp.sin(ang)
    rest = HEAD_DIM - ROPE_DIMS
    tabs = jnp.stack([jnp.concatenate([cos, cos, jnp.ones((seq, rest), _F32)], axis=-1),
                      jnp.concatenate([-sin, sin, jnp.zeros((seq, rest), _F32)], axis=-1)])
    tabs = tabs.reshape(2, seq // TOKEN_TILE, TOKEN_TILE // dilation, dilation, HEAD_DIM)
    return tabs.transpose(0, 1, 3, 2, 4).reshape(2, seq, HEAD_DIM)


def _inproj(x2d, seq, pre_g, w_in, conv_w, conv_b):
    t, d = x2d.shape
    tiles_per_seq = seq // TOKEN_TILE
    w_cols = N_GROUPS * ATT_WIDTH + 3 * M_WIDTH + LANES
    tables = [_rope_tables(seq, dil) for dil in GROUP_DILATIONS]
    row = lambda width: pl.BlockSpec((TOKEN_TILE, width), lambda i: (i, 0))
    cls = lambda dil: pl.BlockSpec((TOKEN_TILE // dil, dil * ATT_WIDTH), lambda i: (i, 0))
    tab = pl.BlockSpec((2, TOKEN_TILE, HEAD_DIM), lambda i: (0, i % tiles_per_seq, 0))
    out_shape = [jax.ShapeDtypeStruct((t // dil, dil * ATT_WIDTH), _BF16) for dil in GROUP_DILATIONS]
    out_shape += [jax.ShapeDtypeStruct((t, M_WIDTH), _BF16)] * 3
    out_shape.append(jax.ShapeDtypeStruct((t, LANES), _F32))
    return pl.pallas_call(
        functools.partial(_inproj_kernel, tiles_per_seq=tiles_per_seq),
        grid=(t // TOKEN_TILE,),
        in_specs=[row(d), _resident((1, d)), _HBM, tab, tab, tab,
                  _resident((CONV_WIDTH, 2 * M_WIDTH)), _resident((1, 2 * M_WIDTH))],
        out_specs=[cls(dil) for dil in GROUP_DILATIONS] + [row(M_WIDTH)] * 3 + [row(LANES)],
        out_shape=out_shape,
        scratch_shapes=_weight_scratch((w_cols, d)) + [
            pltpu.VMEM((TOKEN_TILE, d), _BF16),
            pltpu.VMEM((d // LANES, TOKEN_TILE, LANES), _F32),
            pltpu.VMEM((TOKEN_TILE, d), _BF16),
            pltpu.VMEM((TOKEN_TILE, d), _BF16),
            pltpu.VMEM((CONV_COLS // LANES, 2 * (CONV_HALO + TOKEN_TILE), LANES), _F32),
            pltpu.VMEM((CONV_HALO, 2 * M_WIDTH), _F32)],
        compiler_params=_params("arbitrary"),
        name="inproj",
    )(x2d, pre_g.reshape(1, d), w_in.T, *tables, 0.5 * conv_w, 0.5 * conv_b.reshape(1, -1))


def _attn_kernel(qkv_ref, bias_two_ref, bias_one_ref, o_ref, lse_ref, *, dilation, n_blocks):
    lane = lax.broadcasted_iota(jnp.int32, (ATT_BLOCK, LANES), 1)

    def block(row0, key0, n_keys, bias_ref):
        ones = jnp.ones((n_keys, HEAD_DIM), _BF16)
        rows = pl.ds(row0, ATT_BLOCK)
        for r in range(dilation):
            scores = []
            for hh in range(GROUP_HEADS):
                qc = r * ATT_WIDTH + hh * HEAD_DIM
                q = qkv_ref[rows, qc:qc + HEAD_DIM]
                k = qkv_ref[pl.ds(key0, n_keys), qc + GROUP_WIDTH:qc + GROUP_WIDTH + HEAD_DIM]
                scores.append(lax.dot_general(q, k, _NT, preferred_element_type=_F32) + bias_ref[...])
            tops = [jnp.max(s, axis=1, keepdims=True) for s in scores]
            probs = [jnp.exp2(s - m).astype(_BF16) for s, m in zip(scores, tops)]
            lse_tile = jnp.zeros((ATT_BLOCK, LANES), _F32)
            for hh in range(GROUP_HEADS):
                vc = r * ATT_WIDTH + 2 * GROUP_WIDTH + hh * HEAD_DIM
                v = qkv_ref[pl.ds(key0, n_keys), vc:vc + HEAD_DIM]
                acc = _dot(probs[hh], jnp.concatenate([v, ones], axis=1))
                denom = acc[:, HEAD_DIM:]
                o = acc[:, :HEAD_DIM] * (1.0 / denom)
                lse_tile = jnp.where(lane == hh, tops[hh] + jnp.log2(denom), lse_tile)
                oc = r * GROUP_WIDTH + hh * HEAD_DIM
                o_ref[rows, oc:oc + HEAD_DIM] = o.astype(_BF16)
            lse_ref[rows, r * LANES:(r + 1) * LANES] = lse_tile * math.log(2.0)

    def two_key_block(i):
        row0 = i * ATT_BLOCK
        if not isinstance(i, int):
            row0 = pl.multiple_of(row0, ATT_BLOCK)
        block(row0, row0 - ATT_BLOCK, 2 * ATT_BLOCK, bias_two_ref)

    block(0, 0, ATT_BLOCK, bias_one_ref)
    per_iter = ATT_BLOCKS_PER_ITER if dilation == 1 else 1
    for i in range(1, min(per_iter, n_blocks)):
        two_key_block(i)

    def body(j, carry):
        for i in range(per_iter):
            two_key_block(j * per_iter + i)
        return carry

    if n_blocks > per_iter:
        lax.fori_loop(1, n_blocks // per_iter, body, 0)


def _band_bias():
    qi = np.arange(ATT_BLOCK)[:, None]
    kj = np.arange(2 * ATT_BLOCK)[None, :]
    valid_two = np.where(kj < ATT_BLOCK, kj >= qi, kj - ATT_BLOCK <= qi)
    valid_one = np.arange(ATT_BLOCK)[None, :] <= qi
    to_bias = lambda valid: jnp.asarray(np.where(valid, 0.0, MASK_BIAS), _F32)
    return to_bias(valid_two), to_bias(valid_one)


def _attention_group(qkv, batch, seq, dilation):
    length = seq // dilation
    view = qkv.reshape(batch, length, dilation * ATT_WIDTH)
    bias_two, bias_one = _band_bias()
    slab = lambda width: pl.BlockSpec((None, length, dilation * width), lambda b: (b, 0, 0))
    o, lse = pl.pallas_call(
        functools.partial(_attn_kernel, dilation=dilation, n_blocks=length // ATT_BLOCK),
        grid=(batch,),
        in_specs=[slab(ATT_WIDTH), _resident(bias_two.shape), _resident(bias_one.shape)],
        out_specs=[slab(GROUP_WIDTH), slab(LANES)],
        out_shape=[jax.ShapeDtypeStruct((batch, length, dilation * GROUP_WIDTH), _BF16),
                   jax.ShapeDtypeStruct((batch, length, dilation * LANES), _F32)],
        compiler_params=_params("parallel"),
        name=f"attn_d{dilation}",
    )(view, bias_two, bias_one)
    return (o.reshape(batch * length, dilation * GROUP_WIDTH), lse.reshape(batch * length, dilation * LANES))


def _mlstm_kernel(qm_ref, km_ref, vm_ref, gates_ref, gate_b_ref, head_g_ref, o_ref, state_ref, m_ref):
    lc = M_CHUNK
    dh = M_HEAD_DIM
    wide = lambda a, n: jnp.concatenate([a] * n, axis=1)

    @pl.when(pl.program_id(1) == 0)
    def _():
        state_ref[...] = jnp.zeros(state_ref.shape, _F32)
        m_ref[...] = jnp.zeros(m_ref.shape, _F32)

    causal = (lax.broadcasted_iota(jnp.int32, (lc, lc), 1) <= lax.broadcasted_iota(jnp.int32, (lc, lc), 0))
    tri = causal.astype(_BF16)
    lane = lax.broadcasted_iota(jnp.int32, (lc, LANES), 1)
    ones = jnp.ones((lc, LANES), _BF16)

    for bb in range(M_BATCH):
        gt = gates_ref[bb] + gate_b_ref[...]
        lf = jnp.minimum(gt, 0.0) - jnp.log(1.0 + jnp.exp(-jnp.abs(gt)))
        hi = lf.astype(_BF16)
        rest = lf - hi.astype(_F32)
        mid = rest.astype(_BF16)
        low = (rest - mid.astype(_F32)).astype(_BF16)
        b_all = _dot(tri, hi) + _dot(tri, mid) + _dot(tri, low)
        ib = jnp.where(lane < M_HEADS, gt, b_all)
        ib_t = ib.T

        for hh in range(M_HEADS):
            hs = slice(hh * dh, (hh + 1) * dh)
            q = qm_ref[bb, :, hs]
            k = km_ref[bb, :, hs]
            v_aug = jnp.concatenate([vm_ref[bb, :, hs], ones], axis=1)
            i_rep = jnp.broadcast_to(ib[:, hh:hh + 1], (lc, LANES))
            b_rep = jnp.broadcast_to(ib[:, M_HEADS + hh:M_HEADS + hh + 1], (lc, LANES))
            u_row = ib_t[hh:hh + 1, :] - ib_t[M_HEADS + hh:M_HEADS + hh + 1, :]
            m_prev = m_ref[bb, hh][0:1, :]
            state = state_ref[bb, hh]

            e = jnp.where(causal, u_row, -jnp.inf)
            big_m = jnp.maximum(m_prev, jnp.broadcast_to(jnp.max(e, axis=1, keepdims=True), (lc, LANES)))
            w = jnp.exp(e - wide(big_m, lc // LANES)) * lax.dot_general(q, k, _NT, preferred_element_type=_F32)
            inter = jnp.exp(m_prev - big_m)
            acc = _dot(w.astype(_BF16), v_aug) + wide(inter, dh // LANES + 1) * _dot(q, state.astype(_BF16))
            den = acc[:, dh:]
            inv = 1.0 / jnp.maximum(jnp.abs(den), jnp.exp(-(b_rep + big_m)))
            h_out = acc[:, :dh] * wide(inv, dh // LANES)

            b_last = b_rep[lc - 1:lc, :]
            g = b_last - b_rep + i_rep
            m_new = jnp.maximum(b_last + m_prev, jnp.max(g, axis=0, keepdims=True))
            a = jnp.exp(g - m_new)
            decay = jnp.exp(b_last + m_prev - m_new)
            av = (wide(a, dh // LANES + 1) * v_aug.astype(_F32)).astype(_BF16)
            state_ref[bb, hh] = (wide(decay, dh // LANES + 1) * state
                                 + lax.dot_general(k, av, _TN, preferred_element_type=_F32))
            m_ref[bb, hh] = jnp.broadcast_to(m_new, m_ref.shape[2:])

            o_ref[bb, :, hs] = (_rms(h_out) * head_g_ref[:, hs]).astype(_BF16)


def _mlstm(qm, km, vm, gates, batch, seq, i_bias, f_bias, head_g):
    n_chunks = seq // M_CHUNK
    gate_b = jnp.pad(jnp.concatenate([i_bias, f_bias]), (0, LANES - 2 * M_HEADS)).reshape(1, LANES)
    chunk = lambda width: pl.BlockSpec((M_BATCH, M_CHUNK, width), lambda b, c: (b, c, 0))
    out = pl.pallas_call(
        _mlstm_kernel,
        grid=(batch // M_BATCH, n_chunks),
        in_specs=[chunk(M_WIDTH), chunk(M_WIDTH), chunk(M_WIDTH), chunk(LANES),
                  _resident((1, LANES)), _resident((1, M_WIDTH))],
        out_specs=chunk(M_WIDTH),
        out_shape=jax.ShapeDtypeStruct((batch, seq, M_WIDTH), _BF16),
        scratch_shapes=[pltpu.VMEM((M_BATCH, M_HEADS, M_HEAD_DIM, M_HEAD_DIM + LANES), _F32),
                        pltpu.VMEM((M_BATCH, M_HEADS, 8, LANES), _F32)],
        compiler_params=_params("parallel", "arbitrary"),
        name="mlstm",
    )(qm.reshape(batch, seq, M_WIDTH), km.reshape(batch, seq, M_WIDTH), vm.reshape(batch, seq, M_WIDTH),
      gates.reshape(batch, seq, LANES), gate_b, head_g.reshape(1, -1))
    return out.reshape(batch * seq, M_WIDTH)


def _merge_kernel(x_ref, o0_ref, o1_ref, o2_ref, l0_ref, l1_ref, l2_ref, hm_ref, pre_g_ref, post_g_ref,
                  w_in_hbm, wa_hbm, wm_hbm, wo_hbm, out_ref,
                  wg_ref, wa_ref, wm_ref, wo_ref, stage_ref, sem_ref, o_tok_ref, lse_tok_ref):
    def gate(h, col):
        return _sigmoid(lax.dot_general(h, wg_ref[col:col + D_MODEL, :], _NT, preferred_element_type=_F32))

    @pl.when(pl.program_id(0) == 0)
    def _():
        o_col0 = N_GROUPS * ATT_WIDTH + 3 * M_WIDTH
        branch_col0 = o_col0 + M_WIDTH + 2 * M_HEADS
        _cast_weight_in(w_in_hbm, wg_ref, stage_ref, sem_ref,
                        _row_jobs([(o_col0, 0, M_WIDTH), (branch_col0, M_WIDTH, 2 * D_MODEL)], STAGE_SHAPE[1]))
        _cast_weight_in(wa_hbm, wa_ref, stage_ref, sem_ref)
        _cast_weight_in(wm_hbm, wm_ref, stage_ref, sem_ref)
        _cast_weight_in(wo_hbm, wo_ref, stage_ref, sem_ref)

    x = x_ref[...]
    h = (_rms(x) * pre_g_ref[...]).astype(_BF16)
    for g, (o_ref, l_ref, d) in enumerate(zip((o0_ref, o1_ref, o2_ref), (l0_ref, l1_ref, l2_ref), GROUP_DILATIONS)):
        rows = TOKEN_TILE // d
        for r in range(d):
            where = pl.ds(r, rows, stride=d) if d > 1 else slice(None)
            lse_tok_ref[g, where, :] = l_ref[:, r * LANES:(r + 1) * LANES]
            for hh in range(GROUP_HEADS):
                col = r * GROUP_WIDTH + hh * HEAD_DIM
                o_tok_ref[g * GROUP_HEADS + hh, where, :] = o_ref[:, col:col + HEAD_DIM].astype(_F32)
    lses = [lse_tok_ref[g] for g in range(N_GROUPS)]
    top = jnp.maximum(jnp.maximum(lses[0], lses[1]), lses[2])
    es = [jnp.exp(l - top) for l in lses]
    inv = 1.0 / (es[0] + es[1] + es[2])
    alphas = [e * inv for e in es]
    heads = []
    for hh in range(GROUP_HEADS):
        acc = None
        for g, alpha in enumerate(alphas):
            term = alpha[:, hh:hh + 1] * o_tok_ref[g * GROUP_HEADS + hh]
            acc = term if acc is None else acc + term
        heads.append(acc.astype(_BF16))
    att = jnp.concatenate(heads, axis=1)
    a = gate(h, M_WIDTH) * _dot(att, wa_ref[...])
    ml = (gate(h, 0) * hm_ref[...].astype(_F32)).astype(_BF16)
    m = gate(h, M_WIDTH + D_MODEL) * _dot(ml, wm_ref[...])
    y = _dot((a + m).astype(_BF16), wo_ref[...])
    out_ref[...] = x + _rms(y) * post_g_ref[...]


def _merge(x2d, outs, lses, hm, pre_g, w_in, w_att, w_ml, w_out, post_g):
    t, d = x2d.shape
    row = lambda width: pl.BlockSpec((TOKEN_TILE, width), lambda i: (i, 0))
    cls = lambda width: [pl.BlockSpec((TOKEN_TILE // dil, dil * width), lambda i: (i, 0)) for dil in GROUP_DILATIONS]
    return pl.pallas_call(
        _merge_kernel,
        grid=(t // TOKEN_TILE,),
        in_specs=[row(d)] + cls(GROUP_WIDTH) + cls(LANES) + [row(M_WIDTH), _resident((1, d)),
                  _resident((1, d))] + [_HBM] * 4,
        out_specs=row(d),
        out_shape=jax.ShapeDtypeStruct((t, d), _F32),
        scratch_shapes=_weight_scratch((M_WIDTH + 2 * D_MODEL, d), w_att.shape, w_ml.shape, w_out.shape) + [
            pltpu.VMEM((N_GROUPS * GROUP_HEADS, TOKEN_TILE, HEAD_DIM), _F32),
            pltpu.VMEM((N_GROUPS, TOKEN_TILE, LANES), _F32)],
        compiler_params=_params("arbitrary"),
        name="merge",
    )(x2d, *outs, *lses, hm, pre_g.reshape(1, d), post_g.reshape(1, d), w_in.T, w_att, w_ml, w_out)


def kernel(x, ffn1_pre_g, ffn1_w_gate, ffn1_w_up, ffn1_w_down, ffn1_post_g, mix_pre_g, w_in, conv_w, conv_b, mlstm_i_bias, mlstm_f_bias, mlstm_head_g, w_att_branch, w_mlstm_branch, w_out, mix_post_g, ffn2_pre_g, ffn2_w_gate, ffn2_w_up, ffn2_w_down, ffn2_post_g):
    batch, seq, d = x.shape
    t = batch * seq
    xt = x.reshape(t, d)
    for l in range(ffn1_pre_g.shape[0]):
        xt = _ffn(xt, ffn1_pre_g[l], ffn1_w_gate[l], ffn1_w_up[l], ffn1_w_down[l], ffn1_post_g[l])
        qkv0, qkv1, qkv2, qm, km, vm, gates = _inproj(xt, seq, mix_pre_g[l], w_in[l], conv_w[l], conv_b[l])
        outs, lses = zip(*(_attention_group(qkv, batch, seq, dil)
                           for qkv, dil in zip((qkv0, qkv1, qkv2), GROUP_DILATIONS)))
        hm = _mlstm(qm, km, vm, gates, batch, seq, mlstm_i_bias[l], mlstm_f_bias[l], mlstm_head_g[l])
        xt = _merge(xt, outs, lses, hm, mix_pre_g[l], w_in[l], w_att_branch[l], w_mlstm_branch[l], w_out[l],
                    mix_post_g[l])
        xt = _ffn(xt, ffn2_pre_g[l], ffn2_w_gate[l], ffn2_w_up[l], ffn2_w_down[l], ffn2_post_g[l])
    return xt.reshape(batch, seq, d)
```

```python
import functools
import math

import numpy as np
import jax
import jax.numpy as jnp
from jax import lax
from jax.experimental import pallas as pl
from jax.experimental.pallas import tpu as pltpu

D_MODEL = 1024
N_GROUPS = 3
GROUP_DILATIONS = (1, 4, 16)
ATT_SPAN = 128
GROUP_HEADS = 4
HEAD_DIM = 128
GROUP_WIDTH = GROUP_HEADS * HEAD_DIM
ATT_WIDTH = N_GROUPS * GROUP_WIDTH
ATT_BLOCK = 128
ROPE_THETA = 500000.0
ROPE_DIMS = HEAD_DIM // 4
ROPE_HALF = ROPE_DIMS // 2
ROPE_GAP = HEAD_DIM // 2 - ROPE_HALF
M_HEADS = 4
M_WIDTH = D_MODEL
M_HEAD_DIM = M_WIDTH // M_HEADS
M_CHUNK = 256
ATT_BLOCKS_PER_ITER = 16
M_BATCH = 4
CONV_WIDTH = 4
CONV_HALO = 8
CONV_COLS = 256
RMS_EPS = 1e-6
LANES = 128
MASK_BIAS = -1e30
VMEM_LIMIT_BYTES = 58 * 1024 * 1024
TOKEN_TILE = 512
MERGE_SUBTILES = 2
FFN_SUBTILES = 2
STAGE_SHAPE = (4, 256, 1024)

_BF16 = jnp.bfloat16
_F32 = jnp.float32
_NT = (((1,), (1,)), ((), ()))
_TN = (((0,), (0,)), ((), ()))


def _rms(x):
    return x * lax.rsqrt(jnp.mean(x * x, axis=-1, keepdims=True) + RMS_EPS)


def _sigmoid(x):
    return 1.0 / (1.0 + jnp.exp(-x))


def _dot(a, b):
    return jnp.dot(a, b, preferred_element_type=_F32)


def _resident(shape):
    return pl.BlockSpec(shape, lambda *_: (0,) * len(shape), pipeline_mode=pl.Buffered(1))


def _params(*semantics):
    return pltpu.CompilerParams(dimension_semantics=semantics, vmem_limit_bytes=VMEM_LIMIT_BYTES)


def _cast_weight_in(src_hbm, dst_ref, stage_ref, sem_ref, jobs=None, scale=None):
    slots, stage_rows, stage_cols = stage_ref.shape
    if jobs is None:
        rows, cols = dst_ref.shape
        jobs = [(r, c, r, c, min(stage_rows, rows - r), min(stage_cols, cols - c))
                for r in range(0, rows, stage_rows) for c in range(0, cols, stage_cols)]

    def copy(k):
        src_row, src_col, _, _, nr, nc = jobs[k]
        return pltpu.make_async_copy(src_hbm.at[pl.ds(src_row, nr), pl.ds(src_col, nc)],
                                     stage_ref.at[k % slots, pl.ds(0, nr), pl.ds(0, nc)], sem_ref.at[k % slots])

    for k in range(min(slots - 1, len(jobs))):
        copy(k).start()
    for k, (_, _, dst_row, dst_col, nr, nc) in enumerate(jobs):
        if k + slots - 1 < len(jobs):
            copy(k + slots - 1).start()
        copy(k).wait()
        block = stage_ref[k % slots, 0:nr, 0:nc]
        dst_ref[dst_row:dst_row + nr, dst_col:dst_col + nc] = (block if scale is None else block * scale).astype(_BF16)


def _row_jobs(windows, stage_rows):
    return [(src + r, 0, dst + r, 0, min(stage_rows, n - r), STAGE_SHAPE[2])
            for src, dst, n in windows for r in range(0, n, stage_rows)]


def _ffn_kernel(x_ref, pre_g_ref, post_g_ref, wg_hbm, wu_hbm, wd_hbm, o_ref,
                wg_ref, wu_ref, wd_ref, stage_ref, sem_ref):
    @pl.when(pl.program_id(0) == 0)
    def _():
        _cast_weight_in(wg_hbm, wg_ref, stage_ref, sem_ref, scale=0.5)
        _cast_weight_in(wu_hbm, wu_ref, stage_ref, sem_ref)
        _cast_weight_in(wd_hbm, wd_ref, stage_ref, sem_ref)

    for sub in range(x_ref.shape[0] // TOKEN_TILE):
        rows = slice(sub * TOKEN_TILE, (sub + 1) * TOKEN_TILE)
        x = x_ref[rows, :]
        h = (_rms(x) * pre_g_ref[...]).astype(_BF16)
        g = _dot(h, wg_ref[...])
        u = _dot(h, wu_ref[...])
        a = ((g + g * jnp.tanh(g)) * u).astype(_BF16)
        f = _dot(a, wd_ref[...])
        o_ref[rows, :] = x + 0.5 * (_rms(f) * post_g_ref[...])


def _weight_scratch(*shapes):
    return ([pltpu.VMEM(shape, _BF16) for shape in shapes]
            + [pltpu.VMEM(STAGE_SHAPE, _F32), pltpu.SemaphoreType.DMA((STAGE_SHAPE[0],))])


_HBM = pl.BlockSpec(memory_space=pl.ANY)


def _ffn(x2d, pre_g, w_gate, w_up, w_down, post_g):
    t, d = x2d.shape
    row = pl.BlockSpec((FFN_SUBTILES * TOKEN_TILE, d), lambda i: (i, 0))
    return pl.pallas_call(
        _ffn_kernel,
        grid=(t // (FFN_SUBTILES * TOKEN_TILE),),
        in_specs=[row, _resident((1, d)), _resident((1, d)), _HBM, _HBM, _HBM],
        out_specs=row,
        out_shape=jax.ShapeDtypeStruct((t, d), _F32),
        scratch_shapes=_weight_scratch(w_gate.shape, w_up.shape, w_down.shape),
        compiler_params=_params("arbitrary"),
        name="ffn",
    )(x2d, pre_g.reshape(1, d), post_g.reshape(1, d), w_gate, w_up, w_down)


def _inproj_kernel(x_ref, g_ref, w_hbm, tab0_ref, tab1_ref, tab2_ref, conv_w_ref, conv_b_ref,
                   qkv0_ref, qkv1_ref, qkv2_ref, qm_ref, km_ref, vm_ref, gates_ref,
                   w_ref, stage_ref, sem_ref, hn_ref, hs_ref, hp1_ref, hp2_ref, xs_ref, halo_ref, *, tiles_per_seq):
    tm = TOKEN_TILE
    n_slabs = D_MODEL // LANES
    conv_col0 = N_GROUPS * ATT_WIDTH
    vm_col0 = conv_col0 + 2 * M_WIDTH
    gate_col0 = vm_col0 + M_WIDTH

    def project(h_ref, col, width):
        return lax.dot_general(h_ref[...], w_ref[col:col + width, :], _NT, preferred_element_type=_F32)

    @pl.when(pl.program_id(0) == 0)
    def _():
        w_ref[gate_col0 + 2 * M_HEADS:gate_col0 + LANES, :] = jnp.zeros((LANES - 2 * M_HEADS, D_MODEL), _BF16)
        _cast_weight_in(w_hbm, w_ref, stage_ref, sem_ref, _row_jobs(_inproj_windows(), STAGE_SHAPE[1]))
        for g in range(N_GROUPS):
            for part in range(2):
                for hh in range(GROUP_HEADS):
                    base = (g * 3 + part) * GROUP_WIDTH + hh * HEAD_DIM
                    head = w_ref[base:base + HEAD_DIM, :]
                    w_ref[base:base + HEAD_DIM, :] = jnp.concatenate(
                        [head[:ROPE_HALF], head[ROPE_DIMS:ROPE_DIMS + ROPE_GAP], head[ROPE_HALF:ROPE_DIMS],
                         head[ROPE_DIMS + ROPE_GAP:]], axis=0)

    @pl.when(pl.program_id(0) % tiles_per_seq == 0)
    def _():
        halo_ref[...] = jnp.zeros(halo_ref.shape, _F32)

    hf = _rms(x_ref[...]) * g_ref[...]
    hn_ref[...] = hf.astype(_BF16)
    for c in range(n_slabs):
        hs_ref[c] = hf[:, c * LANES:(c + 1) * LANES]


    def regroup(hp_ref, d):
        rows = tm // d
        for r in range(d):
            piece = jnp.concatenate([hs_ref[c, pl.ds(r, rows, stride=d), :] for c in range(n_slabs)], axis=1)
            hp_ref[r * rows:(r + 1) * rows, :] = piece.astype(_BF16)

    def attention_part(g, part, h_ref):
        return project(h_ref, (g * 3 + part) * GROUP_WIDTH, GROUP_WIDTH)

    def attention_store(p, part, qkv_ref, tab_ref, d):
        rows = tm // d
        for hh in range(GROUP_HEADS):
            x = p[:, hh * HEAD_DIM:(hh + 1) * HEAD_DIM]
            if part < 2:
                x = x * tab_ref[2 * part] + pltpu.roll(x, HEAD_DIM // 2, 1) * tab_ref[2 * part + 1]
            x = x.astype(_BF16)
            for r in range(d):
                lo = r * ATT_WIDTH + part * GROUP_WIDTH + hh * HEAD_DIM
                qkv_ref[:, lo:lo + HEAD_DIM] = x[r * rows:(r + 1) * rows, :]

    def conv_dot(cc):
        return project(hn_ref, conv_col0 + cc * CONV_COLS, CONV_COLS)

    def conv_store(p, cc):
        for c in range(CONV_COLS // LANES):
            lanes = slice(cc * CONV_COLS + c * LANES, cc * CONV_COLS + (c + 1) * LANES)
            pc = p[:, c * LANES:(c + 1) * LANES]
            xs_ref[c, pl.ds(0, CONV_HALO, stride=2), :] = halo_ref[:, lanes]
            xs_ref[c, pl.ds(2 * CONV_HALO, tm, stride=2), :] = pc
            halo_ref[:, lanes] = pc[tm - CONV_HALO:tm, :]
            y = conv_b_ref[:, lanes] + conv_w_ref[CONV_WIDTH - 1:CONV_WIDTH, lanes] * pc
            for j in range(CONV_WIDTH - 1):
                shift = CONV_WIDTH - 1 - j
                y = y + conv_w_ref[j:j + 1, lanes] * xs_ref[c, pl.ds(2 * (CONV_HALO - shift), tm, stride=2), :]
            qk = y + y * jnp.tanh(y)
            if lanes.start < M_WIDTH:
                qm_ref[:, lanes] = qk.astype(_BF16)
            else:
                km_ref[:, lanes.start - M_WIDTH:lanes.stop - M_WIDTH] = (qk * M_HEAD_DIM ** -0.5).astype(_BF16)

    def vm_store(p):
        vm_ref[...] = p.astype(_BF16)

    def gates_store(p):
        gates_ref[...] = p

    hps = (hn_ref, hp1_ref, hp2_ref)
    qkvs = (qkv0_ref, qkv1_ref, qkv2_ref)
    tabs = (tab0_ref, tab1_ref, tab2_ref)
    attention = [(functools.partial(attention_part, g, part, hps[g]),
                  functools.partial(attention_store, part=part, qkv_ref=qkvs[g], tab_ref=tabs[g], d=GROUP_DILATIONS[g]))
                 for g in range(N_GROUPS) for part in range(3)]
    convs = [(functools.partial(conv_dot, cc), functools.partial(conv_store, cc=cc))
             for cc in range(2 * M_WIDTH // CONV_COLS)]
    sections = [(lambda: project(hn_ref, vm_col0, M_WIDTH), vm_store),
                (lambda: project(hn_ref, gate_col0, LANES),
                 lambda p: (gates_store(p), regroup(hp1_ref, GROUP_DILATIONS[1]))),
                attention[0], convs[0], attention[1], convs[1], attention[2],
                (convs[2][0], lambda p: (conv_store(p, 2), regroup(hp2_ref, GROUP_DILATIONS[2]))),
                attention[3], convs[3], attention[4], convs[4], attention[5], convs[5],
                attention[6], convs[6], attention[7], convs[7], attention[8]]
    pending = None
    for matmul, epilogue in sections:
        p = matmul()
        if pending is not None:
            pending()
        pending = functools.partial(epilogue, p)
    pending()


def _inproj_windows():
    a, m = ATT_WIDTH, M_WIDTH
    wins = []
    for g in range(N_GROUPS):
        for part in range(3):
            wins.append((part * a + g * GROUP_WIDTH, (g * 3 + part) * GROUP_WIDTH, GROUP_WIDTH))
    base = 3 * a
    wins.append((base, base, 3 * m))
    wins.append((base + 4 * m, base + 3 * m, 2 * M_HEADS))
    return wins


def _rope_tables(seq, dilation):
    inv_freq = jnp.power(ROPE_THETA, -(jnp.arange(ROPE_HALF, dtype=_F32) * 2.0 / ROPE_DIMS))
    ang = jnp.arange(seq, dtype=jnp.int32).astype(_F32)[:, None] * inv_freq[None, :]
    cos, sin = jnp.cos(ang), jnp.sin(ang)
    ones = jnp.ones((seq, ROPE_GAP), _F32)
    zeros = jnp.zeros((seq, ROPE_GAP), _F32)
    cos_tab = jnp.concatenate([cos, ones, cos, ones], axis=-1)
    sin_tab = jnp.concatenate([-sin, zeros, sin, zeros], axis=-1)
    q_scale = HEAD_DIM ** -0.5 * math.log2(math.e)
    tabs = jnp.stack([cos_tab * q_scale, sin_tab * q_scale, cos_tab, sin_tab])
    tabs = tabs.reshape(4, seq // TOKEN_TILE, TOKEN_TILE // dilation, dilation, HEAD_DIM)
    return tabs.transpose(0, 1, 3, 2, 4).reshape(4, seq, HEAD_DIM)


def _inproj(x2d, seq, pre_g, w_in, conv_w, conv_b):
    t, d = x2d.shape
    tiles_per_seq = seq // TOKEN_TILE
    w_cols = N_GROUPS * ATT_WIDTH + 3 * M_WIDTH + LANES
    tables = [_rope_tables(seq, dil) for dil in GROUP_DILATIONS]
    row = lambda width: pl.BlockSpec((TOKEN_TILE, width), lambda i: (i, 0))
    cls = lambda dil: pl.BlockSpec((TOKEN_TILE // dil, dil * ATT_WIDTH), lambda i: (i, 0))
    tab = pl.BlockSpec((4, TOKEN_TILE, HEAD_DIM), lambda i: (0, i % tiles_per_seq, 0))
    out_shape = [jax.ShapeDtypeStruct((t // dil, dil * ATT_WIDTH), _BF16) for dil in GROUP_DILATIONS]
    out_shape += [jax.ShapeDtypeStruct((t, M_WIDTH), _BF16)] * 3
    out_shape.append(jax.ShapeDtypeStruct((t, LANES), _F32))
    return pl.pallas_call(
        functools.partial(_inproj_kernel, tiles_per_seq=tiles_per_seq),
        grid=(t // TOKEN_TILE,),
        in_specs=[row(d), _resident((1, d)), _HBM, tab, tab, tab,
                  _resident((CONV_WIDTH, 2 * M_WIDTH)), _resident((1, 2 * M_WIDTH))],
        out_specs=[cls(dil) for dil in GROUP_DILATIONS] + [row(M_WIDTH)] * 3 + [row(LANES)],
        out_shape=out_shape,
        scratch_shapes=_weight_scratch((w_cols, d)) + [
            pltpu.VMEM((TOKEN_TILE, d), _BF16),
            pltpu.VMEM((d // LANES, TOKEN_TILE, LANES), _F32),
            pltpu.VMEM((TOKEN_TILE, d), _BF16),
            pltpu.VMEM((TOKEN_TILE, d), _BF16),
            pltpu.VMEM((CONV_COLS // LANES, 2 * (CONV_HALO + TOKEN_TILE), LANES), _F32),
            pltpu.VMEM((CONV_HALO, 2 * M_WIDTH), _F32)],
        compiler_params=_params("arbitrary"),
        name="inproj",
    )(x2d, pre_g.reshape(1, d), w_in.T, *tables, 0.5 * conv_w, 0.5 * conv_b.reshape(1, -1))


def _attn_kernel(qkv_ref, bias_two_ref, bias_one_ref, o_ref, lse_ref, *, dilation, n_blocks):
    lane = lax.broadcasted_iota(jnp.int32, (ATT_BLOCK, LANES), 1)

    def block(row0, key0, n_keys, bias_ref):
        ones = jnp.ones((n_keys, HEAD_DIM), _BF16)
        rows = pl.ds(row0, ATT_BLOCK)
        for r in range(dilation):
            scores = []
            for hh in range(GROUP_HEADS):
                qc = r * ATT_WIDTH + hh * HEAD_DIM
                q = qkv_ref[rows, qc:qc + HEAD_DIM]
                k = qkv_ref[pl.ds(key0, n_keys), qc + GROUP_WIDTH:qc + GROUP_WIDTH + HEAD_DIM]
                scores.append(lax.dot_general(q, k, _NT, preferred_element_type=_F32) + bias_ref[...])
            tops = [jnp.max(s, axis=1, keepdims=True) for s in scores]
            probs = [jnp.exp2(s - m).astype(_BF16) for s, m in zip(scores, tops)]
            lse_tile = jnp.zeros((ATT_BLOCK, LANES), _F32)
            for hh in range(GROUP_HEADS):
                vc = r * ATT_WIDTH + 2 * GROUP_WIDTH + hh * HEAD_DIM
                v = qkv_ref[pl.ds(key0, n_keys), vc:vc + HEAD_DIM]
                acc = _dot(probs[hh], jnp.concatenate([v, ones], axis=1))
                denom = acc[:, HEAD_DIM:]
                o = acc[:, :HEAD_DIM] * (1.0 / denom)
                lse_tile = jnp.where(lane == hh, tops[hh] + jnp.log2(denom), lse_tile)
                oc = r * GROUP_WIDTH + hh * HEAD_DIM
                o_ref[rows, oc:oc + HEAD_DIM] = o.astype(_BF16)
            lse_ref[rows, r * LANES:(r + 1) * LANES] = lse_tile * math.log(2.0)

    def two_key_block(i):
        row0 = i * ATT_BLOCK
        if not isinstance(i, int):
            row0 = pl.multiple_of(row0, ATT_BLOCK)
        block(row0, row0 - ATT_BLOCK, 2 * ATT_BLOCK, bias_two_ref)

    block(0, 0, ATT_BLOCK, bias_one_ref)
    per_iter = min(n_blocks, ATT_BLOCKS_PER_ITER)
    for i in range(1, min(per_iter, n_blocks)):
        two_key_block(i)

    def body(j, carry):
        for i in range(per_iter):
            two_key_block(j * per_iter + i)
        return carry

    if n_blocks > per_iter:
        lax.fori_loop(1, n_blocks // per_iter, body, 0)


def _band_bias():
    qi = np.arange(ATT_BLOCK)[:, None]
    kj = np.arange(2 * ATT_BLOCK)[None, :]
    valid_two = np.where(kj < ATT_BLOCK, kj >= qi, kj - ATT_BLOCK <= qi)
    valid_one = np.arange(ATT_BLOCK)[None, :] <= qi
    to_bias = lambda valid: jnp.asarray(np.where(valid, 0.0, MASK_BIAS), _F32)
    return to_bias(valid_two), to_bias(valid_one)


def _attention_group(qkv, batch, seq, dilation):
    length = seq // dilation
    view = qkv.reshape(batch, length, dilation * ATT_WIDTH)
    bias_two, bias_one = _band_bias()
    slab = lambda width: pl.BlockSpec((None, length, dilation * width), lambda b: (b, 0, 0))
    o, lse = pl.pallas_call(
        functools.partial(_attn_kernel, dilation=dilation, n_blocks=length // ATT_BLOCK),
        grid=(batch,),
        in_specs=[slab(ATT_WIDTH), _resident(bias_two.shape), _resident(bias_one.shape)],
        out_specs=[slab(GROUP_WIDTH), slab(LANES)],
        out_shape=[jax.ShapeDtypeStruct((batch, length, dilation * GROUP_WIDTH), _BF16),
                   jax.ShapeDtypeStruct((batch, length, dilation * LANES), _F32)],
        compiler_params=_params("parallel"),
        name=f"attn_d{dilation}",
    )(view, bias_two, bias_one)
    return (o.reshape(batch * length, dilation * GROUP_WIDTH), lse.reshape(batch * length, dilation * LANES))


def _mlstm_kernel(qm_ref, km_ref, vm_ref, gates_ref, gate_b_ref, head_g_ref, o_ref, state_ref, m_ref):
    lc = M_CHUNK
    dh = M_HEAD_DIM
    wide = lambda a, n: jnp.concatenate([a] * n, axis=1)

    @pl.when(pl.program_id(1) == 0)
    def _():
        state_ref[...] = jnp.zeros(state_ref.shape, _F32)
        m_ref[...] = jnp.zeros(m_ref.shape, _F32)

    causal = (lax.broadcasted_iota(jnp.int32, (lc, lc), 1) <= lax.broadcasted_iota(jnp.int32, (lc, lc), 0))
    tri = causal.astype(_BF16)
    lane = lax.broadcasted_iota(jnp.int32, (lc, LANES), 1)
    ones = jnp.ones((lc, LANES), _BF16)

    for bb in range(M_BATCH):
        gt = gates_ref[bb] + gate_b_ref[...]
        lf = jnp.minimum(gt, 0.0) - jnp.log(1.0 + jnp.exp(-jnp.abs(gt)))
        hi = lf.astype(_BF16)
        rest = lf - hi.astype(_F32)
        mid = rest.astype(_BF16)
        low = (rest - mid.astype(_F32)).astype(_BF16)
        b_all = _dot(tri, hi) + _dot(tri, mid) + _dot(tri, low)
        ib = jnp.where(lane < M_HEADS, gt, b_all)
        ib_t = ib.T

        for hh in range(M_HEADS):
            hs = slice(hh * dh, (hh + 1) * dh)
            q = qm_ref[bb, :, hs]
            k = km_ref[bb, :, hs]
            v_aug = jnp.concatenate([vm_ref[bb, :, hs], ones], axis=1)
            i_rep = jnp.broadcast_to(ib[:, hh:hh + 1], (lc, LANES))
            b_rep = jnp.broadcast_to(ib[:, M_HEADS + hh:M_HEADS + hh + 1], (lc, LANES))
            u_row = ib_t[hh:hh + 1, :] - ib_t[M_HEADS + hh:M_HEADS + hh + 1, :]
            m_prev = m_ref[bb, hh][0:1, :]
            state = state_ref[bb, hh]

            e = jnp.where(causal, u_row, -jnp.inf)
            big_m = jnp.maximum(m_prev, jnp.broadcast_to(jnp.max(e, axis=1, keepdims=True), (lc, LANES)))
            w = jnp.exp(e - wide(big_m, lc // LANES)) * lax.dot_general(q, k, _NT, preferred_element_type=_F32)
            inter = jnp.exp(m_prev - big_m)
            acc = _dot(w.astype(_BF16), v_aug) + wide(inter, dh // LANES + 1) * _dot(q, state.astype(_BF16))
            den = acc[:, dh:]
            inv = 1.0 / jnp.maximum(jnp.abs(den), jnp.exp(-(b_rep + big_m)))
            h_out = acc[:, :dh] * wide(inv, dh // LANES)

            b_last = b_rep[lc - 1:lc, :]
            g = b_last - b_rep + i_rep
            m_new = jnp.maximum(b_last + m_prev, jnp.max(g, axis=0, keepdims=True))
            a = jnp.exp(g - m_new)
            decay = jnp.exp(b_last + m_prev - m_new)
            av = (wide(a, dh // LANES + 1) * v_aug.astype(_F32)).astype(_BF16)
            state_ref[bb, hh] = (wide(decay, dh // LANES + 1) * state
                                 + lax.dot_general(k, av, _TN, preferred_element_type=_F32))
            m_ref[bb, hh] = jnp.broadcast_to(m_new, m_ref.shape[2:])

            o_ref[bb, :, hs] = (_rms(h_out) * head_g_ref[:, hs]).astype(_BF16)


def _mlstm(qm, km, vm, gates, batch, seq, i_bias, f_bias, head_g):
    n_chunks = seq // M_CHUNK
    gate_b = jnp.pad(jnp.concatenate([i_bias, f_bias]), (0, LANES - 2 * M_HEADS)).reshape(1, LANES)
    chunk = lambda width: pl.BlockSpec((M_BATCH, M_CHUNK, width), lambda b, c: (b, c, 0))
    out = pl.pallas_call(
        _mlstm_kernel,
        grid=(batch // M_BATCH, n_chunks),
        in_specs=[chunk(M_WIDTH), chunk(M_WIDTH), chunk(M_WIDTH), chunk(LANES),
                  _resident((1, LANES)), _resident((1, M_WIDTH))],
        out_specs=chunk(M_WIDTH),
        out_shape=jax.ShapeDtypeStruct((batch, seq, M_WIDTH), _BF16),
        scratch_shapes=[pltpu.VMEM((M_BATCH, M_HEADS, M_HEAD_DIM, M_HEAD_DIM + LANES), _F32),
                        pltpu.VMEM((M_BATCH, M_HEADS, 8, LANES), _F32)],
        compiler_params=_params("parallel", "arbitrary"),
        name="mlstm",
    )(qm.reshape(batch, seq, M_WIDTH), km.reshape(batch, seq, M_WIDTH), vm.reshape(batch, seq, M_WIDTH),
      gates.reshape(batch, seq, LANES), gate_b, head_g.reshape(1, -1))
    return out.reshape(batch * seq, M_WIDTH)


def _merge_kernel(x_ref, o0_ref, o1_ref, o2_ref, l0_ref, l1_ref, l2_ref, hm_ref, pre_g_ref, post_g_ref,
                  w_in_hbm, wa_hbm, wm_hbm, wo_hbm, out_ref,
                  wg_ref, wa_ref, wm_ref, wo_ref, stage_ref, sem_ref, o_tok_ref, lse_tok_ref):
    def gate(h, col):
        return _sigmoid(lax.dot_general(h, wg_ref[col:col + D_MODEL, :], _NT, preferred_element_type=_F32))

    @pl.when(pl.program_id(0) == 0)
    def _():
        o_col0 = N_GROUPS * ATT_WIDTH + 3 * M_WIDTH
        branch_col0 = o_col0 + M_WIDTH + 2 * M_HEADS
        _cast_weight_in(w_in_hbm, wg_ref, stage_ref, sem_ref,
                        _row_jobs([(o_col0, 0, M_WIDTH), (branch_col0, M_WIDTH, 2 * D_MODEL)], STAGE_SHAPE[1]))
        _cast_weight_in(wa_hbm, wa_ref, stage_ref, sem_ref)
        _cast_weight_in(wm_hbm, wm_ref, stage_ref, sem_ref)
        _cast_weight_in(wo_hbm, wo_ref, stage_ref, sem_ref)

    for sub in range(x_ref.shape[0] // TOKEN_TILE):
        tok = slice(sub * TOKEN_TILE, (sub + 1) * TOKEN_TILE)
        x = x_ref[tok, :]
        h = (_rms(x) * pre_g_ref[...]).astype(_BF16)
        gate_o = gate(h, 0)
        gate_a = gate(h, M_WIDTH)
        gate_m = gate(h, M_WIDTH + D_MODEL)
        for g, (o_ref, l_ref, d) in enumerate(zip((o0_ref, o1_ref, o2_ref), (l0_ref, l1_ref, l2_ref),
                                                   GROUP_DILATIONS)):
            rows = TOKEN_TILE // d
            cls_rows = slice(sub * rows, (sub + 1) * rows)
            for r in range(d):
                where = pl.ds(r, rows, stride=d) if d > 1 else slice(None)
                lse_tok_ref[g, where, :] = l_ref[cls_rows, r * LANES:(r + 1) * LANES]
                for hh in range(GROUP_HEADS):
                    col = r * GROUP_WIDTH + hh * HEAD_DIM
                    o_tok_ref[g * GROUP_HEADS + hh, where, :] = o_ref[cls_rows, col:col + HEAD_DIM].astype(_F32)
        lses = [lse_tok_ref[g] for g in range(N_GROUPS)]
        top = jnp.maximum(jnp.maximum(lses[0], lses[1]), lses[2])
        es = [jnp.exp(l - top) for l in lses]
        inv = 1.0 / (es[0] + es[1] + es[2])
        alphas = [e * inv for e in es]
        heads = []
        for hh in range(GROUP_HEADS):
            acc = None
            for g, alpha in enumerate(alphas):
                term = alpha[:, hh:hh + 1] * o_tok_ref[g * GROUP_HEADS + hh]
                acc = term if acc is None else acc + term
            heads.append(acc.astype(_BF16))
        att = jnp.concatenate(heads, axis=1)
        a = gate_a * _dot(att, wa_ref[...])
        ml = (gate_o * hm_ref[tok, :].astype(_F32)).astype(_BF16)
        m = gate_m * _dot(ml, wm_ref[...])
        y = _dot((a + m).astype(_BF16), wo_ref[...])
        out_ref[tok, :] = x + _rms(y) * post_g_ref[...]


def _merge(x2d, outs, lses, hm, pre_g, w_in, w_att, w_ml, w_out, post_g):
    t, d = x2d.shape
    step = MERGE_SUBTILES * TOKEN_TILE
    row = lambda width: pl.BlockSpec((step, width), lambda i: (i, 0))
    cls = lambda width: [pl.BlockSpec((step // dil, dil * width), lambda i: (i, 0)) for dil in GROUP_DILATIONS]
    return pl.pallas_call(
        _merge_kernel,
        grid=(t // step,),
        in_specs=[row(d)] + cls(GROUP_WIDTH) + cls(LANES) + [row(M_WIDTH), _resident((1, d)),
                  _resident((1, d))] + [_HBM] * 4,
        out_specs=row(d),
        out_shape=jax.ShapeDtypeStruct((t, d), _F32),
        scratch_shapes=_weight_scratch((M_WIDTH + 2 * D_MODEL, d), w_att.shape, w_ml.shape, w_out.shape) + [
            pltpu.VMEM((N_GROUPS * GROUP_HEADS, TOKEN_TILE, HEAD_DIM), _F32),
            pltpu.VMEM((N_GROUPS, TOKEN_TILE, LANES), _F32)],
        compiler_params=_params("arbitrary"),
        name="merge",
    )(x2d, *outs, *lses, hm, pre_g.reshape(1, d), post_g.reshape(1, d), w_in.T, w_att, w_ml, w_out)


def kernel(x, ffn1_pre_g, ffn1_w_gate, ffn1_w_up, ffn1_w_down, ffn1_post_g, mix_pre_g, w_in, conv_w, conv_b, mlstm_i_bias, mlstm_f_bias, mlstm_head_g, w_att_branch, w_mlstm_branch, w_out, mix_post_g, ffn2_pre_g, ffn2_w_gate, ffn2_w_up, ffn2_w_down, ffn2_post_g):
    batch, seq, d = x.shape
    t = batch * seq
    xt = x.reshape(t, d)
    for l in range(ffn1_pre_g.shape[0]):
        xt = _ffn(xt, ffn1_pre_g[l], ffn1_w_gate[l], ffn1_w_up[l], ffn1_w_down[l], ffn1_post_g[l])
        qkv0, qkv1, qkv2, qm, km, vm, gates = _inproj(xt, seq, mix_pre_g[l], w_in[l], conv_w[l], conv_b[l])
        outs, lses = zip(*(_attention_group(qkv, batch, seq, dil)
                           for qkv, dil in zip((qkv0, qkv1, qkv2), GROUP_DILATIONS)))
        hm = _mlstm(qm, km, vm, gates, batch, seq, mlstm_i_bias[l], mlstm_f_bias[l], mlstm_head_g[l])
        xt = _merge(xt, outs, lses, hm, mix_pre_g[l], w_in[l], w_att_branch[l], w_mlstm_branch[l], w_out[l],
                    mix_post_g[l])
        xt = _ffn(xt, ffn2_pre_g[l], ffn2_w_gate[l], ffn2_w_up[l], ffn2_w_down[l], ffn2_post_g[l])
    return xt.reshape(batch, seq, d)
```

```python
import functools
import math

import numpy as np
import jax
import jax.numpy as jnp
from jax import lax
from jax.experimental import pallas as pl
from jax.experimental.pallas import tpu as pltpu

D_MODEL = 1024
N_GROUPS = 3
GROUP_DILATIONS = (1, 4, 16)
ATT_SPAN = 128
GROUP_HEADS = 4
HEAD_DIM = 128
GROUP_WIDTH = GROUP_HEADS * HEAD_DIM
ATT_WIDTH = N_GROUPS * GROUP_WIDTH
ATT_BLOCK = 128
ROPE_THETA = 500000.0
ROPE_DIMS = HEAD_DIM // 4
ROPE_HALF = ROPE_DIMS // 2
ROPE_GAP = HEAD_DIM // 2 - ROPE_HALF
M_HEADS = 4
M_WIDTH = D_MODEL
M_HEAD_DIM = M_WIDTH // M_HEADS
M_CHUNK = 256
ATT_BLOCKS_PER_ITER = 16
M_BATCH = 4
CONV_WIDTH = 4
CONV_HALO = 8
CONV_COLS = 256
RMS_EPS = 1e-6
LANES = 128
MASK_BIAS = -1e30
VMEM_LIMIT_BYTES = 58 * 1024 * 1024
TOKEN_TILE = 512
MERGE_SUBTILES = 2
FFN_SUBTILES = 2
STAGE_SHAPE = (4, 256, 1024)

_BF16 = jnp.bfloat16
_F32 = jnp.float32
_NT = (((1,), (1,)), ((), ()))
_TN = (((0,), (0,)), ((), ()))


def _rms(x):
    return x * lax.rsqrt(jnp.mean(x * x, axis=-1, keepdims=True) + RMS_EPS)


def _dot(a, b):
    return jnp.dot(a, b, preferred_element_type=_F32)


def _resident(shape):
    return pl.BlockSpec(shape, lambda *_: (0,) * len(shape), pipeline_mode=pl.Buffered(1))


def _params(*semantics):
    return pltpu.CompilerParams(dimension_semantics=semantics, vmem_limit_bytes=VMEM_LIMIT_BYTES)


def _cast_weight_in(src_hbm, dst_ref, stage_ref, sem_ref, jobs=None, scale=None):
    slots, stage_rows, stage_cols = stage_ref.shape
    if jobs is None:
        rows, cols = dst_ref.shape
        jobs = [(r, c, r, c, min(stage_rows, rows - r), min(stage_cols, cols - c))
                for r in range(0, rows, stage_rows) for c in range(0, cols, stage_cols)]

    def copy(k):
        src_row, src_col, _, _, nr, nc = jobs[k]
        return pltpu.make_async_copy(src_hbm.at[pl.ds(src_row, nr), pl.ds(src_col, nc)],
                                     stage_ref.at[k % slots, pl.ds(0, nr), pl.ds(0, nc)], sem_ref.at[k % slots])

    for k in range(min(slots - 1, len(jobs))):
        copy(k).start()
    for k, (_, _, dst_row, dst_col, nr, nc) in enumerate(jobs):
        if k + slots - 1 < len(jobs):
            copy(k + slots - 1).start()
        copy(k).wait()
        block = stage_ref[k % slots, 0:nr, 0:nc]
        dst_ref[dst_row:dst_row + nr, dst_col:dst_col + nc] = (block if scale is None else block * scale).astype(_BF16)


def _row_jobs(windows, stage_rows):
    return [(src + r, 0, dst + r, 0, min(stage_rows, n - r), STAGE_SHAPE[2])
            for src, dst, n in windows for r in range(0, n, stage_rows)]


def _ffn_kernel(x_ref, pre_g_ref, post_g_ref, wg_hbm, wu_hbm, wd_hbm, o_ref,
                wg_ref, wu_ref, wd_ref, stage_ref, sem_ref):
    @pl.when(pl.program_id(0) == 0)
    def _():
        _cast_weight_in(wg_hbm, wg_ref, stage_ref, sem_ref, scale=0.5)
        _cast_weight_in(wu_hbm, wu_ref, stage_ref, sem_ref)
        _cast_weight_in(wd_hbm, wd_ref, stage_ref, sem_ref)

    for sub in range(x_ref.shape[0] // TOKEN_TILE):
        rows = slice(sub * TOKEN_TILE, (sub + 1) * TOKEN_TILE)
        x = x_ref[rows, :]
        h = (_rms(x) * pre_g_ref[...]).astype(_BF16)
        g = _dot(h, wg_ref[...])
        u = _dot(h, wu_ref[...])
        a = ((g + g * jnp.tanh(g)) * u).astype(_BF16)
        f = _dot(a, wd_ref[...])
        o_ref[rows, :] = x + 0.5 * (_rms(f) * post_g_ref[...])


def _weight_scratch(*shapes):
    return ([pltpu.VMEM(shape, _BF16) for shape in shapes]
            + [pltpu.VMEM(STAGE_SHAPE, _F32), pltpu.SemaphoreType.DMA((STAGE_SHAPE[0],))])


_HBM = pl.BlockSpec(memory_space=pl.ANY)


def _ffn(x2d, pre_g, w_gate, w_up, w_down, post_g):
    t, d = x2d.shape
    row = pl.BlockSpec((FFN_SUBTILES * TOKEN_TILE, d), lambda i: (i, 0))
    return pl.pallas_call(
        _ffn_kernel,
        grid=(t // (FFN_SUBTILES * TOKEN_TILE),),
        in_specs=[row, _resident((1, d)), _resident((1, d)), _HBM, _HBM, _HBM],
        out_specs=row,
        out_shape=jax.ShapeDtypeStruct((t, d), _F32),
        scratch_shapes=_weight_scratch(w_gate.shape, w_up.shape, w_down.shape),
        compiler_params=_params("arbitrary"),
        name="ffn",
    )(x2d, pre_g.reshape(1, d), post_g.reshape(1, d), w_gate, w_up, w_down)


def _inproj_kernel(x_ref, g_ref, w_hbm, tab0_ref, tab1_ref, tab2_ref, conv_w_ref, conv_b_ref,
                   qkv0_ref, qkv1_ref, qkv2_ref, qm_ref, km_ref, vm_ref, gates_ref,
                   w_ref, stage_ref, sem_ref, hn_ref, hs_ref, hp1_ref, hp2_ref, xs_ref, halo_ref, *, tiles_per_seq):
    tm = TOKEN_TILE
    n_slabs = D_MODEL // LANES
    conv_col0 = N_GROUPS * ATT_WIDTH
    vm_col0 = conv_col0 + 2 * M_WIDTH
    gate_col0 = vm_col0 + M_WIDTH

    def project(h_ref, col, width):
        return lax.dot_general(h_ref[...], w_ref[col:col + width, :], _NT, preferred_element_type=_F32)

    @pl.when(pl.program_id(0) == 0)
    def _():
        w_ref[gate_col0 + 2 * M_HEADS:gate_col0 + LANES, :] = jnp.zeros((LANES - 2 * M_HEADS, D_MODEL), _BF16)
        _cast_weight_in(w_hbm, w_ref, stage_ref, sem_ref, _row_jobs(_inproj_windows(), STAGE_SHAPE[1]))
        for g in range(N_GROUPS):
            for part in range(2):
                for hh in range(GROUP_HEADS):
                    base = (g * 3 + part) * GROUP_WIDTH + hh * HEAD_DIM
                    head = w_ref[base:base + HEAD_DIM, :]
                    w_ref[base:base + HEAD_DIM, :] = jnp.concatenate(
                        [head[:ROPE_HALF], head[ROPE_DIMS:ROPE_DIMS + ROPE_GAP], head[ROPE_HALF:ROPE_DIMS],
                         head[ROPE_DIMS + ROPE_GAP:]], axis=0)

    @pl.when(pl.program_id(0) % tiles_per_seq == 0)
    def _():
        halo_ref[...] = jnp.zeros(halo_ref.shape, _F32)

    hf = _rms(x_ref[...]) * g_ref[...]
    hn_ref[...] = hf.astype(_BF16)
    for c in range(n_slabs):
        hs_ref[c] = hf[:, c * LANES:(c + 1) * LANES]


    def regroup(hp_ref, d):
        rows = tm // d
        for r in range(d):
            piece = jnp.concatenate([hs_ref[c, pl.ds(r, rows, stride=d), :] for c in range(n_slabs)], axis=1)
            hp_ref[r * rows:(r + 1) * rows, :] = piece.astype(_BF16)

    def attention_part(g, part, h_ref):
        return project(h_ref, (g * 3 + part) * GROUP_WIDTH, GROUP_WIDTH)

    def attention_store(p, part, qkv_ref, tab_ref, d):
        rows = tm // d
        for hh in range(GROUP_HEADS):
            x = p[:, hh * HEAD_DIM:(hh + 1) * HEAD_DIM]
            if part < 2:
                x = x * tab_ref[2 * part] + pltpu.roll(x, HEAD_DIM // 2, 1) * tab_ref[2 * part + 1]
            x = x.astype(_BF16)
            for r in range(d):
                lo = r * ATT_WIDTH + part * GROUP_WIDTH + hh * HEAD_DIM
                qkv_ref[:, lo:lo + HEAD_DIM] = x[r * rows:(r + 1) * rows, :]

    def conv_dot(cc):
        return project(hn_ref, conv_col0 + cc * CONV_COLS, CONV_COLS)

    def conv_store(p, cc):
        for c in range(CONV_COLS // LANES):
            lanes = slice(cc * CONV_COLS + c * LANES, cc * CONV_COLS + (c + 1) * LANES)
            pc = p[:, c * LANES:(c + 1) * LANES]
            xs_ref[c, pl.ds(0, CONV_HALO, stride=2), :] = halo_ref[:, lanes]
            xs_ref[c, pl.ds(2 * CONV_HALO, tm, stride=2), :] = pc
            halo_ref[:, lanes] = pc[tm - CONV_HALO:tm, :]
            y = conv_b_ref[:, lanes] + conv_w_ref[CONV_WIDTH - 1:CONV_WIDTH, lanes] * pc
            for j in range(CONV_WIDTH - 1):
                shift = CONV_WIDTH - 1 - j
                y = y + conv_w_ref[j:j + 1, lanes] * xs_ref[c, pl.ds(2 * (CONV_HALO - shift), tm, stride=2), :]
            qk = y + y * jnp.tanh(y)
            if lanes.start < M_WIDTH:
                qm_ref[:, lanes] = qk.astype(_BF16)
            else:
                km_ref[:, lanes.start - M_WIDTH:lanes.stop - M_WIDTH] = (qk * M_HEAD_DIM ** -0.5).astype(_BF16)

    def vm_store(p):
        vm_ref[...] = p.astype(_BF16)

    def gates_store(p):
        gates_ref[...] = p

    hps = (hn_ref, hp1_ref, hp2_ref)
    qkvs = (qkv0_ref, qkv1_ref, qkv2_ref)
    tabs = (tab0_ref, tab1_ref, tab2_ref)
    attention = [(functools.partial(attention_part, g, part, hps[g]),
                  functools.partial(attention_store, part=part, qkv_ref=qkvs[g], tab_ref=tabs[g], d=GROUP_DILATIONS[g]))
                 for g in range(N_GROUPS) for part in range(3)]
    convs = [(functools.partial(conv_dot, cc), functools.partial(conv_store, cc=cc))
             for cc in range(2 * M_WIDTH // CONV_COLS)]
    sections = [(lambda: project(hn_ref, vm_col0, M_WIDTH), vm_store),
                (lambda: project(hn_ref, gate_col0, LANES),
                 lambda p: (gates_store(p), regroup(hp1_ref, GROUP_DILATIONS[1]))),
                attention[0], convs[0], attention[1], convs[1], attention[2],
                (convs[2][0], lambda p: (conv_store(p, 2), regroup(hp2_ref, GROUP_DILATIONS[2]))),
                attention[3], convs[3], attention[4], convs[4], attention[5], convs[5],
                attention[6], convs[6], attention[7], convs[7], attention[8]]
    pending = None
    for matmul, epilogue in sections:
        p = matmul()
        if pending is not None:
            pending()
        pending = functools.partial(epilogue, p)
    pending()


def _inproj_windows():
    a, m = ATT_WIDTH, M_WIDTH
    wins = []
    for g in range(N_GROUPS):
        for part in range(3):
            wins.append((part * a + g * GROUP_WIDTH, (g * 3 + part) * GROUP_WIDTH, GROUP_WIDTH))
    base = 3 * a
    wins.append((base, base, 3 * m))
    wins.append((base + 4 * m, base + 3 * m, 2 * M_HEADS))
    return wins


def _rope_tables(seq, dilation):
    inv_freq = np.power(ROPE_THETA, -(np.arange(ROPE_HALF, dtype=np.float64) * 2.0 / ROPE_DIMS))
    ang = np.arange(seq, dtype=np.float64)[:, None] * inv_freq[None, :]
    cos, sin = np.cos(ang), np.sin(ang)
    ones = np.ones((seq, ROPE_GAP))
    zeros = np.zeros((seq, ROPE_GAP))
    cos_tab = np.concatenate([cos, ones, cos, ones], axis=-1)
    sin_tab = np.concatenate([-sin, zeros, sin, zeros], axis=-1)
    q_scale = HEAD_DIM ** -0.5 * math.log2(math.e)
    tabs = np.stack([cos_tab * q_scale, sin_tab * q_scale, cos_tab, sin_tab])
    tabs = tabs.reshape(4, seq // TOKEN_TILE, TOKEN_TILE // dilation, dilation, HEAD_DIM)
    return jnp.asarray(tabs.transpose(0, 1, 3, 2, 4).reshape(4, seq, HEAD_DIM), _F32)


def _inproj(x2d, seq, pre_g, w_in, conv_w, conv_b):
    t, d = x2d.shape
    tiles_per_seq = seq // TOKEN_TILE
    w_cols = N_GROUPS * ATT_WIDTH + 3 * M_WIDTH + LANES
    tables = [_rope_tables(seq, dil) for dil in GROUP_DILATIONS]
    row = lambda width: pl.BlockSpec((TOKEN_TILE, width), lambda i: (i, 0))
    cls = lambda dil: pl.BlockSpec((TOKEN_TILE // dil, dil * ATT_WIDTH), lambda i: (i, 0))
    tab = pl.BlockSpec((4, TOKEN_TILE, HEAD_DIM), lambda i: (0, i % tiles_per_seq, 0))
    out_shape = [jax.ShapeDtypeStruct((t // dil, dil * ATT_WIDTH), _BF16) for dil in GROUP_DILATIONS]
    out_shape += [jax.ShapeDtypeStruct((t, M_WIDTH), _BF16)] * 3
    out_shape.append(jax.ShapeDtypeStruct((t, LANES), _F32))
    return pl.pallas_call(
        functools.partial(_inproj_kernel, tiles_per_seq=tiles_per_seq),
        grid=(t // TOKEN_TILE,),
        in_specs=[row(d), _resident((1, d)), _HBM, tab, tab, tab,
                  _resident((CONV_WIDTH, 2 * M_WIDTH)), _resident((1, 2 * M_WIDTH))],
        out_specs=[cls(dil) for dil in GROUP_DILATIONS] + [row(M_WIDTH)] * 3 + [row(LANES)],
        out_shape=out_shape,
        scratch_shapes=_weight_scratch((w_cols, d)) + [
            pltpu.VMEM((TOKEN_TILE, d), _BF16),
            pltpu.VMEM((d // LANES, TOKEN_TILE, LANES), _F32),
            pltpu.VMEM((TOKEN_TILE, d), _BF16),
            pltpu.VMEM((TOKEN_TILE, d), _BF16),
            pltpu.VMEM((CONV_COLS // LANES, 2 * (CONV_HALO + TOKEN_TILE), LANES), _F32),
            pltpu.VMEM((CONV_HALO, 2 * M_WIDTH), _F32)],
        compiler_params=_params("arbitrary"),
        name="inproj",
    )(x2d, pre_g.reshape(1, d), w_in.T, *tables, 0.5 * conv_w, 0.5 * conv_b.reshape(1, -1))


def _attn_kernel(qkv_ref, bias_two_ref, bias_one_ref, o_ref, lse_ref, *, dilation, n_blocks):
    lane = lax.broadcasted_iota(jnp.int32, (ATT_BLOCK, LANES), 1)

    def block(row0, key0, n_keys, bias_ref):
        ones = jnp.ones((n_keys, HEAD_DIM), _BF16)
        rows = pl.ds(row0, ATT_BLOCK)
        for r in range(dilation):
            scores = []
            for hh in range(GROUP_HEADS):
                qc = r * ATT_WIDTH + hh * HEAD_DIM
                q = qkv_ref[rows, qc:qc + HEAD_DIM]
                k = qkv_ref[pl.ds(key0, n_keys), qc + GROUP_WIDTH:qc + GROUP_WIDTH + HEAD_DIM]
                scores.append(lax.dot_general(q, k, _NT, preferred_element_type=_F32) + bias_ref[...])
            tops = [jnp.max(s, axis=1, keepdims=True) for s in scores]
            probs = [jnp.exp2(s - m).astype(_BF16) for s, m in zip(scores, tops)]
            lse_tile = jnp.zeros((ATT_BLOCK, LANES), _F32)
            for hh in range(GROUP_HEADS):
                vc = r * ATT_WIDTH + 2 * GROUP_WIDTH + hh * HEAD_DIM
                v = qkv_ref[pl.ds(key0, n_keys), vc:vc + HEAD_DIM]
                acc = _dot(probs[hh], jnp.concatenate([v, ones], axis=1))
                denom = acc[:, HEAD_DIM:]
                o = acc[:, :HEAD_DIM] * (1.0 / denom)
                lse_tile = jnp.where(lane == hh, tops[hh] + jnp.log2(denom), lse_tile)
                oc = r * GROUP_WIDTH + hh * HEAD_DIM
                o_ref[rows, oc:oc + HEAD_DIM] = o.astype(_BF16)
            lse_ref[rows, r * LANES:(r + 1) * LANES] = lse_tile * math.log(2.0)

    def two_key_block(i):
        row0 = i * ATT_BLOCK
        if not isinstance(i, int):
            row0 = pl.multiple_of(row0, ATT_BLOCK)
        block(row0, row0 - ATT_BLOCK, 2 * ATT_BLOCK, bias_two_ref)

    block(0, 0, ATT_BLOCK, bias_one_ref)
    per_iter = min(n_blocks, ATT_BLOCKS_PER_ITER)
    for i in range(1, min(per_iter, n_blocks)):
        two_key_block(i)

    def body(j, carry):
        for i in range(per_iter):
            two_key_block(j * per_iter + i)
        return carry

    if n_blocks > per_iter:
        lax.fori_loop(1, n_blocks // per_iter, body, 0)


def _band_bias():
    qi = np.arange(ATT_BLOCK)[:, None]
    kj = np.arange(2 * ATT_BLOCK)[None, :]
    valid_two = np.where(kj < ATT_BLOCK, kj >= qi, kj - ATT_BLOCK <= qi)
    valid_one = np.arange(ATT_BLOCK)[None, :] <= qi
    to_bias = lambda valid: jnp.asarray(np.where(valid, 0.0, MASK_BIAS), _F32)
    return to_bias(valid_two), to_bias(valid_one)


def _attention_group(qkv, batch, seq, dilation):
    length = seq // dilation
    view = qkv.reshape(batch, length, dilation * ATT_WIDTH)
    bias_two, bias_one = _band_bias()
    slab = lambda width: pl.BlockSpec((None, length, dilation * width), lambda b: (b, 0, 0))
    o, lse = pl.pallas_call(
        functools.partial(_attn_kernel, dilation=dilation, n_blocks=length // ATT_BLOCK),
        grid=(batch,),
        in_specs=[slab(ATT_WIDTH), _resident(bias_two.shape), _resident(bias_one.shape)],
        out_specs=[slab(GROUP_WIDTH), slab(LANES)],
        out_shape=[jax.ShapeDtypeStruct((batch, length, dilation * GROUP_WIDTH), _BF16),
                   jax.ShapeDtypeStruct((batch, length, dilation * LANES), _F32)],
        compiler_params=_params("parallel"),
        name=f"attn_d{dilation}",
    )(view, bias_two, bias_one)
    return (o.reshape(batch * length, dilation * GROUP_WIDTH), lse.reshape(batch * length, dilation * LANES))


def _mlstm_kernel(qm_ref, km_ref, vm_ref, gates_ref, gate_b_ref, head_g_ref, o_ref, state_ref, m_ref):
    lc = M_CHUNK
    dh = M_HEAD_DIM
    wide = lambda a, n: jnp.concatenate([a] * n, axis=1)

    @pl.when(pl.program_id(1) == 0)
    def _():
        state_ref[...] = jnp.zeros(state_ref.shape, _F32)
        m_ref[...] = jnp.zeros(m_ref.shape, _F32)

    causal = (lax.broadcasted_iota(jnp.int32, (lc, lc), 1) <= lax.broadcasted_iota(jnp.int32, (lc, lc), 0))
    tri = causal.astype(_BF16)
    lane = lax.broadcasted_iota(jnp.int32, (lc, LANES), 1)
    ones = jnp.ones((lc, LANES), _BF16)

    for bb in range(M_BATCH):
        gt = gates_ref[bb] + gate_b_ref[...]
        lf = jnp.minimum(gt, 0.0) - jnp.log(1.0 + jnp.exp(-jnp.abs(gt)))
        hi = lf.astype(_BF16)
        rest = lf - hi.astype(_F32)
        mid = rest.astype(_BF16)
        low = (rest - mid.astype(_F32)).astype(_BF16)
        b_all = _dot(tri, hi) + _dot(tri, mid) + _dot(tri, low)
        ib = jnp.where(lane < M_HEADS, gt, b_all)
        ib_t = ib.T

        for hh in range(M_HEADS):
            hs = slice(hh * dh, (hh + 1) * dh)
            q = qm_ref[bb, :, hs]
            k = km_ref[bb, :, hs]
            v_aug = jnp.concatenate([vm_ref[bb, :, hs], ones], axis=1)
            i_rep = jnp.broadcast_to(ib[:, hh:hh + 1], (lc, LANES))
            b_rep = jnp.broadcast_to(ib[:, M_HEADS + hh:M_HEADS + hh + 1], (lc, LANES))
            u_row = ib_t[hh:hh + 1, :] - ib_t[M_HEADS + hh:M_HEADS + hh + 1, :]
            m_prev = m_ref[bb, hh][0:1, :]
            state = state_ref[bb, hh]

            e = jnp.where(causal, u_row, -jnp.inf)
            big_m = jnp.maximum(m_prev, jnp.broadcast_to(jnp.max(e, axis=1, keepdims=True), (lc, LANES)))
            w = jnp.exp(e - wide(big_m, lc // LANES)) * lax.dot_general(q, k, _NT, preferred_element_type=_F32)
            inter = jnp.exp(m_prev - big_m)
            acc = _dot(w.astype(_BF16), v_aug) + wide(inter, dh // LANES + 1) * _dot(q, state.astype(_BF16))
            den = acc[:, dh:]
            inv = 1.0 / jnp.maximum(jnp.abs(den), jnp.exp(-(b_rep + big_m)))
            h_out = acc[:, :dh] * wide(inv, dh // LANES)

            b_last = b_rep[lc - 1:lc, :]
            g = b_last - b_rep + i_rep
            m_new = jnp.maximum(b_last + m_prev, jnp.max(g, axis=0, keepdims=True))
            a = jnp.exp(g - m_new)
            decay = jnp.exp(b_last + m_prev - m_new)
            av = (wide(a, dh // LANES + 1) * v_aug.astype(_F32)).astype(_BF16)
            state_ref[bb, hh] = (wide(decay, dh // LANES + 1) * state
                                 + lax.dot_general(k, av, _TN, preferred_element_type=_F32))
            m_ref[bb, hh] = jnp.broadcast_to(m_new, m_ref.shape[2:])

            o_ref[bb, :, hs] = (_rms(h_out) * head_g_ref[:, hs]).astype(_BF16)


def _mlstm(qm, km, vm, gates, batch, seq, i_bias, f_bias, head_g):
    n_chunks = seq // M_CHUNK
    gate_b = jnp.pad(jnp.concatenate([i_bias, f_bias]), (0, LANES - 2 * M_HEADS)).reshape(1, LANES)
    chunk = lambda width: pl.BlockSpec((M_BATCH, M_CHUNK, width), lambda b, c: (b, c, 0))
    out = pl.pallas_call(
        _mlstm_kernel,
        grid=(batch // M_BATCH, n_chunks),
        in_specs=[chunk(M_WIDTH), chunk(M_WIDTH), chunk(M_WIDTH), chunk(LANES),
                  _resident((1, LANES)), _resident((1, M_WIDTH))],
        out_specs=chunk(M_WIDTH),
        out_shape=jax.ShapeDtypeStruct((batch, seq, M_WIDTH), _BF16),
        scratch_shapes=[pltpu.VMEM((M_BATCH, M_HEADS, M_HEAD_DIM, M_HEAD_DIM + LANES), _F32),
                        pltpu.VMEM((M_BATCH, M_HEADS, 8, LANES), _F32)],
        compiler_params=_params("parallel", "arbitrary"),
        name="mlstm",
    )(qm.reshape(batch, seq, M_WIDTH), km.reshape(batch, seq, M_WIDTH), vm.reshape(batch, seq, M_WIDTH),
      gates.reshape(batch, seq, LANES), gate_b, head_g.reshape(1, -1))
    return out.reshape(batch * seq, M_WIDTH)


def _merge_kernel(x_ref, o0_ref, o1_ref, o2_ref, l0_ref, l1_ref, l2_ref, hm_ref, pre_g_ref, post_g_ref,
                  w_in_hbm, wa_hbm, wm_hbm, wo_hbm, out_ref,
                  wg_ref, wa_ref, wm_ref, wo_ref, stage_ref, sem_ref, o_tok_ref, lse_tok_ref):
    def gate(h, col):
        return jnp.tanh(lax.dot_general(h, wg_ref[col:col + D_MODEL, :], _NT, preferred_element_type=_F32))

    @pl.when(pl.program_id(0) == 0)
    def _():
        o_col0 = N_GROUPS * ATT_WIDTH + 3 * M_WIDTH
        branch_col0 = o_col0 + M_WIDTH + 2 * M_HEADS
        _cast_weight_in(w_in_hbm, wg_ref, stage_ref, sem_ref,
                        _row_jobs([(o_col0, 0, M_WIDTH), (branch_col0, M_WIDTH, 2 * D_MODEL)], STAGE_SHAPE[1]),
                        scale=0.5)
        _cast_weight_in(wa_hbm, wa_ref, stage_ref, sem_ref, scale=0.5)
        _cast_weight_in(wm_hbm, wm_ref, stage_ref, sem_ref, scale=0.25)
        _cast_weight_in(wo_hbm, wo_ref, stage_ref, sem_ref)

    for sub in range(x_ref.shape[0] // TOKEN_TILE):
        tok = slice(sub * TOKEN_TILE, (sub + 1) * TOKEN_TILE)
        x = x_ref[tok, :]
        h = (_rms(x) * pre_g_ref[...]).astype(_BF16)
        gate_o = gate(h, 0)
        gate_a = gate(h, M_WIDTH)
        gate_m = gate(h, M_WIDTH + D_MODEL)
        for g, (o_ref, l_ref, d) in enumerate(zip((o0_ref, o1_ref, o2_ref), (l0_ref, l1_ref, l2_ref),
                                                   GROUP_DILATIONS)):
            rows = TOKEN_TILE // d
            cls_rows = slice(sub * rows, (sub + 1) * rows)
            for r in range(d):
                where = pl.ds(r, rows, stride=d) if d > 1 else slice(None)
                lse_tok_ref[g, where, :] = l_ref[cls_rows, r * LANES:(r + 1) * LANES]
                for hh in range(GROUP_HEADS):
                    col = r * GROUP_WIDTH + hh * HEAD_DIM
                    o_tok_ref[g * GROUP_HEADS + hh, where, :] = o_ref[cls_rows, col:col + HEAD_DIM].astype(_F32)
        lses = [lse_tok_ref[g] for g in range(N_GROUPS)]
        top = jnp.maximum(jnp.maximum(lses[0], lses[1]), lses[2])
        es = [jnp.exp(l - top) for l in lses]
        inv = 1.0 / (es[0] + es[1] + es[2])
        alphas = [e * inv for e in es]
        heads = []
        for hh in range(GROUP_HEADS):
            acc = None
            for g, alpha in enumerate(alphas):
                term = alpha[:, hh:hh + 1] * o_tok_ref[g * GROUP_HEADS + hh]
                acc = term if acc is None else acc + term
            heads.append(acc.astype(_BF16))
        att = jnp.concatenate(heads, axis=1)
        half_a = _dot(att, wa_ref[...])
        a = half_a + gate_a * half_a
        hm = hm_ref[tok, :].astype(_F32)
        ml = (hm + gate_o * hm).astype(_BF16)
        half_m = _dot(ml, wm_ref[...])
        m = half_m + gate_m * half_m
        y = _dot((a + m).astype(_BF16), wo_ref[...])
        out_ref[tok, :] = x + _rms(y) * post_g_ref[...]


def _merge(x2d, outs, lses, hm, pre_g, w_in, w_att, w_ml, w_out, post_g):
    t, d = x2d.shape
    step = MERGE_SUBTILES * TOKEN_TILE
    row = lambda width: pl.BlockSpec((step, width), lambda i: (i, 0))
    cls = lambda width: [pl.BlockSpec((step // dil, dil * width), lambda i: (i, 0)) for dil in GROUP_DILATIONS]
    return pl.pallas_call(
        _merge_kernel,
        grid=(t // step,),
        in_specs=[row(d)] + cls(GROUP_WIDTH) + cls(LANES) + [row(M_WIDTH), _resident((1, d)),
                  _resident((1, d))] + [_HBM] * 4,
        out_specs=row(d),
        out_shape=jax.ShapeDtypeStruct((t, d), _F32),
        scratch_shapes=_weight_scratch((M_WIDTH + 2 * D_MODEL, d), w_att.shape, w_ml.shape, w_out.shape) + [
            pltpu.VMEM((N_GROUPS * GROUP_HEADS, TOKEN_TILE, HEAD_DIM), _F32),
            pltpu.VMEM((N_GROUPS, TOKEN_TILE, LANES), _F32)],
        compiler_params=_params("arbitrary"),
        name="merge",
    )(x2d, *outs, *lses, hm, pre_g.reshape(1, d), post_g.reshape(1, d), w_in.T, w_att, w_ml, w_out)


def kernel(x, ffn1_pre_g, ffn1_w_gate, ffn1_w_up, ffn1_w_down, ffn1_post_g, mix_pre_g, w_in, conv_w, conv_b, mlstm_i_bias, mlstm_f_bias, mlstm_head_g, w_att_branch, w_mlstm_branch, w_out, mix_post_g, ffn2_pre_g, ffn2_w_gate, ffn2_w_up, ffn2_w_down, ffn2_post_g):
    batch, seq, d = x.shape
    t = batch * seq
    xt = x.reshape(t, d)
    for l in range(ffn1_pre_g.shape[0]):
        xt = _ffn(xt, ffn1_pre_g[l], ffn1_w_gate[l], ffn1_w_up[l], ffn1_w_down[l], ffn1_post_g[l])
        qkv0, qkv1, qkv2, qm, km, vm, gates = _inproj(xt, seq, mix_pre_g[l], w_in[l], conv_w[l], conv_b[l])
        outs, lses = zip(*(_attention_group(qkv, batch, seq, dil)
                           for qkv, dil in zip((qkv0, qkv1, qkv2), GROUP_DILATIONS)))
        hm = _mlstm(qm, km, vm, gates, batch, seq, mlstm_i_bias[l], mlstm_f_bias[l], mlstm_head_g[l])
        xt = _merge(xt, outs, lses, hm, mix_pre_g[l], w_in[l], w_att_branch[l], w_mlstm_branch[l], w_out[l],
                    mix_post_g[l])
        xt = _ffn(xt, ffn2_pre_g[l], ffn2_w_gate[l], ffn2_w_up[l], ffn2_w_down[l], ffn2_post_g[l])
    return xt.reshape(batch, seq, d)
```

```python
import functools
import math

import numpy as np
import jax
import jax.numpy as jnp
from jax import lax
from jax.experimental import pallas as pl
from jax.experimental.pallas import tpu as pltpu

D_MODEL = 1024
N_GROUPS = 3
GROUP_DILATIONS = (1, 4, 16)
ATT_SPAN = 128
GROUP_HEADS = 4
HEAD_DIM = 128
GROUP_WIDTH = GROUP_HEADS * HEAD_DIM
ATT_WIDTH = N_GROUPS * GROUP_WIDTH
ATT_BLOCK = 128
ROPE_THETA = 500000.0
ROPE_DIMS = HEAD_DIM // 4
ROPE_HALF = ROPE_DIMS // 2
ROPE_GAP = HEAD_DIM // 2 - ROPE_HALF
M_HEADS = 4
M_WIDTH = D_MODEL
M_HEAD_DIM = M_WIDTH // M_HEADS
M_CHUNK = 256
ATT_BLOCKS_PER_ITER = 16
M_BATCH = 4
CONV_WIDTH = 4
CONV_HALO = 8
CONV_COLS = 256
RMS_EPS = 1e-6
LANES = 128
MASK_BIAS = -1e30
VMEM_LIMIT_BYTES = 58 * 1024 * 1024
TOKEN_TILE = 512
MERGE_SUBTILES = 2
FFN_SUBTILES = 2
STAGE_SHAPE = (4, 256, 1024)

_BF16 = jnp.bfloat16
_F32 = jnp.float32
_NT = (((1,), (1,)), ((), ()))
_TN = (((0,), (0,)), ((), ()))


def _rms(x):
    return x * lax.rsqrt(jnp.mean(x * x, axis=-1, keepdims=True) + RMS_EPS)


def _dot(a, b):
    return jnp.dot(a, b, preferred_element_type=_F32)


def _resident(shape):
    return pl.BlockSpec(shape, lambda *_: (0,) * len(shape), pipeline_mode=pl.Buffered(1))


def _params(*semantics):
    return pltpu.CompilerParams(dimension_semantics=semantics, vmem_limit_bytes=VMEM_LIMIT_BYTES)


def _cast_weight_in(src_hbm, dst_ref, stage_ref, sem_ref, jobs=None, scale=None):
    slots, stage_rows, stage_cols = stage_ref.shape
    if jobs is None:
        rows, cols = dst_ref.shape
        jobs = [(r, c, r, c, min(stage_rows, rows - r), min(stage_cols, cols - c))
                for r in range(0, rows, stage_rows) for c in range(0, cols, stage_cols)]

    def copy(k):
        src_row, src_col, _, _, nr, nc = jobs[k]
        return pltpu.make_async_copy(src_hbm.at[pl.ds(src_row, nr), pl.ds(src_col, nc)],
                                     stage_ref.at[k % slots, pl.ds(0, nr), pl.ds(0, nc)], sem_ref.at[k % slots])

    for k in range(min(slots - 1, len(jobs))):
        copy(k).start()
    for k, (_, _, dst_row, dst_col, nr, nc) in enumerate(jobs):
        if k + slots - 1 < len(jobs):
            copy(k + slots - 1).start()
        copy(k).wait()
        block = stage_ref[k % slots, 0:nr, 0:nc]
        dst_ref[dst_row:dst_row + nr, dst_col:dst_col + nc] = (block if scale is None else block * scale).astype(_BF16)


def _row_jobs(windows, stage_rows):
    return [(src + r, 0, dst + r, 0, min(stage_rows, n - r), STAGE_SHAPE[2])
            for src, dst, n in windows for r in range(0, n, stage_rows)]


def _ffn_kernel(x_ref, pre_g_ref, post_g_ref, wg_hbm, wu_hbm, wd_hbm, o_ref,
                wg_ref, wu_ref, wd_ref, stage_ref, sem_ref):
    @pl.when(pl.program_id(0) == 0)
    def _():
        _cast_weight_in(wg_hbm, wg_ref, stage_ref, sem_ref, scale=0.5)
        _cast_weight_in(wu_hbm, wu_ref, stage_ref, sem_ref)
        _cast_weight_in(wd_hbm, wd_ref, stage_ref, sem_ref)

    for sub in range(x_ref.shape[0] // TOKEN_TILE):
        rows = slice(sub * TOKEN_TILE, (sub + 1) * TOKEN_TILE)
        x = x_ref[rows, :]
        h = (_rms(x) * pre_g_ref[...]).astype(_BF16)
        g = _dot(h, wg_ref[...])
        u = _dot(h, wu_ref[...])
        a = ((g + g * jnp.tanh(g)) * u).astype(_BF16)
        f = _dot(a, wd_ref[...])
        o_ref[rows, :] = x + _rms(f) * post_g_ref[...]


def _weight_scratch(*shapes):
    return ([pltpu.VMEM(shape, _BF16) for shape in shapes]
            + [pltpu.VMEM(STAGE_SHAPE, _F32), pltpu.SemaphoreType.DMA((STAGE_SHAPE[0],))])


_HBM = pl.BlockSpec(memory_space=pl.ANY)


def _ffn(x2d, pre_g, w_gate, w_up, w_down, post_g):
    t, d = x2d.shape
    row = pl.BlockSpec((FFN_SUBTILES * TOKEN_TILE, d), lambda i: (i, 0))
    return pl.pallas_call(
        _ffn_kernel,
        grid=(t // (FFN_SUBTILES * TOKEN_TILE),),
        in_specs=[row, _resident((1, d)), _resident((1, d)), _HBM, _HBM, _HBM],
        out_specs=row,
        out_shape=jax.ShapeDtypeStruct((t, d), _F32),
        scratch_shapes=_weight_scratch(w_gate.shape, w_up.shape, w_down.shape),
        compiler_params=_params("arbitrary"),
        name="ffn",
    )(x2d, pre_g.reshape(1, d), 0.5 * post_g.reshape(1, d), w_gate, w_up, w_down)


def _inproj_kernel(x_ref, g_ref, w_hbm, tab0_ref, tab1_ref, tab2_ref, conv_w_ref, conv_b_ref,
                   qkv0_ref, qkv1_ref, qkv2_ref, qm_ref, km_ref, vm_ref, gates_ref,
                   w_ref, stage_ref, sem_ref, hn_ref, hs_ref, hs4_ref, hp1_ref, hp2_ref, xs_ref, halo_ref, *, tiles_per_seq):
    tm = TOKEN_TILE
    n_slabs = D_MODEL // LANES
    conv_col0 = N_GROUPS * ATT_WIDTH
    vm_col0 = conv_col0 + 2 * M_WIDTH
    gate_col0 = vm_col0 + M_WIDTH

    def project(h_ref, col, width):
        return lax.dot_general(h_ref[...], w_ref[col:col + width, :], _NT, preferred_element_type=_F32)

    @pl.when(pl.program_id(0) == 0)
    def _():
        w_ref[gate_col0 + 2 * M_HEADS:gate_col0 + LANES, :] = jnp.zeros((LANES - 2 * M_HEADS, D_MODEL), _BF16)
        _cast_weight_in(w_hbm, w_ref, stage_ref, sem_ref, _row_jobs(_inproj_windows(), STAGE_SHAPE[1]))
        for g in range(N_GROUPS):
            for part in range(2):
                for hh in range(GROUP_HEADS):
                    base = (g * 3 + part) * GROUP_WIDTH + hh * HEAD_DIM
                    head = w_ref[base:base + HEAD_DIM, :]
                    w_ref[base:base + HEAD_DIM, :] = jnp.concatenate(
                        [head[:ROPE_HALF], head[ROPE_DIMS:ROPE_DIMS + ROPE_GAP], head[ROPE_HALF:ROPE_DIMS],
                         head[ROPE_DIMS + ROPE_GAP:]], axis=0)

    @pl.when(pl.program_id(0) % tiles_per_seq == 0)
    def _():
        halo_ref[...] = jnp.zeros(halo_ref.shape, _F32)

    hf = _rms(x_ref[...]) * g_ref[...]
    hn_ref[...] = hf.astype(_BF16)
    for c in range(n_slabs):
        hs_ref[c] = hf[:, c * LANES:(c + 1) * LANES]


    def regroup(hp_ref, d):
        rows = tm // d
        quarter = tm // GROUP_DILATIONS[1]
        for r in range(d):
            if d == GROUP_DILATIONS[1]:
                pieces = [hs_ref[c, pl.ds(r, rows, stride=d), :] for c in range(n_slabs)]
                for c in range(n_slabs):
                    hs4_ref[c, r * rows:(r + 1) * rows, :] = pieces[c]
            else:
                start = (r % GROUP_DILATIONS[1]) * quarter + r // GROUP_DILATIONS[1]
                pieces = [hs4_ref[c, pl.ds(start, rows, stride=GROUP_DILATIONS[1]), :] for c in range(n_slabs)]
            hp_ref[r * rows:(r + 1) * rows, :] = jnp.concatenate(pieces, axis=1).astype(_BF16)

    def attention_part(g, part, h_ref):
        return project(h_ref, (g * 3 + part) * GROUP_WIDTH, GROUP_WIDTH)

    def attention_store(p, part, qkv_ref, tab_ref, d):
        rows = tm // d
        for hh in range(GROUP_HEADS):
            x = p[:, hh * HEAD_DIM:(hh + 1) * HEAD_DIM]
            if part < 2:
                x = x * tab_ref[2 * part] + pltpu.roll(x, HEAD_DIM // 2, 1) * tab_ref[2 * part + 1]
            x = x.astype(_BF16)
            for r in range(d):
                lo = r * ATT_WIDTH + part * GROUP_WIDTH + hh * HEAD_DIM
                qkv_ref[:, lo:lo + HEAD_DIM] = x[r * rows:(r + 1) * rows, :]

    def conv_dot(cc):
        return project(hn_ref, conv_col0 + cc * CONV_COLS, CONV_COLS)

    def conv_store(p, cc):
        for c in range(CONV_COLS // LANES):
            lanes = slice(cc * CONV_COLS + c * LANES, cc * CONV_COLS + (c + 1) * LANES)
            pc = p[:, c * LANES:(c + 1) * LANES]
            xs_ref[c, pl.ds(0, CONV_HALO, stride=2), :] = halo_ref[:, lanes]
            xs_ref[c, pl.ds(2 * CONV_HALO, tm, stride=2), :] = pc
            halo_ref[:, lanes] = pc[tm - CONV_HALO:tm, :]
            y = conv_b_ref[:, lanes] + conv_w_ref[CONV_WIDTH - 1:CONV_WIDTH, lanes] * pc
            for j in range(CONV_WIDTH - 1):
                shift = CONV_WIDTH - 1 - j
                y = y + conv_w_ref[j:j + 1, lanes] * xs_ref[c, pl.ds(2 * (CONV_HALO - shift), tm, stride=2), :]
            qk = y + y * jnp.tanh(y)
            if lanes.start < M_WIDTH:
                qm_ref[:, lanes] = qk.astype(_BF16)
            else:
                km_ref[:, lanes.start - M_WIDTH:lanes.stop - M_WIDTH] = (qk * M_HEAD_DIM ** -0.5).astype(_BF16)

    def vm_store(p):
        vm_ref[...] = p.astype(_BF16)

    def gates_store(p):
        gates_ref[...] = p

    hps = (hn_ref, hp1_ref, hp2_ref)
    qkvs = (qkv0_ref, qkv1_ref, qkv2_ref)
    tabs = (tab0_ref, tab1_ref, tab2_ref)
    attention = [(functools.partial(attention_part, g, part, hps[g]),
                  functools.partial(attention_store, part=part, qkv_ref=qkvs[g], tab_ref=tabs[g], d=GROUP_DILATIONS[g]))
                 for g in range(N_GROUPS) for part in range(3)]
    convs = [(functools.partial(conv_dot, cc), functools.partial(conv_store, cc=cc))
             for cc in range(2 * M_WIDTH // CONV_COLS)]
    sections = [(lambda: project(hn_ref, vm_col0, M_WIDTH), vm_store),
                (lambda: project(hn_ref, gate_col0, LANES),
                 lambda p: (gates_store(p), regroup(hp1_ref, GROUP_DILATIONS[1]))),
                attention[0], convs[0], attention[1], convs[1], attention[2],
                (convs[2][0], lambda p: (conv_store(p, 2), regroup(hp2_ref, GROUP_DILATIONS[2]))),
                attention[3], convs[3], attention[4], convs[4], attention[5], convs[5],
                attention[6], convs[6], attention[7], convs[7], attention[8]]
    pending = None
    for matmul, epilogue in sections:
        p = matmul()
        if pending is not None:
            pending()
        pending = functools.partial(epilogue, p)
    pending()


def _inproj_windows():
    a, m = ATT_WIDTH, M_WIDTH
    wins = []
    for g in range(N_GROUPS):
        for part in range(3):
            wins.append((part * a + g * GROUP_WIDTH, (g * 3 + part) * GROUP_WIDTH, GROUP_WIDTH))
    base = 3 * a
    wins.append((base, base, 3 * m))
    wins.append((base + 4 * m, base + 3 * m, 2 * M_HEADS))
    return wins


def _rope_tables(seq, dilation):
    inv_freq = np.power(ROPE_THETA, -(np.arange(ROPE_HALF, dtype=np.float64) * 2.0 / ROPE_DIMS))
    ang = np.arange(seq, dtype=np.float64)[:, None] * inv_freq[None, :]
    cos, sin = np.cos(ang), np.sin(ang)
    ones = np.ones((seq, ROPE_GAP))
    zeros = np.zeros((seq, ROPE_GAP))
    cos_tab = np.concatenate([cos, ones, cos, ones], axis=-1)
    sin_tab = np.concatenate([-sin, zeros, sin, zeros], axis=-1)
    q_scale = HEAD_DIM ** -0.5 * math.log2(math.e)
    tabs = np.stack([cos_tab * q_scale, sin_tab * q_scale, cos_tab, sin_tab])
    tabs = tabs.reshape(4, seq // TOKEN_TILE, TOKEN_TILE // dilation, dilation, HEAD_DIM)
    return jnp.asarray(tabs.transpose(0, 1, 3, 2, 4).reshape(4, seq, HEAD_DIM), _F32)


def _inproj(x2d, seq, pre_g, w_in, conv_w, conv_b):
    t, d = x2d.shape
    tiles_per_seq = seq // TOKEN_TILE
    w_cols = N_GROUPS * ATT_WIDTH + 3 * M_WIDTH + LANES
    tables = [_rope_tables(seq, dil) for dil in GROUP_DILATIONS]
    row = lambda width: pl.BlockSpec((TOKEN_TILE, width), lambda i: (i, 0))
    cls = lambda dil: pl.BlockSpec((TOKEN_TILE // dil, dil * ATT_WIDTH), lambda i: (i, 0))
    tab = pl.BlockSpec((4, TOKEN_TILE, HEAD_DIM), lambda i: (0, i % tiles_per_seq, 0))
    out_shape = [jax.ShapeDtypeStruct((t // dil, dil * ATT_WIDTH), _BF16) for dil in GROUP_DILATIONS]
    out_shape += [jax.ShapeDtypeStruct((t, M_WIDTH), _BF16)] * 3
    out_shape.append(jax.ShapeDtypeStruct((t, LANES), _F32))
    return pl.pallas_call(
        functools.partial(_inproj_kernel, tiles_per_seq=tiles_per_seq),
        grid=(t // TOKEN_TILE,),
        in_specs=[row(d), _resident((1, d)), _HBM, tab, tab, tab,
                  _resident((CONV_WIDTH, 2 * M_WIDTH)), _resident((1, 2 * M_WIDTH))],
        out_specs=[cls(dil) for dil in GROUP_DILATIONS] + [row(M_WIDTH)] * 3 + [row(LANES)],
        out_shape=out_shape,
        scratch_shapes=_weight_scratch((w_cols, d)) + [
            pltpu.VMEM((TOKEN_TILE, d), _BF16),
            pltpu.VMEM((d // LANES, TOKEN_TILE, LANES), _F32),
            pltpu.VMEM((d // LANES, TOKEN_TILE, LANES), _F32),
            pltpu.VMEM((TOKEN_TILE, d), _BF16),
            pltpu.VMEM((TOKEN_TILE, d), _BF16),
            pltpu.VMEM((CONV_COLS // LANES, 2 * (CONV_HALO + TOKEN_TILE), LANES), _F32),
            pltpu.VMEM((CONV_HALO, 2 * M_WIDTH), _F32)],
        compiler_params=_params("arbitrary"),
        name="inproj",
    )(x2d, pre_g.reshape(1, d), w_in.T, *tables, 0.5 * conv_w, 0.5 * conv_b.reshape(1, -1))


def _attn_kernel(qkv_ref, bias_two_ref, bias_one_ref, o_ref, lse_ref, *, dilation, n_blocks):
    lane = lax.broadcasted_iota(jnp.int32, (ATT_BLOCK, LANES), 1)

    def block(row0, key0, n_keys, bias_ref):
        ones = jnp.ones((n_keys, HEAD_DIM), _BF16)
        rows = pl.ds(row0, ATT_BLOCK)
        for r in range(dilation):
            scores = []
            for hh in range(GROUP_HEADS):
                qc = r * ATT_WIDTH + hh * HEAD_DIM
                q = qkv_ref[rows, qc:qc + HEAD_DIM]
                k = qkv_ref[pl.ds(key0, n_keys), qc + GROUP_WIDTH:qc + GROUP_WIDTH + HEAD_DIM]
                scores.append(lax.dot_general(q, k, _NT, preferred_element_type=_F32) + bias_ref[...])
            tops = [jnp.max(s, axis=1, keepdims=True) for s in scores]
            probs = [jnp.exp2(s - m).astype(_BF16) for s, m in zip(scores, tops)]
            lse_tile = jnp.zeros((ATT_BLOCK, LANES), _F32)
            for hh in range(GROUP_HEADS):
                vc = r * ATT_WIDTH + 2 * GROUP_WIDTH + hh * HEAD_DIM
                v = qkv_ref[pl.ds(key0, n_keys), vc:vc + HEAD_DIM]
                acc = _dot(probs[hh], jnp.concatenate([v, ones], axis=1))
                denom = acc[:, HEAD_DIM:]
                o = acc[:, :HEAD_DIM] * (1.0 / denom)
                lse_tile = jnp.where(lane == hh, tops[hh] + jnp.log2(denom), lse_tile)
                oc = r * GROUP_WIDTH + hh * HEAD_DIM
                o_ref[rows, oc:oc + HEAD_DIM] = o.astype(_BF16)
            lse_ref[rows, r * LANES:(r + 1) * LANES] = lse_tile * math.log(2.0)

    def two_key_block(i):
        row0 = i * ATT_BLOCK
        if not isinstance(i, int):
            row0 = pl.multiple_of(row0, ATT_BLOCK)
        block(row0, row0 - ATT_BLOCK, 2 * ATT_BLOCK, bias_two_ref)

    block(0, 0, ATT_BLOCK, bias_one_ref)
    per_iter = min(n_blocks, ATT_BLOCKS_PER_ITER)
    for i in range(1, min(per_iter, n_blocks)):
        two_key_block(i)

    def body(j, carry):
        for i in range(per_iter):
            two_key_block(j * per_iter + i)
        return carry

    if n_blocks > per_iter:
        lax.fori_loop(1, n_blocks // per_iter, body, 0)


def _band_bias():
    qi = np.arange(ATT_BLOCK)[:, None]
    kj = np.arange(2 * ATT_BLOCK)[None, :]
    valid_two = np.where(kj < ATT_BLOCK, kj >= qi, kj - ATT_BLOCK <= qi)
    valid_one = np.arange(ATT_BLOCK)[None, :] <= qi
    to_bias = lambda valid: jnp.asarray(np.where(valid, 0.0, MASK_BIAS), _F32)
    return to_bias(valid_two), to_bias(valid_one)


def _attention_group(qkv, batch, seq, dilation):
    length = seq // dilation
    view = qkv.reshape(batch, length, dilation * ATT_WIDTH)
    bias_two, bias_one = _band_bias()
    slab = lambda width: pl.BlockSpec((None, length, dilation * width), lambda b: (b, 0, 0))
    o, lse = pl.pallas_call(
        functools.partial(_attn_kernel, dilation=dilation, n_blocks=length // ATT_BLOCK),
        grid=(batch,),
        in_specs=[slab(ATT_WIDTH), _resident(bias_two.shape), _resident(bias_one.shape)],
        out_specs=[slab(GROUP_WIDTH), slab(LANES)],
        out_shape=[jax.ShapeDtypeStruct((batch, length, dilation * GROUP_WIDTH), _BF16),
                   jax.ShapeDtypeStruct((batch, length, dilation * LANES), _F32)],
        compiler_params=_params("parallel"),
        name=f"attn_d{dilation}",
    )(view, bias_two, bias_one)
    return (o.reshape(batch * length, dilation * GROUP_WIDTH), lse.reshape(batch * length, dilation * LANES))


def _mlstm_kernel(qm_ref, km_ref, vm_ref, gates_ref, gate_b_ref, head_g_ref, o_ref, state_ref, m_ref):
    lc = M_CHUNK
    dh = M_HEAD_DIM
    wide = lambda a, n: jnp.concatenate([a] * n, axis=1)

    @pl.when(pl.program_id(1) == 0)
    def _():
        state_ref[...] = jnp.zeros(state_ref.shape, _F32)
        m_ref[...] = jnp.zeros(m_ref.shape, _F32)

    causal = (lax.broadcasted_iota(jnp.int32, (lc, lc), 1) <= lax.broadcasted_iota(jnp.int32, (lc, lc), 0))
    tri = causal.astype(_BF16)
    lane = lax.broadcasted_iota(jnp.int32, (lc, LANES), 1)
    ones = jnp.ones((lc, LANES), _BF16)

    for bb in range(M_BATCH):
        gt = gates_ref[bb] + gate_b_ref[...]
        lf = (jnp.minimum(gt, 0.0) - jnp.log(1.0 + jnp.exp(-jnp.abs(gt)))) * math.log2(math.e)
        gt = gt * math.log2(math.e)
        hi = lf.astype(_BF16)
        rest = lf - hi.astype(_F32)
        mid = rest.astype(_BF16)
        low = (rest - mid.astype(_F32)).astype(_BF16)
        b_all = _dot(tri, hi) + _dot(tri, mid) + _dot(tri, low)
        ib = jnp.where(lane < M_HEADS, gt, b_all)
        ib_t = ib.T

        for hh in range(M_HEADS):
            hs = slice(hh * dh, (hh + 1) * dh)
            q = qm_ref[bb, :, hs]
            k = km_ref[bb, :, hs]
            v_aug = jnp.concatenate([vm_ref[bb, :, hs], ones], axis=1)
            i_rep = jnp.broadcast_to(ib[:, hh:hh + 1], (lc, LANES))
            b_rep = jnp.broadcast_to(ib[:, M_HEADS + hh:M_HEADS + hh + 1], (lc, LANES))
            u_row = ib_t[hh:hh + 1, :] - ib_t[M_HEADS + hh:M_HEADS + hh + 1, :]
            m_prev = m_ref[bb, hh][0:1, :]
            state = state_ref[bb, hh]

            e = jnp.where(causal, u_row, -jnp.inf)
            big_m = jnp.maximum(m_prev, jnp.broadcast_to(jnp.max(e, axis=1, keepdims=True), (lc, LANES)))
            w = jnp.exp2(e - wide(big_m, lc // LANES)) * lax.dot_general(q, k, _NT, preferred_element_type=_F32)
            inter = jnp.exp2(m_prev - big_m)
            acc = _dot(w.astype(_BF16), v_aug) + wide(inter, dh // LANES + 1) * _dot(q, state.astype(_BF16))
            den = acc[:, dh:]
            inv = 1.0 / jnp.maximum(jnp.abs(den), jnp.exp2(-(b_rep + big_m)))
            h_out = acc[:, :dh] * wide(inv, dh // LANES)

            b_last = b_rep[lc - 1:lc, :]
            g = b_last - b_rep + i_rep
            m_new = jnp.maximum(b_last + m_prev, jnp.max(g, axis=0, keepdims=True))
            a = jnp.exp2(g - m_new)
            decay = jnp.exp2(b_last + m_prev - m_new)
            av = (wide(a, dh // LANES + 1) * v_aug.astype(_F32)).astype(_BF16)
            state_ref[bb, hh] = (wide(decay, dh // LANES + 1) * state
                                 + lax.dot_general(k, av, _TN, preferred_element_type=_F32))
            m_ref[bb, hh] = jnp.broadcast_to(m_new, m_ref.shape[2:])

            o_ref[bb, :, hs] = (_rms(h_out) * head_g_ref[:, hs]).astype(_BF16)


def _mlstm(qm, km, vm, gates, batch, seq, i_bias, f_bias, head_g):
    n_chunks = seq // M_CHUNK
    gate_b = jnp.pad(jnp.concatenate([i_bias, f_bias]), (0, LANES - 2 * M_HEADS)).reshape(1, LANES)
    chunk = lambda width: pl.BlockSpec((M_BATCH, M_CHUNK, width), lambda b, c: (b, c, 0))
    out = pl.pallas_call(
        _mlstm_kernel,
        grid=(batch // M_BATCH, n_chunks),
        in_specs=[chunk(M_WIDTH), chunk(M_WIDTH), chunk(M_WIDTH), chunk(LANES),
                  _resident((1, LANES)), _resident((1, M_WIDTH))],
        out_specs=chunk(M_WIDTH),
        out_shape=jax.ShapeDtypeStruct((batch, seq, M_WIDTH), _BF16),
        scratch_shapes=[pltpu.VMEM((M_BATCH, M_HEADS, M_HEAD_DIM, M_HEAD_DIM + LANES), _F32),
                        pltpu.VMEM((M_BATCH, M_HEADS, 8, LANES), _F32)],
        compiler_params=_params("parallel", "arbitrary"),
        name="mlstm",
    )(qm.reshape(batch, seq, M_WIDTH), km.reshape(batch, seq, M_WIDTH), vm.reshape(batch, seq, M_WIDTH),
      gates.reshape(batch, seq, LANES), gate_b, head_g.reshape(1, -1))
    return out.reshape(batch * seq, M_WIDTH)


def _merge_kernel(x_ref, o0_ref, o1_ref, o2_ref, l0_ref, l1_ref, l2_ref, hm_ref, pre_g_ref, post_g_ref,
                  w_in_hbm, wa_hbm, wm_hbm, wo_hbm, out_ref,
                  wg_ref, wa_ref, wm_ref, wo_ref, stage_ref, sem_ref, o_tok_ref, lse_tok_ref):
    def gate(h, col):
        return jnp.tanh(lax.dot_general(h, wg_ref[col:col + D_MODEL, :], _NT, preferred_element_type=_F32))

    @pl.when(pl.program_id(0) == 0)
    def _():
        o_col0 = N_GROUPS * ATT_WIDTH + 3 * M_WIDTH
        branch_col0 = o_col0 + M_WIDTH + 2 * M_HEADS
        _cast_weight_in(w_in_hbm, wg_ref, stage_ref, sem_ref,
                        _row_jobs([(o_col0, 0, M_WIDTH), (branch_col0, M_WIDTH, 2 * D_MODEL)], STAGE_SHAPE[1]),
                        scale=0.5)
        _cast_weight_in(wa_hbm, wa_ref, stage_ref, sem_ref, scale=0.5)
        _cast_weight_in(wm_hbm, wm_ref, stage_ref, sem_ref, scale=0.25)
        _cast_weight_in(wo_hbm, wo_ref, stage_ref, sem_ref)

    for sub in range(x_ref.shape[0] // TOKEN_TILE):
        tok = slice(sub * TOKEN_TILE, (sub + 1) * TOKEN_TILE)
        x = x_ref[tok, :]
        h = (_rms(x) * pre_g_ref[...]).astype(_BF16)
        gate_o = gate(h, 0)
        gate_a = gate(h, M_WIDTH)
        gate_m = gate(h, M_WIDTH + D_MODEL)
        for g, (o_ref, l_ref, d) in enumerate(zip((o0_ref, o1_ref, o2_ref), (l0_ref, l1_ref, l2_ref),
                                                   GROUP_DILATIONS)):
            rows = TOKEN_TILE // d
            cls_rows = slice(sub * rows, (sub + 1) * rows)
            for r in range(d):
                where = pl.ds(r, rows, stride=d) if d > 1 else slice(None)
                lse_tok_ref[g, where, :] = l_ref[cls_rows, r * LANES:(r + 1) * LANES]
                for hh in range(GROUP_HEADS):
                    col = r * GROUP_WIDTH + hh * HEAD_DIM
                    o_tok_ref[g * GROUP_HEADS + hh, where, :] = o_ref[cls_rows, col:col + HEAD_DIM].astype(_F32)
        lses = [lse_tok_ref[g] for g in range(N_GROUPS)]
        top = jnp.maximum(jnp.maximum(lses[0], lses[1]), lses[2])
        es = [jnp.exp(l - top) for l in lses]
        inv = 1.0 / (es[0] + es[1] + es[2])
        alphas = [e * inv for e in es]
        heads = []
        for hh in range(GROUP_HEADS):
            acc = None
            for g, alpha in enumerate(alphas):
                term = alpha[:, hh:hh + 1] * o_tok_ref[g * GROUP_HEADS + hh]
                acc = term if acc is None else acc + term
            heads.append(acc.astype(_BF16))
        att = jnp.concatenate(heads, axis=1)
        half_a = _dot(att, wa_ref[...])
        a = half_a + gate_a * half_a
        hm = hm_ref[tok, :].astype(_F32)
        ml = (hm + gate_o * hm).astype(_BF16)
        half_m = _dot(ml, wm_ref[...])
        m = half_m + gate_m * half_m
        y = _dot((a + m).astype(_BF16), wo_ref[...])
        out_ref[tok, :] = x + _rms(y) * post_g_ref[...]


def _merge(x2d, outs, lses, hm, pre_g, w_in, w_att, w_ml, w_out, post_g):
    t, d = x2d.shape
    step = MERGE_SUBTILES * TOKEN_TILE
    row = lambda width: pl.BlockSpec((step, width), lambda i: (i, 0))
    cls = lambda width: [pl.BlockSpec((step // dil, dil * width), lambda i: (i, 0)) for dil in GROUP_DILATIONS]
    return pl.pallas_call(
        _merge_kernel,
        grid=(t // step,),
        in_specs=[row(d)] + cls(GROUP_WIDTH) + cls(LANES) + [row(M_WIDTH), _resident((1, d)),
                  _resident((1, d))] + [_HBM] * 4,
        out_specs=row(d),
        out_shape=jax.ShapeDtypeStruct((t, d), _F32),
        scratch_shapes=_weight_scratch((M_WIDTH + 2 * D_MODEL, d), w_att.shape, w_ml.shape, w_out.shape) + [
            pltpu.VMEM((N_GROUPS * GROUP_HEADS, TOKEN_TILE, HEAD_DIM), _F32),
            pltpu.VMEM((N_GROUPS, TOKEN_TILE, LANES), _F32)],
        compiler_params=_params("arbitrary"),
        name="merge",
    )(x2d, *outs, *lses, hm, pre_g.reshape(1, d), post_g.reshape(1, d), w_in.T, w_att, w_ml, w_out)


def kernel(x, ffn1_pre_g, ffn1_w_gate, ffn1_w_up, ffn1_w_down, ffn1_post_g, mix_pre_g, w_in, conv_w, conv_b, mlstm_i_bias, mlstm_f_bias, mlstm_head_g, w_att_branch, w_mlstm_branch, w_out, mix_post_g, ffn2_pre_g, ffn2_w_gate, ffn2_w_up, ffn2_w_down, ffn2_post_g):
    batch, seq, d = x.shape
    t = batch * seq
    xt = x.reshape(t, d)
    for l in range(ffn1_pre_g.shape[0]):
        xt = _ffn(xt, ffn1_pre_g[l], ffn1_w_gate[l], ffn1_w_up[l], ffn1_w_down[l], ffn1_post_g[l])
        qkv0, qkv1, qkv2, qm, km, vm, gates = _inproj(xt, seq, mix_pre_g[l], w_in[l], conv_w[l], conv_b[l])
        outs, lses = zip(*(_attention_group(qkv, batch, seq, dil)
                           for qkv, dil in zip((qkv0, qkv1, qkv2), GROUP_DILATIONS)))
        hm = _mlstm(qm, km, vm, gates, batch, seq, mlstm_i_bias[l], mlstm_f_bias[l], mlstm_head_g[l])
        xt = _merge(xt, outs, lses, hm, mix_pre_g[l], w_in[l], w_att_branch[l], w_mlstm_branch[l], w_out[l],
                    mix_post_g[l])
        xt = _ffn(xt, ffn2_pre_g[l], ffn2_w_gate[l], ffn2_w_up[l], ffn2_w_down[l], ffn2_post_g[l])
    return xt.reshape(batch, seq, d)
```

```python
import functools
import math

import numpy as np
import jax
import jax.numpy as jnp
from jax import lax
from jax.experimental import pallas as pl
from jax.experimental.pallas import tpu as pltpu

D_MODEL = 1024
N_GROUPS = 3
GROUP_DILATIONS = (1, 4, 16)
ATT_SPAN = 128
GROUP_HEADS = 4
HEAD_DIM = 128
GROUP_WIDTH = GROUP_HEADS * HEAD_DIM
ATT_WIDTH = N_GROUPS * GROUP_WIDTH
ATT_BLOCK = 128
ROPE_THETA = 500000.0
ROPE_DIMS = HEAD_DIM // 4
ROPE_HALF = ROPE_DIMS // 2
ROPE_GAP = HEAD_DIM // 2 - ROPE_HALF
M_HEADS = 4
M_WIDTH = D_MODEL
M_HEAD_DIM = M_WIDTH // M_HEADS
M_CHUNK = 256
ATT_BLOCKS_PER_ITER = 16
M_BATCH = 4
CONV_WIDTH = 4
CONV_HALO = 8
CONV_COLS = 256
RMS_EPS = 1e-6
LANES = 128
MASK_BIAS = -1e30
VMEM_LIMIT_BYTES = 58 * 1024 * 1024
TOKEN_TILE = 512
MERGE_SUBTILES = 2
FFN_SUBTILES = 2
STAGE_SHAPE = (4, 256, 1024)

_BF16 = jnp.bfloat16
_F32 = jnp.float32
_NT = (((1,), (1,)), ((), ()))
_TN = (((0,), (0,)), ((), ()))


def _rms(x):
    return x * lax.rsqrt(jnp.mean(x * x, axis=-1, keepdims=True) + RMS_EPS)


def _dot(a, b):
    return jnp.dot(a, b, preferred_element_type=_F32)


def _resident(shape):
    return pl.BlockSpec(shape, lambda *_: (0,) * len(shape), pipeline_mode=pl.Buffered(1))


def _params(*semantics):
    return pltpu.CompilerParams(dimension_semantics=semantics, vmem_limit_bytes=VMEM_LIMIT_BYTES)


def _cast_weight_in(src_hbm, dst_ref, stage_ref, sem_ref, jobs=None, scale=None):
    slots, stage_rows, stage_cols = stage_ref.shape
    if jobs is None:
        rows, cols = dst_ref.shape
        jobs = [(r, c, r, c, min(stage_rows, rows - r), min(stage_cols, cols - c))
                for r in range(0, rows, stage_rows) for c in range(0, cols, stage_cols)]

    def copy(k):
        src_row, src_col, _, _, nr, nc = jobs[k]
        return pltpu.make_async_copy(src_hbm.at[pl.ds(src_row, nr), pl.ds(src_col, nc)],
                                     stage_ref.at[k % slots, pl.ds(0, nr), pl.ds(0, nc)], sem_ref.at[k % slots])

    for k in range(min(slots - 1, len(jobs))):
        copy(k).start()
    for k, (_, _, dst_row, dst_col, nr, nc) in enumerate(jobs):
        if k + slots - 1 < len(jobs):
            copy(k + slots - 1).start()
        copy(k).wait()
        block = stage_ref[k % slots, 0:nr, 0:nc]
        dst_ref[dst_row:dst_row + nr, dst_col:dst_col + nc] = (block if scale is None else block * scale).astype(_BF16)


def _row_jobs(windows, stage_rows):
    return [(src + r, 0, dst + r, 0, min(stage_rows, n - r), STAGE_SHAPE[2])
            for src, dst, n in windows for r in range(0, n, stage_rows)]


def _ffn_kernel(x_ref, pre_g_ref, post_g_ref, wg_hbm, wu_hbm, wd_hbm, o_ref,
                wg_ref, wu_ref, wd_ref, stage_ref, sem_ref):
    @pl.when(pl.program_id(0) == 0)
    def _():
        _cast_weight_in(wg_hbm, wg_ref, stage_ref, sem_ref, scale=0.5)
        _cast_weight_in(wu_hbm, wu_ref, stage_ref, sem_ref)
        _cast_weight_in(wd_hbm, wd_ref, stage_ref, sem_ref)

    for sub in range(x_ref.shape[0] // TOKEN_TILE):
        rows = slice(sub * TOKEN_TILE, (sub + 1) * TOKEN_TILE)
        x = x_ref[rows, :]
        h = (_rms(x) * pre_g_ref[...]).astype(_BF16)
        g = _dot(h, wg_ref[...])
        u = _dot(h, wu_ref[...])
        a = ((g + g * jnp.tanh(g)) * u).astype(_BF16)
        f = _dot(a, wd_ref[...])
        o_ref[rows, :] = x + _rms(f) * post_g_ref[...]


def _weight_scratch(*shapes):
    return ([pltpu.VMEM(shape, _BF16) for shape in shapes]
            + [pltpu.VMEM(STAGE_SHAPE, _F32), pltpu.SemaphoreType.DMA((STAGE_SHAPE[0],))])


_HBM = pl.BlockSpec(memory_space=pl.ANY)


def _ffn(x2d, pre_g, w_gate, w_up, w_down, post_g):
    t, d = x2d.shape
    row = pl.BlockSpec((FFN_SUBTILES * TOKEN_TILE, d), lambda i: (i, 0))
    return pl.pallas_call(
        _ffn_kernel,
        grid=(t // (FFN_SUBTILES * TOKEN_TILE),),
        in_specs=[row, _resident((1, d)), _resident((1, d)), _HBM, _HBM, _HBM],
        out_specs=row,
        out_shape=jax.ShapeDtypeStruct((t, d), _F32),
        scratch_shapes=_weight_scratch(w_gate.shape, w_up.shape, w_down.shape),
        compiler_params=_params("arbitrary"),
        name="ffn",
    )(x2d, pre_g.reshape(1, d), 0.5 * post_g.reshape(1, d), w_gate, w_up, w_down)


def _inproj_kernel(x_ref, g_ref, w_hbm, tab0_ref, tab1_ref, tab2_ref, conv_w_ref, conv_b_ref,
                   qkv0_ref, qkv1_ref, qkv2_ref, qm_ref, km_ref, vm_ref, gates_ref,
                   w_ref, stage_ref, sem_ref, hn_ref, hs_ref, hs4_ref, hp1_ref, hp2_ref, xs_ref, halo_ref, *, tiles_per_seq):
    tm = TOKEN_TILE
    n_slabs = D_MODEL // LANES
    conv_col0 = N_GROUPS * ATT_WIDTH
    vm_col0 = conv_col0 + 2 * M_WIDTH
    gate_col0 = vm_col0 + M_WIDTH

    def project(h_ref, col, width):
        return lax.dot_general(h_ref[...], w_ref[col:col + width, :], _NT, preferred_element_type=_F32)

    @pl.when(pl.program_id(0) == 0)
    def _():
        w_ref[gate_col0 + 2 * M_HEADS:gate_col0 + LANES, :] = jnp.zeros((LANES - 2 * M_HEADS, D_MODEL), _BF16)
        _cast_weight_in(w_hbm, w_ref, stage_ref, sem_ref, _row_jobs(_inproj_windows(), STAGE_SHAPE[1]))
        for g in range(N_GROUPS):
            for part in range(2):
                for hh in range(GROUP_HEADS):
                    base = (g * 3 + part) * GROUP_WIDTH + hh * HEAD_DIM
                    head = w_ref[base:base + HEAD_DIM, :]
                    w_ref[base:base + HEAD_DIM, :] = jnp.concatenate(
                        [head[:ROPE_HALF], head[ROPE_DIMS:ROPE_DIMS + ROPE_GAP], head[ROPE_HALF:ROPE_DIMS],
                         head[ROPE_DIMS + ROPE_GAP:]], axis=0)

    @pl.when(pl.program_id(0) % tiles_per_seq == 0)
    def _():
        halo_ref[...] = jnp.zeros(halo_ref.shape, _F32)

    hf = _rms(x_ref[...]) * g_ref[...]
    hn_ref[...] = hf.astype(_BF16)
    for c in range(n_slabs):
        hs_ref[c] = hf[:, c * LANES:(c + 1) * LANES]


    def regroup(hp_ref, d):
        rows = tm // d
        quarter = tm // GROUP_DILATIONS[1]
        for r in range(d):
            if d == GROUP_DILATIONS[1]:
                pieces = [hs_ref[c, pl.ds(r, rows, stride=d), :] for c in range(n_slabs)]
                for c in range(n_slabs):
                    hs4_ref[c, r * rows:(r + 1) * rows, :] = pieces[c]
            else:
                start = (r % GROUP_DILATIONS[1]) * quarter + r // GROUP_DILATIONS[1]
                pieces = [hs4_ref[c, pl.ds(start, rows, stride=GROUP_DILATIONS[1]), :] for c in range(n_slabs)]
            hp_ref[r * rows:(r + 1) * rows, :] = jnp.concatenate(pieces, axis=1).astype(_BF16)

    def attention_part(g, part, h_ref):
        return project(h_ref, (g * 3 + part) * GROUP_WIDTH, GROUP_WIDTH)

    def attention_store(p, part, qkv_ref, tab_ref, d):
        rows = tm // d
        for hh in range(GROUP_HEADS):
            x = p[:, hh * HEAD_DIM:(hh + 1) * HEAD_DIM]
            if part < 2:
                x = x * tab_ref[2 * part] + pltpu.roll(x, HEAD_DIM // 2, 1) * tab_ref[2 * part + 1]
            x = x.astype(_BF16)
            for r in range(d):
                lo = r * ATT_WIDTH + part * GROUP_WIDTH + hh * HEAD_DIM
                qkv_ref[:, lo:lo + HEAD_DIM] = x[r * rows:(r + 1) * rows, :]

    def conv_dot(cc):
        return project(hn_ref, conv_col0 + cc * CONV_COLS, CONV_COLS)

    def conv_store(p, cc):
        for c in range(CONV_COLS // LANES):
            lanes = slice(cc * CONV_COLS + c * LANES, cc * CONV_COLS + (c + 1) * LANES)
            pc = p[:, c * LANES:(c + 1) * LANES]
            xs_ref[c, pl.ds(0, CONV_HALO, stride=2), :] = halo_ref[:, lanes]
            xs_ref[c, pl.ds(2 * CONV_HALO, tm, stride=2), :] = pc
            halo_ref[:, lanes] = pc[tm - CONV_HALO:tm, :]
            y = conv_b_ref[:, lanes] + conv_w_ref[CONV_WIDTH - 1:CONV_WIDTH, lanes] * pc
            for j in range(CONV_WIDTH - 1):
                shift = CONV_WIDTH - 1 - j
                y = y + conv_w_ref[j:j + 1, lanes] * xs_ref[c, pl.ds(2 * (CONV_HALO - shift), tm, stride=2), :]
            qk = y + y * jnp.tanh(y)
            if lanes.start < M_WIDTH:
                qm_ref[:, lanes] = qk.astype(_BF16)
            else:
                km_ref[:, lanes.start - M_WIDTH:lanes.stop - M_WIDTH] = (qk * M_HEAD_DIM ** -0.5).astype(_BF16)

    def vm_store(p):
        vm_ref[...] = p.astype(_BF16)

    def gates_store(p):
        gates_ref[...] = p

    hps = (hn_ref, hp1_ref, hp2_ref)
    qkvs = (qkv0_ref, qkv1_ref, qkv2_ref)
    tabs = (tab0_ref, tab1_ref, tab2_ref)
    attention = [(functools.partial(attention_part, g, part, hps[g]),
                  functools.partial(attention_store, part=part, qkv_ref=qkvs[g], tab_ref=tabs[g], d=GROUP_DILATIONS[g]))
                 for g in range(N_GROUPS) for part in range(3)]
    convs = [(functools.partial(conv_dot, cc), functools.partial(conv_store, cc=cc))
             for cc in range(2 * M_WIDTH // CONV_COLS)]
    sections = [(lambda: project(hn_ref, vm_col0, M_WIDTH), vm_store),
                (lambda: project(hn_ref, gate_col0, LANES),
                 lambda p: (gates_store(p), regroup(hp1_ref, GROUP_DILATIONS[1]))),
                attention[0], convs[0], attention[1], convs[1], attention[2],
                (convs[2][0], lambda p: (conv_store(p, 2), regroup(hp2_ref, GROUP_DILATIONS[2]))),
                attention[3], convs[3], attention[4], convs[4], attention[5], convs[5],
                attention[6], convs[6], attention[7], convs[7], attention[8]]
    pending = None
    for matmul, epilogue in sections:
        p = matmul()
        if pending is not None:
            pending()
        pending = functools.partial(epilogue, p)
    pending()


def _inproj_windows():
    a, m = ATT_WIDTH, M_WIDTH
    wins = []
    for g in range(N_GROUPS):
        for part in range(3):
            wins.append((part * a + g * GROUP_WIDTH, (g * 3 + part) * GROUP_WIDTH, GROUP_WIDTH))
    base = 3 * a
    wins.append((base, base, 3 * m))
    wins.append((base + 4 * m, base + 3 * m, 2 * M_HEADS))
    return wins


def _rope_tables(seq, dilation):
    inv_freq = np.power(ROPE_THETA, -(np.arange(ROPE_HALF, dtype=np.float64) * 2.0 / ROPE_DIMS))
    ang = np.arange(seq, dtype=np.float64)[:, None] * inv_freq[None, :]
    cos, sin = np.cos(ang), np.sin(ang)
    ones = np.ones((seq, ROPE_GAP))
    zeros = np.zeros((seq, ROPE_GAP))
    cos_tab = np.concatenate([cos, ones, cos, ones], axis=-1)
    sin_tab = np.concatenate([-sin, zeros, sin, zeros], axis=-1)
    q_scale = HEAD_DIM ** -0.5 * math.log2(math.e)
    tabs = np.stack([cos_tab * q_scale, sin_tab * q_scale, cos_tab, sin_tab])
    tabs = tabs.reshape(4, seq // TOKEN_TILE, TOKEN_TILE // dilation, dilation, HEAD_DIM)
    return jnp.asarray(tabs.transpose(0, 1, 3, 2, 4).reshape(4, seq, HEAD_DIM), _F32)


def _inproj(x2d, seq, pre_g, w_in, conv_w, conv_b):
    t, d = x2d.shape
    tiles_per_seq = seq // TOKEN_TILE
    w_cols = N_GROUPS * ATT_WIDTH + 3 * M_WIDTH + LANES
    tables = [_rope_tables(seq, dil) for dil in GROUP_DILATIONS]
    row = lambda width: pl.BlockSpec((TOKEN_TILE, width), lambda i: (i, 0))
    cls = lambda dil: pl.BlockSpec((TOKEN_TILE // dil, dil * ATT_WIDTH), lambda i: (i, 0))
    tab = pl.BlockSpec((4, TOKEN_TILE, HEAD_DIM), lambda i: (0, i % tiles_per_seq, 0))
    out_shape = [jax.ShapeDtypeStruct((t // dil, dil * ATT_WIDTH), _BF16) for dil in GROUP_DILATIONS]
    out_shape += [jax.ShapeDtypeStruct((t, M_WIDTH), _BF16)] * 3
    out_shape.append(jax.ShapeDtypeStruct((t, LANES), _F32))
    return pl.pallas_call(
        functools.partial(_inproj_kernel, tiles_per_seq=tiles_per_seq),
        grid=(t // TOKEN_TILE,),
        in_specs=[row(d), _resident((1, d)), _HBM, tab, tab, tab,
                  _resident((CONV_WIDTH, 2 * M_WIDTH)), _resident((1, 2 * M_WIDTH))],
        out_specs=[cls(dil) for dil in GROUP_DILATIONS] + [row(M_WIDTH)] * 3 + [row(LANES)],
        out_shape=out_shape,
        scratch_shapes=_weight_scratch((w_cols, d)) + [
            pltpu.VMEM((TOKEN_TILE, d), _BF16),
            pltpu.VMEM((d // LANES, TOKEN_TILE, LANES), _F32),
            pltpu.VMEM((d // LANES, TOKEN_TILE, LANES), _F32),
            pltpu.VMEM((TOKEN_TILE, d), _BF16),
            pltpu.VMEM((TOKEN_TILE, d), _BF16),
            pltpu.VMEM((CONV_COLS // LANES, 2 * (CONV_HALO + TOKEN_TILE), LANES), _F32),
            pltpu.VMEM((CONV_HALO, 2 * M_WIDTH), _F32)],
        compiler_params=_params("arbitrary"),
        name="inproj",
    )(x2d, pre_g.reshape(1, d), w_in.T, *tables, 0.5 * conv_w, 0.5 * conv_b.reshape(1, -1))


def _attn_kernel(qkv_ref, bias_two_ref, bias_one_ref, o_ref, lse_ref, *, dilation, n_blocks):
    lane = lax.broadcasted_iota(jnp.int32, (ATT_BLOCK, LANES), 1)

    def block(row0, key0, n_keys, bias_ref):
        ones = jnp.ones((n_keys, HEAD_DIM), _BF16)
        rows = pl.ds(row0, ATT_BLOCK)
        for r in range(dilation):
            scores = []
            for hh in range(GROUP_HEADS):
                qc = r * ATT_WIDTH + hh * HEAD_DIM
                q = qkv_ref[rows, qc:qc + HEAD_DIM]
                k = qkv_ref[pl.ds(key0, n_keys), qc + GROUP_WIDTH:qc + GROUP_WIDTH + HEAD_DIM]
                scores.append(lax.dot_general(q, k, _NT, preferred_element_type=_F32) + bias_ref[...])
            tops = [jnp.max(s, axis=1, keepdims=True) for s in scores]
            probs = [jnp.exp2(s - m).astype(_BF16) for s, m in zip(scores, tops)]
            lse_tile = jnp.zeros((ATT_BLOCK, LANES), _F32)
            for hh in range(GROUP_HEADS):
                vc = r * ATT_WIDTH + 2 * GROUP_WIDTH + hh * HEAD_DIM
                v = qkv_ref[pl.ds(key0, n_keys), vc:vc + HEAD_DIM]
                acc = _dot(probs[hh], jnp.concatenate([v, ones], axis=1))
                denom = acc[:, HEAD_DIM:]
                o = acc[:, :HEAD_DIM] * (1.0 / denom)
                lse_tile = jnp.where(lane == hh, tops[hh] + jnp.log2(denom), lse_tile)
                oc = r * GROUP_WIDTH + hh * HEAD_DIM
                o_ref[rows, oc:oc + HEAD_DIM] = o.astype(_BF16)
            lse_ref[rows, r * LANES:(r + 1) * LANES] = lse_tile * math.log(2.0)

    def two_key_block(i):
        row0 = i * ATT_BLOCK
        if not isinstance(i, int):
            row0 = pl.multiple_of(row0, ATT_BLOCK)
        block(row0, row0 - ATT_BLOCK, 2 * ATT_BLOCK, bias_two_ref)

    block(0, 0, ATT_BLOCK, bias_one_ref)
    per_iter = min(n_blocks, ATT_BLOCKS_PER_ITER)
    for i in range(1, min(per_iter, n_blocks)):
        two_key_block(i)

    def body(j, carry):
        for i in range(per_iter):
            two_key_block(j * per_iter + i)
        return carry

    if n_blocks > per_iter:
        lax.fori_loop(1, n_blocks // per_iter, body, 0)


def _band_bias():
    qi = np.arange(ATT_BLOCK)[:, None]
    kj = np.arange(2 * ATT_BLOCK)[None, :]
    valid_two = np.where(kj < ATT_BLOCK, kj >= qi, kj - ATT_BLOCK <= qi)
    valid_one = np.arange(ATT_BLOCK)[None, :] <= qi
    to_bias = lambda valid: jnp.asarray(np.where(valid, 0.0, MASK_BIAS), _F32)
    return to_bias(valid_two), to_bias(valid_one)


def _attention_group(qkv, batch, seq, dilation):
    length = seq // dilation
    view = qkv.reshape(batch, length, dilation * ATT_WIDTH)
    bias_two, bias_one = _band_bias()
    slab = lambda width: pl.BlockSpec((None, length, dilation * width), lambda b: (b, 0, 0))
    o, lse = pl.pallas_call(
        functools.partial(_attn_kernel, dilation=dilation, n_blocks=length // ATT_BLOCK),
        grid=(batch,),
        in_specs=[slab(ATT_WIDTH), _resident(bias_two.shape), _resident(bias_one.shape)],
        out_specs=[slab(GROUP_WIDTH), slab(LANES)],
        out_shape=[jax.ShapeDtypeStruct((batch, length, dilation * GROUP_WIDTH), _BF16),
                   jax.ShapeDtypeStruct((batch, length, dilation * LANES), _F32)],
        compiler_params=_params("parallel"),
        name=f"attn_d{dilation}",
    )(view, bias_two, bias_one)
    return (o.reshape(batch * length, dilation * GROUP_WIDTH), lse.reshape(batch * length, dilation * LANES))


def _mlstm_kernel(qm_ref, km_ref, vm_ref, gates_ref, gate_b_ref, head_g_ref, o_ref, state_ref, m_ref):
    lc = M_CHUNK
    dh = M_HEAD_DIM
    wide = lambda a, n: jnp.concatenate([a] * n, axis=1)

    @pl.when(pl.program_id(1) == 0)
    def _():
        state_ref[...] = jnp.zeros(state_ref.shape, _F32)
        m_ref[...] = jnp.zeros(m_ref.shape, _F32)

    causal = (lax.broadcasted_iota(jnp.int32, (lc, lc), 1) <= lax.broadcasted_iota(jnp.int32, (lc, lc), 0))
    tri_t = (lax.broadcasted_iota(jnp.int32, (lc, lc), 0) <= lax.broadcasted_iota(jnp.int32, (lc, lc), 1)).astype(_BF16)
    ones = jnp.ones((lc, LANES), _BF16)

    for bb in range(M_BATCH):
        gt = (gates_ref[bb] + gate_b_ref[...]).T[0:2 * M_HEADS, :]
        lf = (jnp.minimum(gt, 0.0) - jnp.log(1.0 + jnp.exp(-jnp.abs(gt)))) * math.log2(math.e)
        gt = gt * math.log2(math.e)
        hi = lf.astype(_BF16)
        rest = lf - hi.astype(_F32)
        mid = rest.astype(_BF16)
        low = (rest - mid.astype(_F32)).astype(_BF16)
        b_all = _dot(hi, tri_t) + _dot(mid, tri_t) + _dot(low, tri_t)
        gate_row = lax.broadcasted_iota(jnp.int32, (2 * M_HEADS, lc), 0)
        ib_t = jnp.where(gate_row < M_HEADS, gt, b_all)
        ib = jnp.concatenate([ib_t, jnp.zeros((LANES - 2 * M_HEADS, lc), _F32)], axis=0).T

        for hh in range(M_HEADS):
            hs = slice(hh * dh, (hh + 1) * dh)
            q = qm_ref[bb, :, hs]
            k = km_ref[bb, :, hs]
            v_aug = jnp.concatenate([vm_ref[bb, :, hs], ones], axis=1)
            i_rep = jnp.broadcast_to(ib[:, hh:hh + 1], (lc, LANES))
            b_rep = jnp.broadcast_to(ib[:, M_HEADS + hh:M_HEADS + hh + 1], (lc, LANES))
            u_row = ib_t[hh:hh + 1, :] - ib_t[M_HEADS + hh:M_HEADS + hh + 1, :]
            m_prev = m_ref[bb, hh][0:1, :]
            state = state_ref[bb, hh]

            e = jnp.where(causal, u_row, -jnp.inf)
            big_m = jnp.maximum(m_prev, jnp.broadcast_to(jnp.max(e, axis=1, keepdims=True), (lc, LANES)))
            w = jnp.exp2(e - wide(big_m, lc // LANES)) * lax.dot_general(q, k, _NT, preferred_element_type=_F32)
            inter = jnp.exp2(m_prev - big_m)
            acc = _dot(w.astype(_BF16), v_aug) + wide(inter, dh // LANES + 1) * _dot(q, state.astype(_BF16))
            den = acc[:, dh:]
            inv = 1.0 / jnp.maximum(jnp.abs(den), jnp.exp2(-(b_rep + big_m)))
            h_out = acc[:, :dh] * wide(inv, dh // LANES)

            b_last = b_rep[lc - 1:lc, :]
            g = b_last - b_rep + i_rep
            m_new = jnp.maximum(b_last + m_prev, jnp.max(g, axis=0, keepdims=True))
            a = jnp.exp2(g - m_new)
            decay = jnp.exp2(b_last + m_prev - m_new)
            av = (wide(a, dh // LANES + 1) * v_aug.astype(_F32)).astype(_BF16)
            state_ref[bb, hh] = (wide(decay, dh // LANES + 1) * state
                                 + lax.dot_general(k, av, _TN, preferred_element_type=_F32))
            m_ref[bb, hh] = jnp.broadcast_to(m_new, m_ref.shape[2:])

            o_ref[bb, :, hs] = (_rms(h_out) * head_g_ref[:, hs]).astype(_BF16)


def _mlstm(qm, km, vm, gates, batch, seq, i_bias, f_bias, head_g):
    n_chunks = seq // M_CHUNK
    gate_b = jnp.pad(jnp.concatenate([i_bias, f_bias]), (0, LANES - 2 * M_HEADS)).reshape(1, LANES)
    chunk = lambda width: pl.BlockSpec((M_BATCH, M_CHUNK, width), lambda b, c: (b, c, 0))
    out = pl.pallas_call(
        _mlstm_kernel,
        grid=(batch // M_BATCH, n_chunks),
        in_specs=[chunk(M_WIDTH), chunk(M_WIDTH), chunk(M_WIDTH), chunk(LANES),
                  _resident((1, LANES)), _resident((1, M_WIDTH))],
        out_specs=chunk(M_WIDTH),
        out_shape=jax.ShapeDtypeStruct((batch, seq, M_WIDTH), _BF16),
        scratch_shapes=[pltpu.VMEM((M_BATCH, M_HEADS, M_HEAD_DIM, M_HEAD_DIM + LANES), _F32),
                        pltpu.VMEM((M_BATCH, M_HEADS, 8, LANES), _F32)],
        compiler_params=_params("parallel", "arbitrary"),
        name="mlstm",
    )(qm.reshape(batch, seq, M_WIDTH), km.reshape(batch, seq, M_WIDTH), vm.reshape(batch, seq, M_WIDTH),
      gates.reshape(batch, seq, LANES), gate_b, head_g.reshape(1, -1))
    return out.reshape(batch * seq, M_WIDTH)


def _merge_kernel(x_ref, o0_ref, o1_ref, o2_ref, l0_ref, l1_ref, l2_ref, hm_ref, pre_g_ref, post_g_ref,
                  w_in_hbm, wa_hbm, wm_hbm, wo_hbm, out_ref,
                  wg_ref, wa_ref, wm_ref, wo_ref, stage_ref, sem_ref, o_tok_ref, lse_tok_ref):
    def gate(h, col):
        return jnp.tanh(lax.dot_general(h, wg_ref[col:col + D_MODEL, :], _NT, preferred_element_type=_F32))

    @pl.when(pl.program_id(0) == 0)
    def _():
        o_col0 = N_GROUPS * ATT_WIDTH + 3 * M_WIDTH
        branch_col0 = o_col0 + M_WIDTH + 2 * M_HEADS
        _cast_weight_in(w_in_hbm, wg_ref, stage_ref, sem_ref,
                        _row_jobs([(o_col0, 0, M_WIDTH), (branch_col0, M_WIDTH, 2 * D_MODEL)], STAGE_SHAPE[1]),
                        scale=0.5)
        _cast_weight_in(wa_hbm, wa_ref, stage_ref, sem_ref, scale=0.5)
        _cast_weight_in(wm_hbm, wm_ref, stage_ref, sem_ref, scale=0.25)
        _cast_weight_in(wo_hbm, wo_ref, stage_ref, sem_ref)

    for sub in range(x_ref.shape[0] // TOKEN_TILE):
        tok = slice(sub * TOKEN_TILE, (sub + 1) * TOKEN_TILE)
        x = x_ref[tok, :]
        h = (_rms(x) * pre_g_ref[...]).astype(_BF16)
        gate_o = gate(h, 0)
        gate_a = gate(h, M_WIDTH)
        gate_m = gate(h, M_WIDTH + D_MODEL)
        for g, (o_ref, l_ref, d) in enumerate(zip((o0_ref, o1_ref, o2_ref), (l0_ref, l1_ref, l2_ref),
                                                   GROUP_DILATIONS)):
            rows = TOKEN_TILE // d
            cls_rows = slice(sub * rows, (sub + 1) * rows)
            for r in range(d):
                where = pl.ds(r, rows, stride=d) if d > 1 else slice(None)
                lse_tok_ref[g, where, :] = l_ref[cls_rows, r * LANES:(r + 1) * LANES]
                for hh in range(GROUP_HEADS):
                    col = r * GROUP_WIDTH + hh * HEAD_DIM
                    o_tok_ref[g * GROUP_HEADS + hh, where, :] = o_ref[cls_rows, col:col + HEAD_DIM].astype(_F32)
        lses = [lse_tok_ref[g] for g in range(N_GROUPS)]
        top = jnp.maximum(jnp.maximum(lses[0], lses[1]), lses[2])
        es = [jnp.exp(l - top) for l in lses]
        inv = 1.0 / (es[0] + es[1] + es[2])
        alphas = [e * inv for e in es]
        heads = []
        for hh in range(GROUP_HEADS):
            acc = None
            for g, alpha in enumerate(alphas):
                term = alpha[:, hh:hh + 1] * o_tok_ref[g * GROUP_HEADS + hh]
                acc = term if acc is None else acc + term
            heads.append(acc.astype(_BF16))
        att = jnp.concatenate(heads, axis=1)
        half_a = _dot(att, wa_ref[...])
        a = half_a + gate_a * half_a
        hm = hm_ref[tok, :].astype(_F32)
        ml = (hm + gate_o * hm).astype(_BF16)
        half_m = _dot(ml, wm_ref[...])
        m = half_m + gate_m * half_m
        y = _dot((a + m).astype(_BF16), wo_ref[...])
        out_ref[tok, :] = x + _rms(y) * post_g_ref[...]


def _merge(x2d, outs, lses, hm, pre_g, w_in, w_att, w_ml, w_out, post_g):
    t, d = x2d.shape
    step = MERGE_SUBTILES * TOKEN_TILE
    row = lambda width: pl.BlockSpec((step, width), lambda i: (i, 0))
    cls = lambda width: [pl.BlockSpec((step // dil, dil * width), lambda i: (i, 0)) for dil in GROUP_DILATIONS]
    return pl.pallas_call(
        _merge_kernel,
        grid=(t // step,),
        in_specs=[row(d)] + cls(GROUP_WIDTH) + cls(LANES) + [row(M_WIDTH), _resident((1, d)),
                  _resident((1, d))] + [_HBM] * 4,
        out_specs=row(d),
        out_shape=jax.ShapeDtypeStruct((t, d), _F32),
        scratch_shapes=_weight_scratch((M_WIDTH + 2 * D_MODEL, d), w_att.shape, w_ml.shape, w_out.shape) + [
            pltpu.VMEM((N_GROUPS * GROUP_HEADS, TOKEN_TILE, HEAD_DIM), _F32),
            pltpu.VMEM((N_GROUPS, TOKEN_TILE, LANES), _F32)],
        compiler_params=_params("arbitrary"),
        name="merge",
    )(x2d, *outs, *lses, hm, pre_g.reshape(1, d), post_g.reshape(1, d), w_in.T, w_att, w_ml, w_out)


def kernel(x, ffn1_pre_g, ffn1_w_gate, ffn1_w_up, ffn1_w_down, ffn1_post_g, mix_pre_g, w_in, conv_w, conv_b, mlstm_i_bias, mlstm_f_bias, mlstm_head_g, w_att_branch, w_mlstm_branch, w_out, mix_post_g, ffn2_pre_g, ffn2_w_gate, ffn2_w_up, ffn2_w_down, ffn2_post_g):
    batch, seq, d = x.shape
    t = batch * seq
    xt = x.reshape(t, d)
    for l in range(ffn1_pre_g.shape[0]):
        xt = _ffn(xt, ffn1_pre_g[l], ffn1_w_gate[l], ffn1_w_up[l], ffn1_w_down[l], ffn1_post_g[l])
        qkv0, qkv1, qkv2, qm, km, vm, gates = _inproj(xt, seq, mix_pre_g[l], w_in[l], conv_w[l], conv_b[l])
        outs, lses = zip(*(_attention_group(qkv, batch, seq, dil)
                           for qkv, dil in zip((qkv0, qkv1, qkv2), GROUP_DILATIONS)))
        hm = _mlstm(qm, km, vm, gates, batch, seq, mlstm_i_bias[l], mlstm_f_bias[l], mlstm_head_g[l])
        xt = _merge(xt, outs, lses, hm, mix_pre_g[l], w_in[l], w_att_branch[l], w_mlstm_branch[l], w_out[l],
                    mix_post_g[l])
        xt = _ffn(xt, ffn2_pre_g[l], ffn2_w_gate[l], ffn2_w_up[l], ffn2_w_down[l], ffn2_post_g[l])
    return xt.reshape(batch, seq, d)
```

```python
import functools
import math

import numpy as np
import jax
import jax.numpy as jnp
from jax import lax
from jax.experimental import pallas as pl
from jax.experimental.pallas import tpu as pltpu

D_MODEL = 1024
N_GROUPS = 3
GROUP_DILATIONS = (1, 4, 16)
ATT_SPAN = 128
GROUP_HEADS = 4
HEAD_DIM = 128
GROUP_WIDTH = GROUP_HEADS * HEAD_DIM
ATT_WIDTH = N_GROUPS * GROUP_WIDTH
ATT_BLOCK = 128
ROPE_THETA = 500000.0
ROPE_DIMS = HEAD_DIM // 4
ROPE_HALF = ROPE_DIMS // 2
ROPE_GAP = HEAD_DIM // 2 - ROPE_HALF
M_HEADS = 4
M_WIDTH = D_MODEL
M_HEAD_DIM = M_WIDTH // M_HEADS
M_CHUNK = 256
ATT_BLOCKS_PER_ITER = 16
M_BATCH = 4
CONV_WIDTH = 4
CONV_HALO = 8
CONV_COLS = 256
RMS_EPS = 1e-6
LANES = 128
MASK_BIAS = -1e30
VMEM_LIMIT_BYTES = 58 * 1024 * 1024
TOKEN_TILE = 512
MERGE_SUBTILES = 2
FFN_SUBTILES = 2
STAGE_SHAPE = (6, 256, 1024)

_BF16 = jnp.bfloat16
_F32 = jnp.float32
_NT = (((1,), (1,)), ((), ()))
_TN = (((0,), (0,)), ((), ()))


def _rms(x):
    return x * lax.rsqrt(jnp.mean(x * x, axis=-1, keepdims=True) + RMS_EPS)


def _dot(a, b):
    return jnp.dot(a, b, preferred_element_type=_F32)


def _resident(shape):
    return pl.BlockSpec(shape, lambda *_: (0,) * len(shape), pipeline_mode=pl.Buffered(1))


def _params(*semantics):
    return pltpu.CompilerParams(dimension_semantics=semantics, vmem_limit_bytes=VMEM_LIMIT_BYTES)


def _cast_weight_in(src_hbm, dst_ref, stage_ref, sem_ref, jobs=None, scale=None):
    slots, stage_rows, stage_cols = stage_ref.shape
    if jobs is None:
        rows, cols = dst_ref.shape
        jobs = [(r, c, r, c, min(stage_rows, rows - r), min(stage_cols, cols - c))
                for r in range(0, rows, stage_rows) for c in range(0, cols, stage_cols)]

    def copy(k):
        src_row, src_col, _, _, nr, nc = jobs[k]
        return pltpu.make_async_copy(src_hbm.at[pl.ds(src_row, nr), pl.ds(src_col, nc)],
                                     stage_ref.at[k % slots, pl.ds(0, nr), pl.ds(0, nc)], sem_ref.at[k % slots])

    for k in range(min(slots - 1, len(jobs))):
        copy(k).start()
    for k, (_, _, dst_row, dst_col, nr, nc) in enumerate(jobs):
        if k + slots - 1 < len(jobs):
            copy(k + slots - 1).start()
        copy(k).wait()
        block = stage_ref[k % slots, 0:nr, 0:nc]
        dst_ref[dst_row:dst_row + nr, dst_col:dst_col + nc] = (block if scale is None else block * scale).astype(_BF16)


def _row_jobs(windows, stage_rows):
    return [(src + r, 0, dst + r, 0, min(stage_rows, n - r), STAGE_SHAPE[2])
            for src, dst, n in windows for r in range(0, n, stage_rows)]


def _ffn_kernel(x_ref, pre_g_ref, post_g_ref, wg_hbm, wu_hbm, wd_hbm, o_ref,
                wg_ref, wu_ref, wd_ref, stage_ref, sem_ref):
    @pl.when(pl.program_id(0) == 0)
    def _():
        _cast_weight_in(wg_hbm, wg_ref, stage_ref, sem_ref, scale=0.5)
        _cast_weight_in(wu_hbm, wu_ref, stage_ref, sem_ref)
        _cast_weight_in(wd_hbm, wd_ref, stage_ref, sem_ref)

    for sub in range(x_ref.shape[0] // TOKEN_TILE):
        rows = slice(sub * TOKEN_TILE, (sub + 1) * TOKEN_TILE)
        x = x_ref[rows, :]
        h = (_rms(x) * pre_g_ref[...]).astype(_BF16)
        g = _dot(h, wg_ref[...])
        u = _dot(h, wu_ref[...])
        a = ((g + g * jnp.tanh(g)) * u).astype(_BF16)
        f = _dot(a, wd_ref[...])
        o_ref[rows, :] = x + _rms(f) * post_g_ref[...]


def _weight_scratch(*shapes):
    return ([pltpu.VMEM(shape, _BF16) for shape in shapes]
            + [pltpu.VMEM(STAGE_SHAPE, _F32), pltpu.SemaphoreType.DMA((STAGE_SHAPE[0],))])


_HBM = pl.BlockSpec(memory_space=pl.ANY)


def _ffn(x2d, pre_g, w_gate, w_up, w_down, post_g):
    t, d = x2d.shape
    row = pl.BlockSpec((FFN_SUBTILES * TOKEN_TILE, d), lambda i: (i, 0))
    return pl.pallas_call(
        _ffn_kernel,
        grid=(t // (FFN_SUBTILES * TOKEN_TILE),),
        in_specs=[row, _resident((1, d)), _resident((1, d)), _HBM, _HBM, _HBM],
        out_specs=row,
        out_shape=jax.ShapeDtypeStruct((t, d), _F32),
        scratch_shapes=_weight_scratch(w_gate.shape, w_up.shape, w_down.shape),
        compiler_params=_params("arbitrary"),
        name="ffn",
    )(x2d, pre_g.reshape(1, d), 0.5 * post_g.reshape(1, d), w_gate, w_up, w_down)


def _inproj_kernel(x_ref, g_ref, w_hbm, tab0_ref, tab1_ref, tab2_ref, conv_w_ref, conv_b_ref,
                   qkv0_ref, qkv1_ref, qkv2_ref, qm_ref, km_ref, vm_ref, gates_ref,
                   w_ref, stage_ref, sem_ref, hn_ref, hs_ref, hs4_ref, hp1_ref, hp2_ref, xs_ref, halo_ref, *, tiles_per_seq):
    tm = TOKEN_TILE
    n_slabs = D_MODEL // LANES
    conv_col0 = N_GROUPS * ATT_WIDTH
    vm_col0 = conv_col0 + 2 * M_WIDTH
    gate_col0 = vm_col0 + M_WIDTH

    def project(h_ref, col, width):
        return lax.dot_general(h_ref[...], w_ref[col:col + width, :], _NT, preferred_element_type=_F32)

    @pl.when(pl.program_id(0) == 0)
    def _():
        w_ref[gate_col0 + 2 * M_HEADS:gate_col0 + LANES, :] = jnp.zeros((LANES - 2 * M_HEADS, D_MODEL), _BF16)
        _cast_weight_in(w_hbm, w_ref, stage_ref, sem_ref, _row_jobs(_inproj_windows(), STAGE_SHAPE[1]))
        for g in range(N_GROUPS):
            for part in range(2):
                for hh in range(GROUP_HEADS):
                    base = (g * 3 + part) * GROUP_WIDTH + hh * HEAD_DIM
                    head = w_ref[base:base + HEAD_DIM, :]
                    w_ref[base:base + HEAD_DIM, :] = jnp.concatenate(
                        [head[:ROPE_HALF], head[ROPE_DIMS:ROPE_DIMS + ROPE_GAP], head[ROPE_HALF:ROPE_DIMS],
                         head[ROPE_DIMS + ROPE_GAP:]], axis=0)

    @pl.when(pl.program_id(0) % tiles_per_seq == 0)
    def _():
        halo_ref[...] = jnp.zeros(halo_ref.shape, _F32)

    hf = _rms(x_ref[...]) * g_ref[...]
    hn_ref[...] = hf.astype(_BF16)
    for c in range(n_slabs):
        hs_ref[c] = hf[:, c * LANES:(c + 1) * LANES]


    def regroup(hp_ref, d):
        rows = tm // d
        quarter = tm // GROUP_DILATIONS[1]
        for r in range(d):
            if d == GROUP_DILATIONS[1]:
                pieces = [hs_ref[c, pl.ds(r, rows, stride=d), :] for c in range(n_slabs)]
                for c in range(n_slabs):
                    hs4_ref[c, r * rows:(r + 1) * rows, :] = pieces[c]
            else:
                start = (r % GROUP_DILATIONS[1]) * quarter + r // GROUP_DILATIONS[1]
                pieces = [hs4_ref[c, pl.ds(start, rows, stride=GROUP_DILATIONS[1]), :] for c in range(n_slabs)]
            hp_ref[r * rows:(r + 1) * rows, :] = jnp.concatenate(pieces, axis=1).astype(_BF16)

    def attention_part(g, part, h_ref):
        return project(h_ref, (g * 3 + part) * GROUP_WIDTH, GROUP_WIDTH)

    def attention_store(p, part, qkv_ref, tab_ref, d):
        rows = tm // d
        for hh in range(GROUP_HEADS):
            x = p[:, hh * HEAD_DIM:(hh + 1) * HEAD_DIM]
            if part < 2:
                x = x * tab_ref[2 * part] + pltpu.roll(x, HEAD_DIM // 2, 1) * tab_ref[2 * part + 1]
            x = x.astype(_BF16)
            for r in range(d):
                lo = r * ATT_WIDTH + part * GROUP_WIDTH + hh * HEAD_DIM
                qkv_ref[:, lo:lo + HEAD_DIM] = x[r * rows:(r + 1) * rows, :]

    def conv_dot(cc):
        return project(hn_ref, conv_col0 + cc * CONV_COLS, CONV_COLS)

    def conv_store(p, cc):
        for c in range(CONV_COLS // LANES):
            lanes = slice(cc * CONV_COLS + c * LANES, cc * CONV_COLS + (c + 1) * LANES)
            pc = p[:, c * LANES:(c + 1) * LANES]
            xs_ref[c, pl.ds(0, CONV_HALO, stride=2), :] = halo_ref[:, lanes]
            xs_ref[c, pl.ds(2 * CONV_HALO, tm, stride=2), :] = pc
            halo_ref[:, lanes] = pc[tm - CONV_HALO:tm, :]
            y = conv_b_ref[:, lanes] + conv_w_ref[CONV_WIDTH - 1:CONV_WIDTH, lanes] * pc
            for j in range(CONV_WIDTH - 1):
                shift = CONV_WIDTH - 1 - j
                y = y + conv_w_ref[j:j + 1, lanes] * xs_ref[c, pl.ds(2 * (CONV_HALO - shift), tm, stride=2), :]
            qk = y + y * jnp.tanh(y)
            if lanes.start < M_WIDTH:
                qm_ref[:, lanes] = qk.astype(_BF16)
            else:
                km_ref[:, lanes.start - M_WIDTH:lanes.stop - M_WIDTH] = (qk * M_HEAD_DIM ** -0.5).astype(_BF16)

    def vm_store(p):
        vm_ref[...] = p.astype(_BF16)

    def gates_store(p):
        gates_ref[...] = p

    hps = (hn_ref, hp1_ref, hp2_ref)
    qkvs = (qkv0_ref, qkv1_ref, qkv2_ref)
    tabs = (tab0_ref, tab1_ref, tab2_ref)
    attention = [(functools.partial(attention_part, g, part, hps[g]),
                  functools.partial(attention_store, part=part, qkv_ref=qkvs[g], tab_ref=tabs[g], d=GROUP_DILATIONS[g]))
                 for g in range(N_GROUPS) for part in range(3)]
    convs = [(functools.partial(conv_dot, cc), functools.partial(conv_store, cc=cc))
             for cc in range(2 * M_WIDTH // CONV_COLS)]
    sections = [(lambda: project(hn_ref, vm_col0, M_WIDTH), vm_store),
                (lambda: project(hn_ref, gate_col0, LANES),
                 lambda p: (gates_store(p), regroup(hp1_ref, GROUP_DILATIONS[1]))),
                attention[0], convs[0], attention[1], convs[1], attention[2],
                (convs[2][0], lambda p: (conv_store(p, 2), regroup(hp2_ref, GROUP_DILATIONS[2]))),
                attention[3], convs[3], attention[4], convs[4], attention[5], convs[5],
                attention[6], convs[6], attention[7], convs[7], attention[8]]
    pending = None
    for matmul, epilogue in sections:
        p = matmul()
        if pending is not None:
            pending()
        pending = functools.partial(epilogue, p)
    pending()


def _inproj_windows():
    a, m = ATT_WIDTH, M_WIDTH
    wins = []
    for g in range(N_GROUPS):
        for part in range(3):
            wins.append((part * a + g * GROUP_WIDTH, (g * 3 + part) * GROUP_WIDTH, GROUP_WIDTH))
    base = 3 * a
    wins.append((base, base, 3 * m))
    wins.append((base + 4 * m, base + 3 * m, 2 * M_HEADS))
    return wins


def _rope_tables(seq, dilation):
    inv_freq = np.power(ROPE_THETA, -(np.arange(ROPE_HALF, dtype=np.float64) * 2.0 / ROPE_DIMS))
    ang = np.arange(seq, dtype=np.float64)[:, None] * inv_freq[None, :]
    cos, sin = np.cos(ang), np.sin(ang)
    ones = np.ones((seq, ROPE_GAP))
    zeros = np.zeros((seq, ROPE_GAP))
    cos_tab = np.concatenate([cos, ones, cos, ones], axis=-1)
    sin_tab = np.concatenate([-sin, zeros, sin, zeros], axis=-1)
    q_scale = HEAD_DIM ** -0.5 * math.log2(math.e)
    tabs = np.stack([cos_tab * q_scale, sin_tab * q_scale, cos_tab, sin_tab])
    tabs = tabs.reshape(4, seq // TOKEN_TILE, TOKEN_TILE // dilation, dilation, HEAD_DIM)
    return jnp.asarray(tabs.transpose(0, 1, 3, 2, 4).reshape(4, seq, HEAD_DIM), _F32)


def _inproj(x2d, seq, pre_g, w_in, conv_w, conv_b):
    t, d = x2d.shape
    tiles_per_seq = seq // TOKEN_TILE
    w_cols = N_GROUPS * ATT_WIDTH + 3 * M_WIDTH + LANES
    tables = [_rope_tables(seq, dil) for dil in GROUP_DILATIONS]
    row = lambda width: pl.BlockSpec((TOKEN_TILE, width), lambda i: (i, 0))
    cls = lambda dil: pl.BlockSpec((TOKEN_TILE // dil, dil * ATT_WIDTH), lambda i: (i, 0))
    tab = pl.BlockSpec((4, TOKEN_TILE, HEAD_DIM), lambda i: (0, i % tiles_per_seq, 0))
    out_shape = [jax.ShapeDtypeStruct((t // dil, dil * ATT_WIDTH), _BF16) for dil in GROUP_DILATIONS]
    out_shape += [jax.ShapeDtypeStruct((t, M_WIDTH), _BF16)] * 3
    out_shape.append(jax.ShapeDtypeStruct((t, LANES), _F32))
    return pl.pallas_call(
        functools.partial(_inproj_kernel, tiles_per_seq=tiles_per_seq),
        grid=(t // TOKEN_TILE,),
        in_specs=[row(d), _resident((1, d)), _HBM, tab, tab, tab,
                  _resident((CONV_WIDTH, 2 * M_WIDTH)), _resident((1, 2 * M_WIDTH))],
        out_specs=[cls(dil) for dil in GROUP_DILATIONS] + [row(M_WIDTH)] * 3 + [row(LANES)],
        out_shape=out_shape,
        scratch_shapes=_weight_scratch((w_cols, d)) + [
            pltpu.VMEM((TOKEN_TILE, d), _BF16),
            pltpu.VMEM((d // LANES, TOKEN_TILE, LANES), _F32),
            pltpu.VMEM((d // LANES, TOKEN_TILE, LANES), _F32),
            pltpu.VMEM((TOKEN_TILE, d), _BF16),
            pltpu.VMEM((TOKEN_TILE, d), _BF16),
            pltpu.VMEM((CONV_COLS // LANES, 2 * (CONV_HALO + TOKEN_TILE), LANES), _F32),
            pltpu.VMEM((CONV_HALO, 2 * M_WIDTH), _F32)],
        compiler_params=_params("arbitrary"),
        name="inproj",
    )(x2d, pre_g.reshape(1, d), w_in.T, *tables, 0.5 * conv_w, 0.5 * conv_b.reshape(1, -1))


def _attn_kernel(qkv_ref, bias_two_ref, bias_one_ref, o_ref, lse_ref, *, dilation, n_blocks):
    lane = lax.broadcasted_iota(jnp.int32, (ATT_BLOCK, LANES), 1)

    def block(row0, key0, n_keys, bias_ref):
        ones = jnp.ones((n_keys, HEAD_DIM), _BF16)
        rows = pl.ds(row0, ATT_BLOCK)
        for r in range(dilation):
            scores = []
            for hh in range(GROUP_HEADS):
                qc = r * ATT_WIDTH + hh * HEAD_DIM
                q = qkv_ref[rows, qc:qc + HEAD_DIM]
                k = qkv_ref[pl.ds(key0, n_keys), qc + GROUP_WIDTH:qc + GROUP_WIDTH + HEAD_DIM]
                scores.append(lax.dot_general(q, k, _NT, preferred_element_type=_F32) + bias_ref[...])
            tops = [jnp.max(s, axis=1, keepdims=True) for s in scores]
            probs = [jnp.exp2(s - m).astype(_BF16) for s, m in zip(scores, tops)]
            lse_tile = jnp.zeros((ATT_BLOCK, LANES), _F32)
            for hh in range(GROUP_HEADS):
                vc = r * ATT_WIDTH + 2 * GROUP_WIDTH + hh * HEAD_DIM
                v = qkv_ref[pl.ds(key0, n_keys), vc:vc + HEAD_DIM]
                acc = _dot(probs[hh], jnp.concatenate([v, ones], axis=1))
                denom = acc[:, HEAD_DIM:]
                o = acc[:, :HEAD_DIM] * (1.0 / denom)
                lse_tile = jnp.where(lane == hh, tops[hh] + jnp.log2(denom), lse_tile)
                oc = r * GROUP_WIDTH + hh * HEAD_DIM
                o_ref[rows, oc:oc + HEAD_DIM] = o.astype(_BF16)
            lse_ref[rows, r * LANES:(r + 1) * LANES] = lse_tile * math.log(2.0)

    def two_key_block(i):
        row0 = i * ATT_BLOCK
        if not isinstance(i, int):
            row0 = pl.multiple_of(row0, ATT_BLOCK)
        block(row0, row0 - ATT_BLOCK, 2 * ATT_BLOCK, bias_two_ref)

    block(0, 0, ATT_BLOCK, bias_one_ref)
    per_iter = min(n_blocks, ATT_BLOCKS_PER_ITER)
    for i in range(1, min(per_iter, n_blocks)):
        two_key_block(i)

    def body(j, carry):
        for i in range(per_iter):
            two_key_block(j * per_iter + i)
        return carry

    if n_blocks > per_iter:
        lax.fori_loop(1, n_blocks // per_iter, body, 0)


def _band_bias():
    qi = np.arange(ATT_BLOCK)[:, None]
    kj = np.arange(2 * ATT_BLOCK)[None, :]
    valid_two = np.where(kj < ATT_BLOCK, kj >= qi, kj - ATT_BLOCK <= qi)
    valid_one = np.arange(ATT_BLOCK)[None, :] <= qi
    to_bias = lambda valid: jnp.asarray(np.where(valid, 0.0, MASK_BIAS), _F32)
    return to_bias(valid_two), to_bias(valid_one)


def _attention_group(qkv, batch, seq, dilation):
    length = seq // dilation
    view = qkv.reshape(batch, length, dilation * ATT_WIDTH)
    bias_two, bias_one = _band_bias()
    slab = lambda width: pl.BlockSpec((None, length, dilation * width), lambda b: (b, 0, 0))
    o, lse = pl.pallas_call(
        functools.partial(_attn_kernel, dilation=dilation, n_blocks=length // ATT_BLOCK),
        grid=(batch,),
        in_specs=[slab(ATT_WIDTH), _resident(bias_two.shape), _resident(bias_one.shape)],
        out_specs=[slab(GROUP_WIDTH), slab(LANES)],
        out_shape=[jax.ShapeDtypeStruct((batch, length, dilation * GROUP_WIDTH), _BF16),
                   jax.ShapeDtypeStruct((batch, length, dilation * LANES), _F32)],
        compiler_params=_params("parallel"),
        name=f"attn_d{dilation}",
    )(view, bias_two, bias_one)
    return (o.reshape(batch * length, dilation * GROUP_WIDTH), lse.reshape(batch * length, dilation * LANES))


def _mlstm_kernel(qm_ref, km_ref, vm_ref, gates_ref, gate_b_ref, head_g_ref, o_ref, state_ref, m_ref):
    lc = M_CHUNK
    dh = M_HEAD_DIM
    wide = lambda a, n: jnp.concatenate([a] * n, axis=1)

    @pl.when(pl.program_id(1) == 0)
    def _():
        state_ref[...] = jnp.zeros(state_ref.shape, _F32)
        m_ref[...] = jnp.zeros(m_ref.shape, _F32)

    causal = (lax.broadcasted_iota(jnp.int32, (lc, lc), 1) <= lax.broadcasted_iota(jnp.int32, (lc, lc), 0))
    tri_t = (lax.broadcasted_iota(jnp.int32, (lc, lc), 0) <= lax.broadcasted_iota(jnp.int32, (lc, lc), 1)).astype(_BF16)
    ones = jnp.ones((lc, LANES), _BF16)

    for bb in range(M_BATCH):
        gt = (gates_ref[bb] + gate_b_ref[...]).T[0:2 * M_HEADS, :]
        lf = (jnp.minimum(gt, 0.0) - jnp.log(1.0 + jnp.exp(-jnp.abs(gt)))) * math.log2(math.e)
        gt = gt * math.log2(math.e)
        hi = lf.astype(_BF16)
        rest = lf - hi.astype(_F32)
        mid = rest.astype(_BF16)
        low = (rest - mid.astype(_F32)).astype(_BF16)
        b_all = _dot(hi, tri_t) + _dot(mid, tri_t) + _dot(low, tri_t)
        gate_row = lax.broadcasted_iota(jnp.int32, (2 * M_HEADS, lc), 0)
        ib_t = jnp.where(gate_row < M_HEADS, gt, b_all)
        ib = jnp.concatenate([ib_t, jnp.zeros((LANES - 2 * M_HEADS, lc), _F32)], axis=0).T

        for hh in range(M_HEADS):
            hs = slice(hh * dh, (hh + 1) * dh)
            q = qm_ref[bb, :, hs]
            k = km_ref[bb, :, hs]
            v_aug = jnp.concatenate([vm_ref[bb, :, hs], ones], axis=1)
            i_rep = jnp.broadcast_to(ib[:, hh:hh + 1], (lc, LANES))
            b_rep = jnp.broadcast_to(ib[:, M_HEADS + hh:M_HEADS + hh + 1], (lc, LANES))
            u_row = ib_t[hh:hh + 1, :] - ib_t[M_HEADS + hh:M_HEADS + hh + 1, :]
            m_prev = m_ref[bb, hh][0:1, :]
            state = state_ref[bb, hh]

            e = jnp.where(causal, u_row, -jnp.inf)
            big_m = jnp.maximum(m_prev, jnp.broadcast_to(jnp.max(e, axis=1, keepdims=True), (lc, LANES)))
            w = jnp.exp2(e - wide(big_m, lc // LANES)) * lax.dot_general(q, k, _NT, preferred_element_type=_F32)
            inter = jnp.exp2(m_prev - big_m)
            acc = _dot(w.astype(_BF16), v_aug) + wide(inter, dh // LANES + 1) * _dot(q, state.astype(_BF16))
            den = acc[:, dh:]
            inv = 1.0 / jnp.maximum(jnp.abs(den), jnp.exp2(-(b_rep + big_m)))
            h_out = acc[:, :dh] * wide(inv, dh // LANES)

            b_last = b_rep[lc - 1:lc, :]
            g = b_last - b_rep + i_rep
            m_new = jnp.maximum(b_last + m_prev, jnp.max(g, axis=0, keepdims=True))
            a = jnp.exp2(g - m_new)
            decay = jnp.exp2(b_last + m_prev - m_new)
            av = (wide(a, dh // LANES + 1) * v_aug.astype(_F32)).astype(_BF16)
            state_ref[bb, hh] = (wide(decay, dh // LANES + 1) * state
                                 + lax.dot_general(k, av, _TN, preferred_element_type=_F32))
            m_ref[bb, hh] = jnp.broadcast_to(m_new, m_ref.shape[2:])

            o_ref[bb, :, hs] = (_rms(h_out) * head_g_ref[:, hs]).astype(_BF16)


def _mlstm(qm, km, vm, gates, batch, seq, i_bias, f_bias, head_g):
    n_chunks = seq // M_CHUNK
    gate_b = jnp.pad(jnp.concatenate([i_bias, f_bias]), (0, LANES - 2 * M_HEADS)).reshape(1, LANES)
    chunk = lambda width: pl.BlockSpec((M_BATCH, M_CHUNK, width), lambda b, c: (b, c, 0))
    out = pl.pallas_call(
        _mlstm_kernel,
        grid=(batch // M_BATCH, n_chunks),
        in_specs=[chunk(M_WIDTH), chunk(M_WIDTH), chunk(M_WIDTH), chunk(LANES),
                  _resident((1, LANES)), _resident((1, M_WIDTH))],
        out_specs=chunk(M_WIDTH),
        out_shape=jax.ShapeDtypeStruct((batch, seq, M_WIDTH), _BF16),
        scratch_shapes=[pltpu.VMEM((M_BATCH, M_HEADS, M_HEAD_DIM, M_HEAD_DIM + LANES), _F32),
                        pltpu.VMEM((M_BATCH, M_HEADS, 8, LANES), _F32)],
        compiler_params=_params("parallel", "arbitrary"),
        name="mlstm",
    )(qm.reshape(batch, seq, M_WIDTH), km.reshape(batch, seq, M_WIDTH), vm.reshape(batch, seq, M_WIDTH),
      gates.reshape(batch, seq, LANES), gate_b, head_g.reshape(1, -1))
    return out.reshape(batch * seq, M_WIDTH)


def _merge_kernel(x_ref, o0_ref, o1_ref, o2_ref, l0_ref, l1_ref, l2_ref, hm_ref, pre_g_ref, post_g_ref,
                  w_in_hbm, wa_hbm, wm_hbm, wo_hbm, out_ref,
                  wg_ref, wa_ref, wm_ref, wo_ref, stage_ref, sem_ref, o_tok_ref, lse_tok_ref):
    def gate(h, col):
        return jnp.tanh(lax.dot_general(h, wg_ref[col:col + D_MODEL, :], _NT, preferred_element_type=_F32))

    @pl.when(pl.program_id(0) == 0)
    def _():
        o_col0 = N_GROUPS * ATT_WIDTH + 3 * M_WIDTH
        branch_col0 = o_col0 + M_WIDTH + 2 * M_HEADS
        _cast_weight_in(w_in_hbm, wg_ref, stage_ref, sem_ref,
                        _row_jobs([(o_col0, 0, M_WIDTH), (branch_col0, M_WIDTH, 2 * D_MODEL)], STAGE_SHAPE[1]),
                        scale=0.5)
        _cast_weight_in(wa_hbm, wa_ref, stage_ref, sem_ref, scale=0.5)
        _cast_weight_in(wm_hbm, wm_ref, stage_ref, sem_ref, scale=0.25)
        _cast_weight_in(wo_hbm, wo_ref, stage_ref, sem_ref)

    for sub in range(x_ref.shape[0] // TOKEN_TILE):
        tok = slice(sub * TOKEN_TILE, (sub + 1) * TOKEN_TILE)
        x = x_ref[tok, :]
        h = (_rms(x) * pre_g_ref[...]).astype(_BF16)
        gate_o = gate(h, 0)
        gate_a = gate(h, M_WIDTH)
        gate_m = gate(h, M_WIDTH + D_MODEL)
        for g, (o_ref, l_ref, d) in enumerate(zip((o0_ref, o1_ref, o2_ref), (l0_ref, l1_ref, l2_ref),
                                                   GROUP_DILATIONS)):
            rows = TOKEN_TILE // d
            cls_rows = slice(sub * rows, (sub + 1) * rows)
            for r in range(d):
                where = pl.ds(r, rows, stride=d) if d > 1 else slice(None)
                lse_tok_ref[g, where, :] = l_ref[cls_rows, r * LANES:(r + 1) * LANES]
                for hh in range(GROUP_HEADS):
                    col = r * GROUP_WIDTH + hh * HEAD_DIM
                    o_tok_ref[g * GROUP_HEADS + hh, where, :] = o_ref[cls_rows, col:col + HEAD_DIM].astype(_F32)
        lses = [lse_tok_ref[g] for g in range(N_GROUPS)]
        top = jnp.maximum(jnp.maximum(lses[0], lses[1]), lses[2])
        es = [jnp.exp(l - top) for l in lses]
        inv = 1.0 / (es[0] + es[1] + es[2])
        alphas = [e * inv for e in es]
        heads = []
        for hh in range(GROUP_HEADS):
            acc = None
            for g, alpha in enumerate(alphas):
                term = alpha[:, hh:hh + 1] * o_tok_ref[g * GROUP_HEADS + hh]
                acc = term if acc is None else acc + term
            heads.append(acc.astype(_BF16))
        att = jnp.concatenate(heads, axis=1)
        half_a = _dot(att, wa_ref[...])
        a = half_a + gate_a * half_a
        hm = hm_ref[tok, :].astype(_F32)
        ml = (hm + gate_o * hm).astype(_BF16)
        half_m = _dot(ml, wm_ref[...])
        m = half_m + gate_m * half_m
        y = _dot((a + m).astype(_BF16), wo_ref[...])
        out_ref[tok, :] = x + _rms(y) * post_g_ref[...]


def _merge(x2d, outs, lses, hm, pre_g, w_in, w_att, w_ml, w_out, post_g):
    t, d = x2d.shape
    step = MERGE_SUBTILES * TOKEN_TILE
    row = lambda width: pl.BlockSpec((step, width), lambda i: (i, 0))
    cls = lambda width: [pl.BlockSpec((step // dil, dil * width), lambda i: (i, 0)) for dil in GROUP_DILATIONS]
    return pl.pallas_call(
        _merge_kernel,
        grid=(t // step,),
        in_specs=[row(d)] + cls(GROUP_WIDTH) + cls(LANES) + [row(M_WIDTH), _resident((1, d)),
                  _resident((1, d))] + [_HBM] * 4,
        out_specs=row(d),
        out_shape=jax.ShapeDtypeStruct((t, d), _F32),
        scratch_shapes=_weight_scratch((M_WIDTH + 2 * D_MODEL, d), w_att.shape, w_ml.shape, w_out.shape) + [
            pltpu.VMEM((N_GROUPS * GROUP_HEADS, TOKEN_TILE, HEAD_DIM), _F32),
            pltpu.VMEM((N_GROUPS, TOKEN_TILE, LANES), _F32)],
        compiler_params=_params("arbitrary"),
        name="merge",
    )(x2d, *outs, *lses, hm, pre_g.reshape(1, d), post_g.reshape(1, d), w_in.T, w_att, w_ml, w_out)


def kernel(x, ffn1_pre_g, ffn1_w_gate, ffn1_w_up, ffn1_w_down, ffn1_post_g, mix_pre_g, w_in, conv_w, conv_b, mlstm_i_bias, mlstm_f_bias, mlstm_head_g, w_att_branch, w_mlstm_branch, w_out, mix_post_g, ffn2_pre_g, ffn2_w_gate, ffn2_w_up, ffn2_w_down, ffn2_post_g):
    batch, seq, d = x.shape
    t = batch * seq
    xt = x.reshape(t, d)
    for l in range(ffn1_pre_g.shape[0]):
        xt = _ffn(xt, ffn1_pre_g[l], ffn1_w_gate[l], ffn1_w_up[l], ffn1_w_down[l], ffn1_post_g[l])
        qkv0, qkv1, qkv2, qm, km, vm, gates = _inproj(xt, seq, mix_pre_g[l], w_in[l], conv_w[l], conv_b[l])
        outs, lses = zip(*(_attention_group(qkv, batch, seq, dil)
                           for qkv, dil in zip((qkv0, qkv1, qkv2), GROUP_DILATIONS)))
        hm = _mlstm(qm, km, vm, gates, batch, seq, mlstm_i_bias[l], mlstm_f_bias[l], mlstm_head_g[l])
        xt = _merge(xt, outs, lses, hm, mix_pre_g[l], w_in[l], w_att_branch[l], w_mlstm_branch[l], w_out[l],
                    mix_post_g[l])
        xt = _ffn(xt, ffn2_pre_g[l], ffn2_w_gate[l], ffn2_w_up[l], ffn2_w_down[l], ffn2_post_g[l])
    return xt.reshape(batch, seq, d)
```
